```python
import math
import jax, jax.numpy as jnp
from jax import lax
import numpy as np

D_MODEL = 1024
BATCH = 8
SEQ = 2048
DEPTH = 2
DEC_BATCH = 128
DEC_SEQ = 1
PAST_LEN = 16384
PAGE_SIZE = 128

F32 = jnp.float32
N_EVEN = (DEPTH + 1) // 2
N_ODD = DEPTH // 2
NORM_EPS = 1e-6
CONV_W = 4
LRU_WIDTH = D_MODEL // 2
LRU_HEADS = 8
LRU_BLOCK = LRU_WIDTH // LRU_HEADS
LRU_C = 8.0
RWKV_WIDTH = D_MODEL // 2
RWKV_HEAD = 64
RWKV_HEADS = RWKV_WIDTH // RWKV_HEAD
DECAY_RANK = 64
ICLR_RANK = 64
GATE_RANK = 128
RWKV_GN_EPS = 64e-5
SHIFT_COLS = 3 * RWKV_WIDTH + DECAY_RANK + ICLR_RANK + GATE_RANK
RWKV_SPLITS = (RWKV_WIDTH, 2 * RWKV_WIDTH, 3 * RWKV_WIDTH, 3 * RWKV_WIDTH + DECAY_RANK, 3 * RWKV_WIDTH + DECAY_RANK + ICLR_RANK)
PROJ0_COLS = 2 * LRU_WIDTH + SHIFT_COLS
MIX0_WIDTH = LRU_WIDTH + RWKV_WIDTH
SSM_INNER = 2 * D_MODEL
SSM_HEAD = 64
SSM_HEADS = SSM_INNER // SSM_HEAD
SSM_GROUPS = 8
SSM_HPG = SSM_HEADS // SSM_GROUPS
SSM_STATE = 128
SSM_CHUNK = 128
SSM_GN = SSM_GROUPS * SSM_STATE
SSM_CONV_CH = SSM_INNER + 2 * SSM_GN
PROJ1_COLS = SSM_INNER + SSM_CONV_CH + SSM_HEADS
SSM_NORM_EPS = 1e-5
D_FF = 3 * D_MODEL
N_EXPERTS = 8
TOP_K = 2
D_FF_EXPERT = 7 * D_MODEL // 2
MOE_BLOCK = 128

kernel_name = 'hybrid_rglru_rwkv7_mamba2_moe_step'


def rmsnorm(x, g):
    xf = x.astype(F32)
    y = xf * lax.rsqrt(jnp.mean(xf * xf, axis=-1, keepdims=True) + NORM_EPS)
    return (y * g.astype(F32)).astype(x.dtype)


def causal_conv(u, buf, w, b):
    L = u.shape[1]
    ext = jnp.concatenate([buf.astype(u.dtype), u], axis=1)
    y = ext[:, 0:L] * w[0]
    for tap in range(1, CONV_W):
        y = y + ext[:, tap:tap + L] * w[tap]
    return y + b, ext[:, -(CONV_W - 1):]


def token_shift(p, prev, mu):
    p_prev = jnp.concatenate([prev[:, None].astype(p.dtype), p[:, :-1]], axis=1)
    return p + (p_prev - p) * mu, p[:, -1]


def rg_lru(x, h0, w_a, b_a, w_x, b_x, lam):
    bsz, L, _ = x.shape
    xf = x.astype(F32)
    xh = xf.reshape(bsz, L, LRU_HEADS, LRU_BLOCK)
    r = jax.nn.sigmoid(jnp.einsum('blhi,hij->blhj', xh, w_a) + b_a).reshape(bsz, L, LRU_WIDTH)
    i = jax.nn.sigmoid(jnp.einsum('blhi,hij->blhj', xh, w_x) + b_x).reshape(bsz, L, LRU_WIDTH)
    log_a = -LRU_C * r * jax.nn.softplus(-lam.astype(F32))
    a = jnp.exp(log_a)
    u = jnp.sqrt(-jnp.expm1(2.0 * log_a)) * (i * xf)

    def step(h, au):
        a_t, u_t = au
        h = a_t * h + u_t
        return h, h

    hT, hs = lax.scan(step, h0.astype(F32), (jnp.moveaxis(a, 1, 0), jnp.moveaxis(u, 1, 0)))
    return jnp.moveaxis(hs, 0, 1).astype(x.dtype), hT.astype(x.dtype)


def rwkv7_scan(r, w, k, v, a, b, S0):
    def step(S, inp):
        r_t, w_t, k_t, v_t, a_t, b_t = inp
        sa = jnp.einsum('bhvk,bhk->bhv', S, a_t)
        S = S * w_t[:, :, None, :] + sa[..., None] * b_t[:, :, None, :] + v_t[..., None] * k_t[:, :, None, :]
        o = jnp.einsum('bhvk,bhk->bhv', S, r_t)
        return S, o

    ST, o = lax.scan(step, S0, tuple(jnp.moveaxis(z, 1, 0) for z in (r, w, k, v, a, b)))
    return jnp.moveaxis(o, 0, 1), ST


def mixer_ab(xn, conv_buf, h0, shift_prev, wkv0, prm, j):
    bsz, L, _ = xn.shape
    proj = xn @ prm['w_in0'][j]
    x_lru = proj[..., :LRU_WIDTH]
    g_lru = proj[..., LRU_WIDTH:2 * LRU_WIDTH]
    p_rwkv = proj[..., 2 * LRU_WIDTH:]
    xc, new_conv = causal_conv(x_lru, conv_buf, prm['lru_conv_w'][j], prm['lru_conv_b'][j])
    hs, new_h = rg_lru(xc, h0, prm['lru_wa'][j], prm['lru_ba'][j], prm['lru_wx'][j], prm['lru_bx'][j], prm['lru_lambda'][j])
    out_a = hs * jax.nn.gelu(g_lru)
    mixed, new_shift = token_shift(p_rwkv, shift_prev, prm['rwkv_mu'][j])
    r, k, v, wd, ad, gd = jnp.split(mixed.astype(F32), RWKV_SPLITS, axis=-1)
    w_log = -jax.nn.softplus(-(prm['rwkv_w0'][j] + jnp.tanh(wd) @ prm['rwkv_w_decay_up'][j])) - 0.5
    decay = jnp.exp(-jnp.exp(w_log))
    iclr = jax.nn.sigmoid(prm['rwkv_a0'][j] + ad @ prm['rwkv_w_iclr_up'][j])
    gate = jax.nn.sigmoid(gd) @ prm['rwkv_w_gate_up'][j]

    def heads(z):
        return z.reshape(bsz, L, RWKV_HEADS, RWKV_HEAD)

    kk = heads(k * prm['rwkv_k_k'][j])
    kk = kk / jnp.maximum(jnp.sqrt(jnp.sum(kk * kk, axis=-1, keepdims=True)), 1e-12)
    k = k * (1.0 + (iclr - 1.0) * prm['rwkv_k_a'][j])
    rh, kh, vh, ah = heads(r), heads(k), heads(v), heads(iclr)
    o, new_wkv = rwkv7_scan(rh, heads(decay), kh, vh, -kk, kk * ah, wkv0.astype(F32))
    mu = jnp.mean(o, axis=-1, keepdims=True)
    var = jnp.mean(jnp.square(o - mu), axis=-1, keepdims=True)
    ln_w = prm['rwkv_ln_w'][j].reshape(RWKV_HEADS, RWKV_HEAD)
    ln_b = prm['rwkv_ln_b'][j].reshape(RWKV_HEADS, RWKV_HEAD)
    o = (o - mu) * lax.rsqrt(var + RWKV_GN_EPS) * ln_w + ln_b
    o = o + jnp.sum(rh * kh * prm['rwkv_r_k'][j], axis=-1, keepdims=True) * vh
    out_b = (o.reshape(bsz, L, RWKV_WIDTH) * gate).astype(xn.dtype)
    y = jnp.concatenate([out_a, out_b], axis=-1) @ prm['w_out0'][j]
    return y, new_conv, new_h, new_shift, new_wkv.astype(xn.dtype)


def ssd(x, dt, A, Bm, Cm, S0, chunk):
    bsz, L = x.shape[:2]
    nc = L // chunk
    xc = x.reshape(bsz, nc, chunk, SSM_GROUPS, SSM_HPG, SSM_HEAD)
    dtc = dt.reshape(bsz, nc, chunk, SSM_GROUPS, SSM_HPG)
    Bc = Bm.reshape(bsz, nc, chunk, SSM_GROUPS, SSM_STATE)
    Cc = Cm.reshape(bsz, nc, chunk, SSM_GROUPS, SSM_STATE)
    dA = jnp.cumsum(dtc * A, axis=2)
    xdt = xc * dtc[..., None]
    dA_t = jnp.moveaxis(dA, 2, -1)
    seg = dA_t[..., :, None] - dA_t[..., None, :]
    causal = jnp.tril(jnp.ones((chunk, chunk), bool))
    Lmat = jnp.exp(jnp.where(causal, seg, -jnp.inf))
    scores = jnp.einsum('bclgn,bcsgn->bcgls', Cc, Bc)
    y_diag = jnp.einsum('bcgqls,bcsgqp->bclgqp', scores[:, :, :, None] * Lmat, xdt)
    decay_to_end = jnp.exp(dA[:, :, -1:] - dA)
    chunk_states = jnp.einsum('bclgn,bclgq,bclgqp->bcgqpn', Bc, decay_to_end, xdt)
    chunk_decay = jnp.exp(dA[:, :, -1])

    def step(S, inp):
        dec, st = inp
        return S * dec[..., None, None] + st, S

    S0g = S0.astype(F32).reshape(bsz, SSM_GROUPS, SSM_HPG, SSM_HEAD, SSM_STATE)
    ST, S_start = lax.scan(step, S0g, (jnp.moveaxis(chunk_decay, 1, 0), jnp.moveaxis(chunk_states, 1, 0)))
    S_start = jnp.moveaxis(S_start, 0, 1)
    y_off = jnp.einsum('bclgn,bcgqpn,bclgq->bclgqp', Cc, S_start, jnp.exp(dA))
    y = (y_diag + y_off).reshape(bsz, L, SSM_GROUPS, SSM_HPG, SSM_HEAD)
    return y, ST.reshape(bsz, SSM_HEADS, SSM_HEAD, SSM_STATE)


def mamba2_mixer(xn, conv_buf, S0, prm, j):
    bsz, L, _ = xn.shape
    proj = xn @ prm['w_in1'][j]
    z = proj[..., :SSM_INNER]
    xbc = proj[..., SSM_INNER:SSM_INNER + SSM_CONV_CH]
    dt = proj[..., SSM_INNER + SSM_CONV_CH:]
    xbc, new_conv = causal_conv(xbc, conv_buf, prm['ssm_conv_w'][j], prm['ssm_conv_b'][j])
    xbc = jax.nn.silu(xbc).astype(F32)
    xs = xbc[..., :SSM_INNER].reshape(bsz, L, SSM_GROUPS, SSM_HPG, SSM_HEAD)
    Bm = xbc[..., SSM_INNER:SSM_INNER + SSM_GN].reshape(bsz, L, SSM_GROUPS, SSM_STATE)
    Cm = xbc[..., SSM_INNER + SSM_GN:].reshape(bsz, L, SSM_GROUPS, SSM_STATE)
    dt = jax.nn.softplus(dt.astype(F32) + prm['ssm_dt_bias'][j]).reshape(bsz, L, SSM_GROUPS, SSM_HPG)
    A = -jnp.exp(prm['ssm_a_log'][j].astype(F32)).reshape(SSM_GROUPS, SSM_HPG)
    chunk = SSM_CHUNK if L % SSM_CHUNK == 0 else L
    y, new_ssm = ssd(xs, dt, A, Bm, Cm, S0, chunk)
    y = y + prm['ssm_d'][j].reshape(SSM_GROUPS, SSM_HPG)[..., None] * xs
    y = y.reshape(bsz, L, SSM_INNER) * jax.nn.silu(z.astype(F32))
    yg = y.reshape(bsz, L, SSM_GROUPS, SSM_INNER // SSM_GROUPS)
    yg = yg * lax.rsqrt(jnp.mean(yg * yg, axis=-1, keepdims=True) + SSM_NORM_EPS)
    y = (yg.reshape(bsz, L, SSM_INNER) * prm['ssm_norm_g'][j]).astype(xn.dtype)
    return y @ prm['w_out1'][j], new_conv, new_ssm.astype(xn.dtype)


def swiglu(x, wg, wu, wd):
    return (jax.nn.silu(x @ wg) * (x @ wu)) @ wd


def moe_ffn(xn, router_w, router_b, wg, wu, wd):
    shape = xn.shape
    xf = xn.reshape(-1, shape[-1])
    T = xf.shape[0]
    logits = xf.astype(F32) @ router_w.astype(F32) + router_b.astype(F32)
    top_logit, top_idx = lax.top_k(logits, TOP_K)
    gates = jax.nn.softmax(top_logit, axis=-1).astype(xn.dtype)
    n_assign = T * TOP_K
    n_blocks = -(-n_assign // MOE_BLOCK) + N_EXPERTS
    flat_e = top_idx.reshape(-1)
    flat_tok = jnp.repeat(jnp.arange(T, dtype=jnp.int32), TOP_K)
    flat_g = gates.reshape(-1)
    order = jnp.argsort(flat_e)
    sorted_e = flat_e[order]
    counts = jnp.bincount(flat_e, length=N_EXPERTS)
    padded = (counts + MOE_BLOCK - 1) // MOE_BLOCK * MOE_BLOCK
    padded_end = jnp.cumsum(padded)
    padded_start = padded_end - padded
    start = jnp.cumsum(counts) - counts
    dest = padded_start[sorted_e] + jnp.arange(n_assign) - start[sorted_e]
    slot_tok = jnp.zeros((n_blocks * MOE_BLOCK,), jnp.int32).at[dest].set(flat_tok[order])
    slot_g = jnp.zeros((n_blocks * MOE_BLOCK,), xn.dtype).at[dest].set(flat_g[order])
    block_e = jnp.minimum(jnp.searchsorted(padded_end, jnp.arange(n_blocks) * MOE_BLOCK, side='right'), N_EXPERTS - 1)

    def expert_block(args):
        tok, g, e = args
        xb = xf[tok]
        h = jax.nn.silu(xb @ wg[e]) * (xb @ wu[e])
        return (h @ wd[e]) * g[:, None]

    yb = lax.map(expert_block, (slot_tok.reshape(n_blocks, MOE_BLOCK), slot_g.reshape(n_blocks, MOE_BLOCK), block_e))
    out = jnp.zeros_like(xf).at[slot_tok].add(yb.reshape(-1, shape[-1]))
    return out.reshape(shape)


def run_trunk(x, states, prm):
    lru_conv, lru_h, rwkv_shift, rwkv_wkv, ssm_conv, ssm_state = states
    even_out = ([], [], [], [])
    odd_out = ([], [])
    for layer in range(DEPTH):
        j = layer // 2
        xn = rmsnorm(x, prm['norm_mix'][layer])
        if layer % 2 == 0:
            h, c, hh, sh, wkv = mixer_ab(xn, lru_conv[j], lru_h[j], rwkv_shift[j], rwkv_wkv[j], prm, j)
            for lst, val in zip(even_out, (c, hh, sh, wkv)):
                lst.append(val)
            x = x + h
            x = x + swiglu(rmsnorm(x, prm['norm_ffn'][layer]), prm['ffn_wg'][j], prm['ffn_wu'][j], prm['ffn_wd'][j])
        else:
            h, c, s = mamba2_mixer(xn, ssm_conv[j], ssm_state[j], prm, j)
            odd_out[0].append(c)
            odd_out[1].append(s)
            x = x + h
            x = x + moe_ffn(rmsnorm(x, prm['norm_ffn'][layer]), prm['router_w'][j], prm['router_b'][j],
                            prm['moe_wg'][j], prm['moe_wu'][j], prm['moe_wd'][j])
    y = rmsnorm(x, prm['norm_final'])
    new_states = tuple(jnp.stack(lst) for lst in even_out + odd_out)
    return y, new_states


def setup_inputs(seed: int = 0) -> dict:
    key = jax.random.key(seed)
    ks = iter(jax.random.split(key, 64))

    def nrm(shape, scale):
        return jax.random.normal(next(ks), shape, F32) * scale

    def unif(shape, lo, hi):
        return jax.random.uniform(next(ks), shape, F32, lo, hi)

    D = D_MODEL
    NE = N_EVEN
    NO = N_ODD
    inp = {}
    inp['x_prompt'] = nrm((BATCH, SEQ, D), 1.0)
    inp['x_sample'] = nrm((DEC_BATCH, DEC_SEQ, D), 1.0)
    inp['state_lru_conv'] = nrm((NE, DEC_BATCH, CONV_W - 1, LRU_WIDTH), 1.0)
    inp['state_lru_h'] = nrm((NE, DEC_BATCH, LRU_WIDTH), 0.5)
    inp['state_rwkv_shift'] = nrm((NE, DEC_BATCH, SHIFT_COLS), 1.0)
    inp['state_rwkv_wkv'] = nrm((NE, DEC_BATCH, RWKV_HEADS, RWKV_HEAD, RWKV_HEAD), 0.3)
    inp['state_ssm_conv'] = nrm((NO, DEC_BATCH, CONV_W - 1, SSM_CONV_CH), 1.0)
    inp['state_ssm'] = nrm((NO, DEC_BATCH, SSM_HEADS, SSM_HEAD, SSM_STATE), 0.1)
    inp['norm_mix'] = 1.0 + nrm((DEPTH, D), 0.02)
    inp['norm_ffn'] = 1.0 + nrm((DEPTH, D), 0.02)
    inp['norm_final'] = 1.0 + nrm((D,), 0.02)
    inp['w_in0'] = nrm((NE, D, PROJ0_COLS), D ** -0.5)
    inp['lru_conv_w'] = nrm((NE, CONV_W, LRU_WIDTH), 0.5)
    inp['lru_conv_b'] = nrm((NE, LRU_WIDTH), 0.01)
    inp['lru_wa'] = nrm((NE, LRU_HEADS, LRU_BLOCK, LRU_BLOCK), LRU_BLOCK ** -0.5)
    inp['lru_ba'] = nrm((NE, LRU_HEADS, LRU_BLOCK), 0.01)
    inp['lru_wx'] = nrm((NE, LRU_HEADS, LRU_BLOCK, LRU_BLOCK), LRU_BLOCK ** -0.5)
    inp['lru_bx'] = nrm((NE, LRU_HEADS, LRU_BLOCK), 0.01)
    a_root = unif((NE, LRU_WIDTH), 0.9, 0.999) ** (1.0 / LRU_C)
    inp['lru_lambda'] = jnp.log(a_root) - jnp.log1p(-a_root)
    inp['rwkv_mu'] = unif((NE, SHIFT_COLS), 0.0, 1.0)
    ratio = jnp.arange(RWKV_WIDTH, dtype=F32) / (RWKV_WIDTH - 1)
    inp['rwkv_w0'] = -6.5 + 5.0 * ratio ** 0.85 + nrm((NE, RWKV_WIDTH), 0.1)
    inp['rwkv_w_decay_up'] = nrm((NE, DECAY_RANK, RWKV_WIDTH), 0.1 * DECAY_RANK ** -0.5)
    inp['rwkv_a0'] = nrm((NE, RWKV_WIDTH), 0.1)
    inp['rwkv_w_iclr_up'] = nrm((NE, ICLR_RANK, RWKV_WIDTH), 0.5 * ICLR_RANK ** -0.5)
    inp['rwkv_w_gate_up'] = nrm((NE, GATE_RANK, RWKV_WIDTH), GATE_RANK ** -0.5)
    inp['rwkv_k_k'] = 0.85 + nrm((NE, RWKV_WIDTH), 0.02)
    inp['rwkv_k_a'] = 1.0 + nrm((NE, RWKV_WIDTH), 0.02)
    inp['rwkv_r_k'] = nrm((NE, RWKV_HEADS, RWKV_HEAD), 0.1)
    inp['rwkv_ln_w'] = 1.0 + nrm((NE, RWKV_WIDTH), 0.02)
    inp['rwkv_ln_b'] = nrm((NE, RWKV_WIDTH), 0.01)
    inp['w_out0'] = nrm((NE, MIX0_WIDTH, D), MIX0_WIDTH ** -0.5)
    inp['ffn_wg'] = nrm((NE, D, D_FF), D ** -0.5)
    inp['ffn_wu'] = nrm((NE, D, D_FF), D ** -0.5)
    inp['ffn_wd'] = nrm((NE, D_FF, D), D_FF ** -0.5)
    inp['w_in1'] = nrm((NO, D, PROJ1_COLS), D ** -0.5)
    inp['ssm_conv_w'] = nrm((NO, CONV_W, SSM_CONV_CH), 0.5)
    inp['ssm_conv_b'] = nrm((NO, SSM_CONV_CH), 0.01)
    dt0 = jnp.exp(unif((NO, SSM_HEADS), math.log(1e-3), math.log(1e-1)))
    inp['ssm_dt_bias'] = dt0 + jnp.log(-jnp.expm1(-dt0))
    inp['ssm_a_log'] = jnp.log(unif((NO, SSM_HEADS), 1.0, 16.0))
    inp['ssm_d'] = 1.0 + nrm((NO, SSM_HEADS), 0.1)
    inp['ssm_norm_g'] = 1.0 + nrm((NO, SSM_INNER), 0.02)
    inp['w_out1'] = nrm((NO, SSM_INNER, D), SSM_INNER ** -0.5)
    inp['router_w'] = nrm((NO, D, N_EXPERTS), D ** -0.5)
    inp['router_b'] = nrm((NO, N_EXPERTS), 0.01)
    inp['moe_wg'] = nrm((NO, N_EXPERTS, D, D_FF_EXPERT), D ** -0.5)
    inp['moe_wu'] = nrm((NO, N_EXPERTS, D, D_FF_EXPERT), D ** -0.5)
    inp['moe_wd'] = nrm((NO, N_EXPERTS, D_FF_EXPERT, D), D_FF_EXPERT ** -0.5)
    return inp


def reference(x_prompt, x_sample, state_lru_conv, state_lru_h, state_rwkv_shift, state_rwkv_wkv, state_ssm_conv, state_ssm,
              norm_mix, norm_ffn, norm_final, w_in0, lru_conv_w, lru_conv_b, lru_wa, lru_ba, lru_wx, lru_bx, lru_lambda,
              rwkv_mu, rwkv_w0, rwkv_w_decay_up, rwkv_a0, rwkv_w_iclr_up, rwkv_w_gate_up, rwkv_k_k, rwkv_k_a, rwkv_r_k,
              rwkv_ln_w, rwkv_ln_b, w_out0, ffn_wg, ffn_wu, ffn_wd, w_in1, ssm_conv_w, ssm_conv_b, ssm_dt_bias, ssm_a_log,
              ssm_d, ssm_norm_g, w_out1, router_w, router_b, moe_wg, moe_wu, moe_wd):
    prm = dict(norm_mix=norm_mix, norm_ffn=norm_ffn, norm_final=norm_final, w_in0=w_in0, lru_conv_w=lru_conv_w,
               lru_conv_b=lru_conv_b, lru_wa=lru_wa, lru_ba=lru_ba, lru_wx=lru_wx, lru_bx=lru_bx, lru_lambda=lru_lambda,
               rwkv_mu=rwkv_mu, rwkv_w0=rwkv_w0, rwkv_w_decay_up=rwkv_w_decay_up, rwkv_a0=rwkv_a0,
               rwkv_w_iclr_up=rwkv_w_iclr_up, rwkv_w_gate_up=rwkv_w_gate_up, rwkv_k_k=rwkv_k_k, rwkv_k_a=rwkv_k_a,
               rwkv_r_k=rwkv_r_k, rwkv_ln_w=rwkv_ln_w, rwkv_ln_b=rwkv_ln_b, w_out0=w_out0, ffn_wg=ffn_wg, ffn_wu=ffn_wu,
               ffn_wd=ffn_wd, w_in1=w_in1, ssm_conv_w=ssm_conv_w, ssm_conv_b=ssm_conv_b, ssm_dt_bias=ssm_dt_bias,
               ssm_a_log=ssm_a_log, ssm_d=ssm_d, ssm_norm_g=ssm_norm_g, w_out1=w_out1, router_w=router_w,
               router_b=router_b, moe_wg=moe_wg, moe_wu=moe_wu, moe_wd=moe_wd)
    sample_states = (state_lru_conv, state_lru_h, state_rwkv_shift, state_rwkv_wkv, state_ssm_conv, state_ssm)
    nb = x_prompt.shape[0]
    prompt_states = tuple(jnp.zeros(s.shape[:1] + (nb,) + s.shape[2:], x_prompt.dtype) for s in sample_states)
    y_prompt, p_new = run_trunk(x_prompt, prompt_states, prm)
    y_sample, s_new = run_trunk(x_sample, sample_states, prm)
    p_lru_conv, p_lru_h, p_rwkv_shift, p_rwkv_wkv, p_ssm_conv, p_ssm = p_new
    s_lru_conv, s_lru_h, s_rwkv_shift, s_rwkv_wkv, s_ssm_conv, s_ssm = s_new
    return (y_prompt, y_sample, p_lru_conv, p_lru_h, p_rwkv_shift, p_rwkv_wkv, p_ssm_conv, p_ssm,
            s_lru_conv, s_lru_h, s_rwkv_shift, s_rwkv_wkv, s_ssm_conv, s_ssm)
```

```python
import functools

import jax
import jax.numpy as jnp
from jax import lax
from jax.experimental import pallas as pl
from jax.experimental.pallas import tpu as pltpu

F32 = jnp.float32
BF16 = jnp.bfloat16
HIGHEST = lax.Precision.HIGHEST

NORM_EPS = 1e-6
CONV_W = 4
LRU_HEADS = 8
LRU_C = 8.0
RWKV_HEAD = 64
DECAY_RANK = 64
ICLR_RANK = 64
GATE_RANK = 128
RWKV_GN_EPS = 64e-5
SSM_HEAD = 64
SSM_GROUPS = 8
SSM_STATE = 128
SSM_CHUNK = 128
SSM_NORM_EPS = 1e-5
N_EXPERTS = 8

V7X_VMEM_BYTES = 64 * 1024 * 1024
VMEM_LIMIT = V7X_VMEM_BYTES - 8 * 1024 * 1024
SUBLANES = 8
LANES = 128

RWKV_CHUNK = 64
LRU_CHUNK = 256
MOE_TILE = 512
MOE_FF_TILE = 512
FFN_FF_TILE = 512
ROUTER_TILE = 384
GATHER_TILE = 384


def _cparams(n_axes):
    return pltpu.CompilerParams(dimension_semantics=("arbitrary",) * n_axes,
                                vmem_limit_bytes=VMEM_LIMIT)


def _row_tile(n_rows, cap):
    best = None
    for t in range(SUBLANES, min(cap, n_rows) + 1, SUBLANES):
        if n_rows % t == 0:
            best = t
    assert best is not None, (n_rows, cap)
    return best


def _dot(a, b, precision=None):
    return jnp.dot(a, b, preferred_element_type=F32, precision=precision)


def _dot_nt(a, b, precision=None):
    return lax.dot_general(a, b, (((1,), (1,)), ((), ())), preferred_element_type=F32,
                           precision=precision)


def _dot_tn(a, b, precision=None):
    return lax.dot_general(a, b, (((0,), (0,)), ((), ())), preferred_element_type=F32,
                           precision=precision)


def _softplus(x):
    return jnp.maximum(x, 0.0) + jnp.log1p(jnp.exp(-jnp.abs(x)))


def _silu(x):
    return x * jax.nn.sigmoid(x)


def _gelu_tanh(x):
    return 0.5 * x * (1.0 + jnp.tanh(0.7978845608028654 * (x + 0.044715 * (x * x * x))))


def _rms(x, g, eps):
    return x * lax.rsqrt(jnp.mean(x * x, axis=-1, keepdims=True) + eps) * g


def _norm_mm_kernel(x_ref, g_ref, w_ref, o_ref, xn_sc):
    @pl.when(pl.program_id(1) == 0)
    def _():
        xn_sc[...] = _rms(x_ref[...], g_ref[...], NORM_EPS).astype(BF16)

    o_ref[...] = _dot(xn_sc[...], w_ref[...].astype(BF16))


def norm_matmul(x, g, w, col0_blk, n_blk, tn, tm, name):
    t_rows, d = x.shape
    return pl.pallas_call(
        _norm_mm_kernel,
        grid=(t_rows // tm, n_blk),
        in_specs=[pl.BlockSpec((tm, d), lambda i, j: (i, 0)),
                  pl.BlockSpec((1, d), lambda i, j: (0, 0)),
                  pl.BlockSpec((d, tn), lambda i, j: (0, j + col0_blk))],
        out_specs=pl.BlockSpec((tm, tn), lambda i, j: (i, j)),
        out_shape=jax.ShapeDtypeStruct((t_rows, n_blk * tn), F32),
        scratch_shapes=[pltpu.VMEM((tm, d), BF16)],
        compiler_params=_cparams(2),
        name=name,
    )(x, g.reshape(1, d), w)


def _mm_res_kernel(*refs, n_in):
    x_refs, w_refs = refs[:n_in], refs[n_in:2 * n_in]
    res_ref, o_ref = refs[2 * n_in], refs[2 * n_in + 1]
    acc = res_ref[...]
    for x_ref, w_ref in zip(x_refs, w_refs):
        acc = acc + _dot(x_ref[...].astype(BF16), w_ref[...].astype(BF16))
    o_ref[...] = acc


def matmul_residual(xs, w, res, tm, tn, name):
    n_in = len(xs)
    t_rows, kp = xs[0].shape
    n_cols = w.shape[1]
    in_specs = [pl.BlockSpec((tm, kp), lambda i, j: (i, 0)) for _ in xs]
    in_specs += [pl.BlockSpec((kp, tn), functools.partial(lambda i, j, p: (p, j), p=p))
                 for p in range(n_in)]
    in_specs += [pl.BlockSpec((tm, tn), lambda i, j: (i, j))]
    return pl.pallas_call(
        functools.partial(_mm_res_kernel, n_in=n_in),
        grid=(t_rows // tm, n_cols // tn),
        in_specs=in_specs,
        out_specs=pl.BlockSpec((tm, tn), lambda i, j: (i, j)),
        out_shape=jax.ShapeDtypeStruct((t_rows, n_cols), F32),
        compiler_params=_cparams(2),
        name=name,
    )(*xs, *([w] * n_in), res)


def _ffn_kernel(x_ref, g_ref, wg_ref, wu_ref, wd_ref, o_ref, xn_sc, acc_sc):
    j = pl.program_id(1)

    @pl.when(j == 0)
    def _():
        xn_sc[...] = _rms(x_ref[...], g_ref[...], NORM_EPS).astype(BF16)
        acc_sc[...] = jnp.zeros_like(acc_sc)

    xn = xn_sc[...]
    hg = _dot(xn, wg_ref[...].astype(BF16))
    hu = _dot(xn, wu_ref[...].astype(BF16))
    h = (_silu(hg) * hu).astype(BF16)
    acc_sc[...] += _dot(h, wd_ref[...].astype(BF16))

    @pl.when(j == pl.num_programs(1) - 1)
    def _():
        o_ref[...] = x_ref[...] + acc_sc[...]


def ffn_residual(x, g, wg, wu, wd, tm, tf):
    t_rows, d = x.shape
    d_ff = wg.shape[1]
    return pl.pallas_call(
        _ffn_kernel,
        grid=(t_rows // tm, d_ff // tf),
        in_specs=[pl.BlockSpec((tm, d), lambda i, j: (i, 0)),
                  pl.BlockSpec((1, d), lambda i, j: (0, 0)),
                  pl.BlockSpec((d, tf), lambda i, j: (0, j)),
                  pl.BlockSpec((d, tf), lambda i, j: (0, j)),
                  pl.BlockSpec((tf, d), lambda i, j: (j, 0))],
        out_specs=pl.BlockSpec((tm, d), lambda i, j: (i, 0)),
        out_shape=jax.ShapeDtypeStruct((t_rows, d), F32),
        scratch_shapes=[pltpu.VMEM((tm, d), BF16), pltpu.VMEM((tm, d), F32)],
        compiler_params=_cparams(2),
        name="ffn_swiglu",
    )(x, g.reshape(1, d), wg, wu, wd)


def _conv4(u, u1, u2, u3, cw, cb):
    return cb + cw[3:4] * u + cw[2:3] * u1 + cw[1:2] * u2 + cw[0:1] * u3


def _lru_gates(xc, wa, ba, wx, bx, lam):
    xb = xc.astype(BF16)
    r = jax.nn.sigmoid(_dot(xb, wa) + ba)
    i = jax.nn.sigmoid(_dot(xb, wx) + bx)
    log_a = -LRU_C * r * _softplus(-lam)
    a = jnp.exp(log_a)
    u = jnp.sqrt(1.0 - jnp.exp(2.0 * log_a)) * (i * xc)
    return a, u


def _lru_prompt_kernel(x_ref, g_ref, conv0_ref, h0_ref, cw_ref, cb_ref, wa_ref, ba_ref, wx_ref,
                       bx_ref, lam_ref, o_ref, nconv_ref, nh_ref, ext_sc, h_sc, *, lc):
    width = x_ref.shape[1]

    @pl.when(pl.program_id(1) == 0)
    def _():
        ext_sc[0:SUBLANES, :] = jnp.zeros((SUBLANES, width), F32)
        ext_sc[SUBLANES - 3:SUBLANES, :] = conv0_ref[0]
        h_sc[...] = h0_ref[0]

    u = x_ref[...]
    ext_sc[SUBLANES:SUBLANES + lc, :] = u
    xc = _conv4(u, ext_sc[SUBLANES - 1:SUBLANES - 1 + lc, :], ext_sc[SUBLANES - 2:SUBLANES - 2 + lc, :],
                ext_sc[SUBLANES - 3:SUBLANES - 3 + lc, :], cw_ref[...], cb_ref[...])
    tail = ext_sc[lc + SUBLANES - 3:lc + SUBLANES, :]
    ext_sc[SUBLANES - 3:SUBLANES, :] = tail
    nconv_ref[0] = tail

    a, h = _lru_gates(xc, wa_ref[...], ba_ref[...], wx_ref[...], bx_ref[...], lam_ref[...])
    row = lax.broadcasted_iota(jnp.int32, (lc, width), 0)
    s = 1
    while s < lc:
        keep = row >= s
        a_sh = jnp.where(keep, pltpu.roll(a, s, 0), 1.0)
        h_sh = jnp.where(keep, pltpu.roll(h, s, 0), 0.0)
        h = a * h_sh + h
        a = a * a_sh
        s *= 2
    hs = h + a * h_sc[...]
    h_last = hs[lc - 1:lc, :]
    h_sc[...] = h_last
    nh_ref[0] = h_last
    o_ref[...] = hs * _gelu_tanh(g_ref[...])


def lru_prompt(proj_lru, n_batch, seq, conv0, h0, lp):
    width = conv0.shape[-1]
    lc = min(LRU_CHUNK, seq)
    n_chunks = seq // lc
    full = lambda shape: pl.BlockSpec(shape, lambda b, c: (0,) * len(shape))
    return pl.pallas_call(
        functools.partial(_lru_prompt_kernel, lc=lc),
        grid=(n_batch, n_chunks),
        in_specs=[pl.BlockSpec((lc, width), lambda b, c: (b * n_chunks + c, 0)),
                  pl.BlockSpec((lc, width), lambda b, c: (b * n_chunks + c, 1)),
                  pl.BlockSpec((1, 3, width), lambda b, c: (b, 0, 0)),
                  pl.BlockSpec((1, 1, width), lambda b, c: (b, 0, 0)),
                  full((CONV_W, width)), full((1, width)), full((width, width)), full((1, width)),
                  full((width, width)), full((1, width)), full((1, width))],
        out_specs=[pl.BlockSpec((lc, width), lambda b, c: (b * n_chunks + c, 0)),
                   pl.BlockSpec((1, 3, width), lambda b, c: (b, 0, 0)),
                   pl.BlockSpec((1, 1, width), lambda b, c: (b, 0, 0))],
        out_shape=[jax.ShapeDtypeStruct((n_batch * seq, width), F32),
                   jax.ShapeDtypeStruct((n_batch, 3, width), F32),
                   jax.ShapeDtypeStruct((n_batch, 1, width), F32)],
        scratch_shapes=[pltpu.VMEM((lc + SUBLANES, width), F32), pltpu.VMEM((1, width), F32)],
        compiler_params=_cparams(2),
        name="lru_prompt",
    )(proj_lru, proj_lru, conv0, h0.reshape(n_batch, 1, width), lp["cw"], lp["cb"], lp["wa"], lp["ba"],
      lp["wx"], lp["bx"], lp["lam"])


def _lru_step_kernel(x_ref, g_ref, buf_ref, h0_ref, cw_ref, cb_ref, wa_ref, ba_ref, wx_ref, bx_ref,
                     lam_ref, o_ref, nbuf_ref, nh_ref):
    width = x_ref.shape[1]
    u = x_ref[...]
    b0, b1, b2 = (buf_ref[:, k * width:(k + 1) * width] for k in range(3))
    xc = _conv4(u, b2, b1, b0, cw_ref[...], cb_ref[...])
    a, uu = _lru_gates(xc, wa_ref[...], ba_ref[...], wx_ref[...], bx_ref[...], lam_ref[...])
    h = a * h0_ref[...] + uu
    nh_ref[...] = h
    o_ref[...] = h * _gelu_tanh(g_ref[...])
    nbuf_ref[:, 0:width] = b1
    nbuf_ref[:, width:2 * width] = b2
    nbuf_ref[:, 2 * width:3 * width] = u


def lru_step(x_lru, g_lru, conv_buf, h0, lp):
    n, width = x_lru.shape
    return pl.pallas_call(
        _lru_step_kernel,
        out_shape=[jax.ShapeDtypeStruct((n, width), F32),
                   jax.ShapeDtypeStruct((n, 3 * width), F32),
                   jax.ShapeDtypeStruct((n, width), F32)],
        compiler_params=pltpu.CompilerParams(vmem_limit_bytes=VMEM_LIMIT),
        name="lru_step",
    )(x_lru, g_lru, conv_buf.reshape(n, 3 * width), h0, lp["cw"], lp["cb"], lp["wa"], lp["ba"],
      lp["wx"], lp["bx"], lp["lam"])


def _rwkv_rows(mixed, rp):
    w = rp["w0"].shape[1]
    r, k, v = mixed[:, 0:w], mixed[:, w:2 * w], mixed[:, 2 * w:3 * w]
    o = 3 * w
    wd = mixed[:, o:o + DECAY_RANK]
    ad = mixed[:, o + DECAY_RANK:o + DECAY_RANK + ICLR_RANK]
    gd = mixed[:, o + DECAY_RANK + ICLR_RANK:o + DECAY_RANK + ICLR_RANK + GATE_RANK]
    dec_in = rp["w0"] + _dot(jnp.tanh(wd).astype(BF16), rp["wdec"].astype(BF16))
    w_log = -_softplus(-dec_in) - 0.5
    lw = -jnp.exp(w_log)
    iclr = jax.nn.sigmoid(rp["a0"] + _dot(ad.astype(BF16), rp["wiclr"].astype(BF16)))
    gate = _dot(jax.nn.sigmoid(gd).astype(BF16), rp["wgate"].astype(BF16))
    kk = k * rp["kk"]
    ss = _dot(kk * kk, rp["hsum"], HIGHEST)
    kkn = kk / jnp.maximum(jnp.sqrt(ss), 1e-12)
    k2 = k * (1.0 + (iclr - 1.0) * rp["ka"])
    return r, k2, v, lw, -kkn, kkn * iclr, gate


def _rwkv_post(o, r, k2, v, gate, rp):
    inv = 1.0 / RWKV_HEAD
    mu = _dot(o, rp["hsum"], HIGHEST) * inv
    d = o - mu
    var = _dot(d * d, rp["hsum"], HIGHEST) * inv
    on = d * lax.rsqrt(var + RWKV_GN_EPS) * rp["lnw"] + rp["lnb"]
    bonus = _dot(r * k2 * rp["rk"], rp["hsum"], HIGHEST) * v
    return (on + bonus) * gate


_RWKV_PARAM_NAMES = ("mu", "w0", "wdec", "a0", "wiclr", "wgate", "kk", "ka", "rk", "lnw", "lnb", "hsum")


def _rwkv_prompt_kernel(p_ref, shift0_ref, s0_ref, *rest, chunk):
    n_prm = len(_RWKV_PARAM_NAMES)
    rp = {name: ref[...] for name, ref in zip(_RWKV_PARAM_NAMES, rest[:n_prm])}
    o_ref, nshift_ref, nwkv_ref, prev_sc, s_sc, o_sc = rest[n_prm:]
    n_heads = s_sc.shape[0]
    hd = RWKV_HEAD

    @pl.when(pl.program_id(1) == 0)
    def _():
        prev_sc[...] = shift0_ref[0]
        s_sc[...] = s0_ref[0]

    p = p_ref[...]
    row = lax.broadcasted_iota(jnp.int32, p.shape, 0)
    p_prev = jnp.where(row >= 1, pltpu.roll(p, 1, 0), prev_sc[...])
    last = p[chunk - 1:chunk, :]
    prev_sc[...] = last
    nshift_ref[0] = last
    mixed = p + (p_prev - p) * rp["mu"]
    r, k2, v, lw, a, b, gate = _rwkv_rows(mixed, rp)

    ti = lax.broadcasted_iota(jnp.int32, (chunk, chunk), 0)
    si = lax.broadcasted_iota(jnp.int32, (chunk, chunk), 1)
    strict, incl = ti > si, ti >= si
    cs = _dot(incl.astype(F32), lw, HIGHEST)
    g_in, g_ex, g_inv = jnp.exp(cs), jnp.exp(cs - lw), jnp.exp(-cs)
    at, bt, kt, rt = a * g_ex, b * g_inv, k2 * g_inv, r * g_in
    g_end = g_in[chunk - 1:chunk, :]
    bc, kc = bt * g_end, kt * g_end

    for h in range(n_heads):
        sl = slice(h * hd, (h + 1) * hd)
        s0 = s_sc[h]
        ar = jnp.concatenate([at[:, sl], rt[:, sl]], axis=0)
        bk = jnp.concatenate([bt[:, sl], kt[:, sl]], axis=0)
        m = _dot_nt(ar, bk, HIGHEST)
        lab = jnp.where(strict, m[:chunk, :chunk], 0.0)
        lak = jnp.where(strict, m[:chunk, chunk:], 0.0)
        lrb = jnp.where(incl, m[chunk:, :chunk], 0.0)
        lrk = jnp.where(incl, m[chunk:, chunk:], 0.0)
        ars = _dot_nt(ar, s0, HIGHEST)
        vh = v[:, sl]
        y = ars[:chunk] + _dot(lak, vh, HIGHEST)
        pw = lab
        n_fac = max(1, (chunk - 1).bit_length())
        for f in range(n_fac):
            y = y + _dot(pw, y, HIGHEST)
            if f + 1 < n_fac:
                pw = _dot(pw, pw, HIGHEST)
        o_sc[:, sl] = ars[chunk:] + _dot(lrb, y, HIGHEST) + _dot(lrk, vh, HIGHEST)
        s_sc[h] = (s0 * g_end[:, sl] + _dot_tn(y, bc[:, sl], HIGHEST)
                   + _dot_tn(vh, kc[:, sl], HIGHEST))

    nwkv_ref[0] = s_sc[...]
    o_ref[...] = _rwkv_post(o_sc[...], r, k2, v, gate, rp)


def _rwkv_param_list(rp):
    return [rp[name] for name in _RWKV_PARAM_NAMES]


def rwkv_prompt(p_rwkv, n_batch, seq, shift0, wkv0, rp):
    cols = shift0.shape[-1]
    n_heads, hd = wkv0.shape[1], wkv0.shape[2]
    width = n_heads * hd
    chunk = min(RWKV_CHUNK, seq)
    n_chunks = seq // chunk
    prm = _rwkv_param_list(rp)
    prm_specs = [pl.BlockSpec(x.shape, lambda b, c: (0, 0)) for x in prm]
    return pl.pallas_call(
        functools.partial(_rwkv_prompt_kernel, chunk=chunk),
        grid=(n_batch, n_chunks),
        in_specs=[pl.BlockSpec((chunk, cols), lambda b, c: (b * n_chunks + c, 0)),
                  pl.BlockSpec((1, 1, cols), lambda b, c: (b, 0, 0)),
                  pl.BlockSpec((1, n_heads, hd, hd), lambda b, c: (b, 0, 0, 0))] + prm_specs,
        out_specs=[pl.BlockSpec((chunk, width), lambda b, c: (b * n_chunks + c, 0)),
                   pl.BlockSpec((1, 1, cols), lambda b, c: (b, 0, 0)),
                   pl.BlockSpec((1, n_heads, hd, hd), lambda b, c: (b, 0, 0, 0))],
        out_shape=[jax.ShapeDtypeStruct((n_batch * seq, width), F32),
                   jax.ShapeDtypeStruct((n_batch, 1, cols), F32),
                   jax.ShapeDtypeStruct((n_batch, n_heads, hd, hd), F32)],
        scratch_shapes=[pltpu.VMEM((1, cols), F32), pltpu.VMEM((n_heads, hd, hd), F32),
                        pltpu.VMEM((chunk, width), F32)],
        compiler_params=_cparams(2),
        name="rwkv_prompt",
    )(p_rwkv, shift0.reshape(n_batch, 1, cols), wkv0, *prm)


def _rwkv_step_pre_kernel(p_ref, prev_ref, *rest):
    n_prm = len(_RWKV_PARAM_NAMES)
    rp = {name: ref[...] for name, ref in zip(_RWKV_PARAM_NAMES, rest[:n_prm])}
    r_ref, k_ref, v_ref, w_ref, a_ref, b_ref, gate_ref = rest[n_prm:]
    p = p_ref[...]
    mixed = p + (prev_ref[...] - p) * rp["mu"]
    r, k2, v, lw, a, b, gate = _rwkv_rows(mixed, rp)
    r_ref[...] = r
    k_ref[...] = k2
    v_ref[...] = v
    w_ref[...] = jnp.exp(lw)
    a_ref[...] = a
    b_ref[...] = b
    gate_ref[...] = gate


def _rwkv_step_kernel(s_ref, w_ref, a_ref, b_ref, k_ref, r_ref, vt_ref, ns_ref, ot_ref):
    bt, n_heads = s_ref.shape[0], s_ref.shape[1]
    lane = lax.broadcasted_iota(jnp.int32, ot_ref.shape[1:], 1)
    for i in range(bt):
        vt = vt_ref[i]
        ot = jnp.zeros(ot_ref.shape[1:], F32)
        for h in range(n_heads):
            s = s_ref[i, h]
            sa = jnp.sum(s * a_ref[i, h], axis=-1, keepdims=True)
            s_new = s * w_ref[i, h] + sa * b_ref[i, h] + vt[:, h:h + 1] * k_ref[i, h]
            ns_ref[i, h] = s_new
            o = jnp.sum(s_new * r_ref[i, h], axis=-1, keepdims=True)
            ot = jnp.where(lane == h, o, ot)
        ot_ref[i] = ot


def _rwkv_step_post_kernel(o_ref, r_ref, k_ref, v_ref, gate_ref, *rest):
    n_prm = len(_RWKV_PARAM_NAMES)
    rp = {name: ref[...] for name, ref in zip(_RWKV_PARAM_NAMES, rest[:n_prm])}
    out_ref = rest[n_prm]
    out_ref[...] = _rwkv_post(o_ref[...], r_ref[...], k_ref[...], v_ref[...], gate_ref[...], rp)


def rwkv_step(p_rwkv, shift_prev, wkv0, rp):
    n = p_rwkv.shape[0]
    n_heads, hd = wkv0.shape[1], wkv0.shape[2]
    width = n_heads * hd
    prm = _rwkv_param_list(rp)
    cp = pltpu.CompilerParams(vmem_limit_bytes=VMEM_LIMIT)
    row = jax.ShapeDtypeStruct((n, width), F32)
    r, k2, v, w, a, b, gate = pl.pallas_call(
        _rwkv_step_pre_kernel, out_shape=[row] * 7, compiler_params=cp, name="rwkv_step_pre",
    )(p_rwkv, shift_prev, *prm)
    hrow = lambda z: z.reshape(n, n_heads, 1, hd)
    vt = jnp.transpose(v.reshape(n, n_heads, hd), (0, 2, 1))
    bt = SUBLANES
    vec_spec = pl.BlockSpec((bt, n_heads, 1, hd), lambda i: (i, 0, 0, 0))
    st_spec = pl.BlockSpec((bt, n_heads, hd, hd), lambda i: (i, 0, 0, 0))
    t_spec = pl.BlockSpec((bt, hd, n_heads), lambda i: (i, 0, 0))
    new_wkv, ot = pl.pallas_call(
        _rwkv_step_kernel,
        grid=(n // bt,),
        in_specs=[st_spec] + [vec_spec] * 5 + [t_spec],
        out_specs=[st_spec, t_spec],
        out_shape=[jax.ShapeDtypeStruct(wkv0.shape, F32), jax.ShapeDtypeStruct((n, hd, n_heads), F32)],
        compiler_params=_cparams(1),
        name="rwkv_step",
    )(wkv0, hrow(w), hrow(a), hrow(b), hrow(k2), hrow(r), vt)
    o = jnp.transpose(ot, (0, 2, 1)).reshape(n, width)
    out_b = pl.pallas_call(
        _rwkv_step_post_kernel, out_shape=row, compiler_params=cp, name="rwkv_step_post",
    )(o, r, k2, v, gate, *prm)
    return out_b, new_wkv


def _mamba_post(y, xs, z, dexp, ng):
    y = (y + dexp * xs) * _silu(z)
    gw = y.shape[1] // SSM_GROUPS
    parts = []
    for g in range(SSM_GROUPS):
        yg = y[:, g * gw:(g + 1) * gw]
        parts.append(yg * lax.rsqrt(jnp.mean(yg * yg, axis=-1, keepdims=True) + SSM_NORM_EPS))
    return jnp.concatenate(parts, axis=1) * ng


def _ssd_prompt_kernel(z_ref, xlo_ref, xhi_ref, dt_ref, conv0_ref, s0_ref, cw_ref, cb_ref, dtb_ref,
                       alog_ref, dexp_ref, ng_ref, y_ref, nconv_ref, nssm_ref, ext_sc, s_sc, y_sc, *, q):
    inner = xlo_ref.shape[1]
    n_heads = s_sc.shape[0]
    hpg = n_heads // SSM_GROUPS

    @pl.when(pl.program_id(1) == 0)
    def _():
        ext_sc[0:SUBLANES, :] = jnp.zeros((SUBLANES, ext_sc.shape[1]), F32)
        ext_sc[SUBLANES - 3:SUBLANES, :] = conv0_ref[0]
        s_sc[...] = s0_ref[0]

    ext_sc[SUBLANES:SUBLANES + q, 0:inner] = xlo_ref[...]
    ext_sc[SUBLANES:SUBLANES + q, inner:] = xhi_ref[...]
    xbc = _silu(_conv4(ext_sc[SUBLANES:SUBLANES + q, :], ext_sc[SUBLANES - 1:SUBLANES - 1 + q, :],
                       ext_sc[SUBLANES - 2:SUBLANES - 2 + q, :], ext_sc[SUBLANES - 3:SUBLANES - 3 + q, :],
                       cw_ref[...], cb_ref[...]))
    tail = ext_sc[q + SUBLANES - 3:q + SUBLANES, :]
    ext_sc[SUBLANES - 3:SUBLANES, :] = tail
    nconv_ref[0] = tail

    xs = xbc[:, 0:inner]
    gn = SSM_GROUPS * SSM_STATE
    bm = xbc[:, inner:inner + gn].astype(BF16)
    cm = xbc[:, inner + gn:].astype(BF16)
    dt = _softplus(dt_ref[...] + dtb_ref[...])
    dta = dt * (-jnp.exp(alog_ref[...]))
    ti = lax.broadcasted_iota(jnp.int32, (q, q), 0)
    si = lax.broadcasted_iota(jnp.int32, (q, q), 1)
    causal = ti >= si
    da = _dot(causal.astype(F32), dta, HIGHEST)
    da_t = _dot_tn(dta, (ti <= si).astype(F32), HIGHEST)
    da_end = da[q - 1:q, :]
    to_end = jnp.exp(da_end - da)
    from_start = jnp.exp(da)
    end_decay = jnp.exp(da_end)

    for g in range(SSM_GROUPS):
        bg = bm[:, g * SSM_STATE:(g + 1) * SSM_STATE]
        cg = cm[:, g * SSM_STATE:(g + 1) * SSM_STATE]
        scores = _dot_nt(cg, bg)
        for hh in range(hpg):
            h = g * hpg + hh
            hs = slice(h * SSM_HEAD, (h + 1) * SSM_HEAD)
            lmat = jnp.exp(jnp.where(causal, da[:, h:h + 1] - da_t[h:h + 1, :], -jnp.inf))
            xh = xs[:, hs]
            xdt = xh * dt[:, h:h + 1]
            s_h = s_sc[h]
            y_diag = _dot((scores * lmat).astype(BF16), xdt.astype(BF16))
            y_off = _dot_nt(cg, s_h.astype(BF16)) * from_start[:, h:h + 1]
            y_sc[:, hs] = y_diag + y_off
            s_sc[h] = (s_h * end_decay[:, h:h + 1]
                       + _dot_tn((xdt * to_end[:, h:h + 1]).astype(BF16), bg))

    nssm_ref[0] = s_sc[...]
    y_ref[...] = _mamba_post(y_sc[...], xs, z_ref[...], dexp_ref[...], ng_ref[...])


def ssd_prompt(proj1, dt_raw, n_batch, seq, conv0, ssm0, mp):
    n_heads, hd, n_state = ssm0.shape[1:]
    inner = n_heads * hd
    conv_ch = conv0.shape[-1]
    q = SSM_CHUNK if seq % SSM_CHUNK == 0 else seq
    n_chunks = seq // q
    full = lambda shape: pl.BlockSpec(shape, lambda b, c: (0,) * len(shape))
    return pl.pallas_call(
        functools.partial(_ssd_prompt_kernel, q=q),
        grid=(n_batch, n_chunks),
        in_specs=[pl.BlockSpec((q, inner), lambda b, c: (b * n_chunks + c, 0)),
                  pl.BlockSpec((q, inner), lambda b, c: (b * n_chunks + c, 1)),
                  pl.BlockSpec((q, inner), lambda b, c: (b * n_chunks + c, 2)),
                  pl.BlockSpec((q, n_heads), lambda b, c: (b * n_chunks + c, 0)),
                  pl.BlockSpec((1, 3, conv_ch), lambda b, c: (b, 0, 0)),
                  pl.BlockSpec((1, n_heads, hd, n_state), lambda b, c: (b, 0, 0, 0)),
                  full((CONV_W, conv_ch)), full((1, conv_ch)), full((1, n_heads)), full((1, n_heads)),
                  full((1, inner)), full((1, inner))],
        out_specs=[pl.BlockSpec((q, inner), lambda b, c: (b * n_chunks + c, 0)),
                   pl.BlockSpec((1, 3, conv_ch), lambda b, c: (b, 0, 0)),
                   pl.BlockSpec((1, n_heads, hd, n_state), lambda b, c: (b, 0, 0, 0))],
        out_shape=[jax.ShapeDtypeStruct((n_batch * seq, inner), F32),
                   jax.ShapeDtypeStruct((n_batch, 3, conv_ch), F32),
                   jax.ShapeDtypeStruct(ssm0.shape, F32)],
        scratch_shapes=[pltpu.VMEM((q + SUBLANES, conv_ch), F32), pltpu.VMEM((n_heads, hd, n_state), F32),
                        pltpu.VMEM((q, inner), F32)],
        compiler_params=_cparams(2),
        name="ssd_prompt",
    )(proj1, proj1, proj1, dt_raw, conv0, ssm0, mp["cw"], mp["cb"], mp["dtb"], mp["alog"], mp["dexp"],
      mp["ng"])


def _ssd_step_pre_kernel(x_ref, dt_ref, buf_ref, cw_ref, cb_ref, dtb_ref, alog_ref,
                         xs_ref, b_ref, c_ref, dt_out_ref, dec_ref, nbuf_ref):
    ch = x_ref.shape[1]
    inner = xs_ref.shape[1]
    gn = b_ref.shape[1]
    u = x_ref[...]
    b0, b1, b2 = (buf_ref[:, k * ch:(k + 1) * ch] for k in range(3))
    xbc = _silu(_conv4(u, b2, b1, b0, cw_ref[...], cb_ref[...]))
    xs_ref[...] = xbc[:, 0:inner]
    b_ref[...] = xbc[:, inner:inner + gn]
    c_ref[...] = xbc[:, inner + gn:]
    dt = _softplus(dt_ref[...] + dtb_ref[...])
    dt_out_ref[...] = dt
    dec_ref[...] = jnp.exp(dt * (-jnp.exp(alog_ref[...])))
    nbuf_ref[:, 0:ch] = b1
    nbuf_ref[:, ch:2 * ch] = b2
    nbuf_ref[:, 2 * ch:3 * ch] = u


def _ssd_step_kernel(s_ref, xt_ref, b_ref, c_ref, dt_ref, dec_ref, ns_ref, yt_ref):
    bt, n_heads = s_ref.shape[0], s_ref.shape[1]
    hpg = n_heads // SSM_GROUPS
    lane = lax.broadcasted_iota(jnp.int32, yt_ref.shape[1:], 1)
    for i in range(bt):
        xt = xt_ref[i]
        dt = dt_ref[i]
        dec = dec_ref[i]
        yt = jnp.zeros(yt_ref.shape[1:], F32)
        for h in range(n_heads):
            g = h // hpg
            xdt = xt[:, h:h + 1] * dt[:, h:h + 1]
            s_new = s_ref[i, h] * dec[:, h:h + 1] + xdt * b_ref[i, g]
            ns_ref[i, h] = s_new
            y = jnp.sum(s_new * c_ref[i, g], axis=-1, keepdims=True)
            yt = jnp.where(lane == h, y, yt)
        yt_ref[i] = yt


def _ssd_step_post_kernel(y_ref, xs_ref, z_ref, dexp_ref, ng_ref, o_ref):
    o_ref[...] = _mamba_post(y_ref[...], xs_ref[...], z_ref[...], dexp_ref[...], ng_ref[...])


def ssd_step(z, xbc_raw, dt_raw, conv_buf, ssm0, mp):
    n, conv_ch = xbc_raw.shape
    n_heads, hd, n_state = ssm0.shape[1:]
    inner = n_heads * hd
    gn = SSM_GROUPS * n_state
    cp = pltpu.CompilerParams(vmem_limit_bytes=VMEM_LIMIT)
    xs, bm, cm, dt, dec, nbuf = pl.pallas_call(
        _ssd_step_pre_kernel,
        out_shape=[jax.ShapeDtypeStruct((n, inner), F32), jax.ShapeDtypeStruct((n, gn), F32),
                   jax.ShapeDtypeStruct((n, gn), F32), jax.ShapeDtypeStruct((n, n_heads), F32),
                   jax.ShapeDtypeStruct((n, n_heads), F32), jax.ShapeDtypeStruct((n, 3 * conv_ch), F32)],
        compiler_params=cp, name="ssd_step_pre",
    )(xbc_raw, dt_raw, conv_buf.reshape(n, 3 * conv_ch), mp["cw"], mp["cb"], mp["dtb"], mp["alog"])
    xt = jnp.transpose(xs.reshape(n, n_heads, hd), (0, 2, 1))
    bt = 4
    st_spec = pl.BlockSpec((bt, n_heads, hd, n_state), lambda i: (i, 0, 0, 0))
    t_spec = pl.BlockSpec((bt, hd, n_heads), lambda i: (i, 0, 0))
    g_spec = pl.BlockSpec((bt, SSM_GROUPS, 1, n_state), lambda i: (i, 0, 0, 0))
    h_spec = pl.BlockSpec((bt, 1, n_heads), lambda i: (i, 0, 0))
    new_ssm, yt = pl.pallas_call(
        _ssd_step_kernel,
        grid=(n // bt,),
        in_specs=[st_spec, t_spec, g_spec, g_spec, h_spec, h_spec],
        out_specs=[st_spec, t_spec],
        out_shape=[jax.ShapeDtypeStruct(ssm0.shape, F32), jax.ShapeDtypeStruct((n, hd, n_heads), F32)],
        compiler_params=_cparams(1),
        name="ssd_step",
    )(ssm0, xt, bm.reshape(n, SSM_GROUPS, 1, n_state), cm.reshape(n, SSM_GROUPS, 1, n_state),
      dt.reshape(n, 1, n_heads), dec.reshape(n, 1, n_heads))
    y = jnp.transpose(yt, (0, 2, 1)).reshape(n, inner)
    y = pl.pallas_call(
        _ssd_step_post_kernel, out_shape=jax.ShapeDtypeStruct((n, inner), F32), compiler_params=cp,
        name="ssd_step_post",
    )(y, xs, z, mp["dexp"], mp["ng"])
    return y, nbuf, new_ssm


def _router_kernel(x_ref, g_ref, rwt_ref, rb_ref, xn_ref, idx_ref, gate_ref, rank_ref, cnt_ref, cnt_sc):
    tm = x_ref.shape[0]

    @pl.when(pl.program_id(0) == 0)
    def _():
        cnt_sc[...] = jnp.zeros_like(cnt_sc)

    xn = _rms(x_ref[...], g_ref[...], NORM_EPS)
    xn_ref[...] = xn
    logits = _dot_nt(rwt_ref[...], xn, HIGHEST) + rb_ref[...]
    e_iota = lax.broadcasted_iota(jnp.int32, logits.shape, 0)
    m1 = jnp.max(logits, axis=0, keepdims=True)
    i1 = jnp.min(jnp.where(logits == m1, e_iota, N_EXPERTS), axis=0, keepdims=True)
    rest = jnp.where(e_iota == i1, -jnp.inf, logits)
    m2 = jnp.max(rest, axis=0, keepdims=True)
    i2 = jnp.min(jnp.where(rest == m2, e_iota, N_EXPERTS), axis=0, keepdims=True)
    e2 = jnp.exp(m2 - m1)
    denom = 1.0 + e2
    idx_ref[...] = jnp.concatenate([i1, i2], axis=0)
    gate_ref[...] = jnp.concatenate([1.0 / denom, e2 / denom], axis=0)
    oh1 = (e_iota == i1).astype(F32)
    oh2 = (e_iota == i2).astype(F32)
    oh = oh1 + oh2
    ti = lax.broadcasted_iota(jnp.int32, (tm, tm), 0)
    si = lax.broadcasted_iota(jnp.int32, (tm, tm), 1)
    before = _dot(oh.astype(BF16), (ti < si).astype(BF16)) + cnt_sc[:, 0:1]
    rank_ref[...] = jnp.concatenate(
        [jnp.sum(oh1 * before, axis=0, keepdims=True), jnp.sum(oh2 * before, axis=0, keepdims=True)],
        axis=0).astype(jnp.int32)
    cnt_sc[...] = cnt_sc[...] + jnp.sum(oh, axis=1, keepdims=True)
    cnt_ref[...] = cnt_sc[...].astype(jnp.int32)


def moe_router(x, g, router_w, router_b, tm):
    t_rows, d = x.shape
    return pl.pallas_call(
        _router_kernel,
        grid=(t_rows // tm,),
        in_specs=[pl.BlockSpec((tm, d), lambda i: (i, 0)),
                  pl.BlockSpec((1, d), lambda i: (0, 0)),
                  pl.BlockSpec((N_EXPERTS, d), lambda i: (0, 0)),
                  pl.BlockSpec((N_EXPERTS, 1), lambda i: (0, 0))],
        out_specs=[pl.BlockSpec((tm, d), lambda i: (i, 0)),
                   pl.BlockSpec((2, tm), lambda i: (0, i)),
                   pl.BlockSpec((2, tm), lambda i: (0, i)),
                   pl.BlockSpec((2, tm), lambda i: (0, i)),
                   pl.BlockSpec((N_EXPERTS, LANES), lambda i: (0, 0))],
        out_shape=[jax.ShapeDtypeStruct((t_rows, d), F32),
                   jax.ShapeDtypeStruct((2, t_rows), jnp.int32),
                   jax.ShapeDtypeStruct((2, t_rows), F32),
                   jax.ShapeDtypeStruct((2, t_rows), jnp.int32),
                   jax.ShapeDtypeStruct((N_EXPERTS, LANES), jnp.int32)],
        scratch_shapes=[pltpu.VMEM((N_EXPERTS, LANES), F32)],
        compiler_params=_cparams(1),
        name="moe_router",
    )(x, g.reshape(1, d), router_w.T, router_b.reshape(N_EXPERTS, 1))


def _row_copy(src_ref, src_row, dst_ref, dst_row, sem):
    return pltpu.make_async_copy(src_ref.at[pl.ds(src_row, 1), :], dst_ref.at[pl.ds(dst_row, 1), :], sem)


def _dispatch_kernel(dest_ref, x_ref, slots_in_ref, slots_ref, sem, *, t_rows):
    del slots_in_ref
    tm = x_ref.shape[0]
    base = pl.program_id(0) * tm

    def copies(r):
        return [_row_copy(x_ref, r, slots_ref, dest_ref[k * t_rows + base + r], sem) for k in range(2)]

    def start(r, carry):
        for cp in copies(r):
            cp.start()
        return carry

    def wait(r, carry):
        for cp in copies(r):
            cp.wait()
        return carry

    lax.fori_loop(0, tm, start, 0)
    lax.fori_loop(0, tm, wait, 0)


def moe_dispatch(xn, dest_flat, n_slots, tm):
    t_rows, d = xn.shape
    return pl.pallas_call(
        functools.partial(_dispatch_kernel, t_rows=t_rows),
        grid_spec=pltpu.PrefetchScalarGridSpec(
            num_scalar_prefetch=1,
            grid=(t_rows // tm,),
            in_specs=[pl.BlockSpec((tm, d), lambda i, dest: (i, 0)),
                      pl.BlockSpec(memory_space=pl.ANY)],
            out_specs=pl.BlockSpec(memory_space=pl.ANY),
            scratch_shapes=[pltpu.SemaphoreType.DMA(())]),
        out_shape=jax.ShapeDtypeStruct((n_slots, d), F32),
        input_output_aliases={2: 0},
        compiler_params=_cparams(1),
        name="moe_dispatch",
    )(dest_flat, xn, jnp.zeros((n_slots, d), F32))


def _moe_kernel(te_ref, tv_ref, x_ref, wg_ref, wu_ref, wd_ref, o_ref, xb_sc, acc_sc):
    i, j = pl.program_id(0), pl.program_id(1)
    valid = tv_ref[i] == 1

    @pl.when(j == 0)
    def _():
        xb_sc[...] = x_ref[...].astype(BF16)
        acc_sc[...] = jnp.zeros_like(acc_sc)

    @pl.when(valid)
    def _():
        xb = xb_sc[...]
        hg = _dot(xb, wg_ref[0].astype(BF16))
        hu = _dot(xb, wu_ref[0].astype(BF16))
        h = (_silu(hg) * hu).astype(BF16)
        acc_sc[...] += _dot(h, wd_ref[0].astype(BF16))

    @pl.when(j == pl.num_programs(1) - 1)
    def _():
        o_ref[...] = acc_sc[...]


def moe_experts(slots, tile_expert, tile_valid, wg, wu, wd, tm, tf):
    n_slots, d = slots.shape
    d_ff = wg.shape[2]
    n_f = d_ff // tf

    def f_idx(i, j, te, tv):
        return jnp.where(tv[i] == 1, j, n_f - 1)

    return pl.pallas_call(
        _moe_kernel,
        grid_spec=pltpu.PrefetchScalarGridSpec(
            num_scalar_prefetch=2,
            grid=(n_slots // tm, n_f),
            in_specs=[pl.BlockSpec((tm, d), lambda i, j, te, tv: (i, 0)),
                      pl.BlockSpec((1, d, tf), lambda i, j, te, tv: (te[i], 0, f_idx(i, j, te, tv))),
                      pl.BlockSpec((1, d, tf), lambda i, j, te, tv: (te[i], 0, f_idx(i, j, te, tv))),
                      pl.BlockSpec((1, tf, d), lambda i, j, te, tv: (te[i], f_idx(i, j, te, tv), 0))],
            out_specs=pl.BlockSpec((tm, d), lambda i, j, te, tv: (i, 0)),
            scratch_shapes=[pltpu.VMEM((tm, d), BF16), pltpu.VMEM((tm, d), F32)]),
        out_shape=jax.ShapeDtypeStruct((n_slots, d), F32),
        compiler_params=_cparams(2),
        name="moe_experts",
    )(tile_expert, tile_valid, slots, wg, wu, wd)


def _combine_kernel(dest_ref, x_ref, gates_ref, g_ref, y_hbm_ref, o_ref, buf_sc, sem, *, t_rows):
    tm = x_ref.shape[0]
    base = pl.program_id(0) * tm

    def copies(r):
        return [_row_copy(y_hbm_ref, dest_ref[k * t_rows + base + r], buf_sc.at[k], r, sem)
                for k in range(2)]

    def start(r, carry):
        for cp in copies(r):
            cp.start()
        return carry

    def wait(r, carry):
        for cp in copies(r):
            cp.wait()
        return carry

    lax.fori_loop(0, tm, start, 0)
    lax.fori_loop(0, tm, wait, 0)
    gates = gates_ref[...]
    out = x_ref[...] + (gates[:, 0:1] * buf_sc[0] + gates[:, 1:2] * buf_sc[1])
    o_ref[...] = _rms(out, g_ref[...], NORM_EPS)


def moe_combine(x, gates_col, dest_flat, y_slots, g_final, tm):
    t_rows, d = x.shape
    return pl.pallas_call(
        functools.partial(_combine_kernel, t_rows=t_rows),
        grid_spec=pltpu.PrefetchScalarGridSpec(
            num_scalar_prefetch=1,
            grid=(t_rows // tm,),
            in_specs=[pl.BlockSpec((tm, d), lambda i, dest: (i, 0)),
                      pl.BlockSpec((tm, 2), lambda i, dest: (i, 0)),
                      pl.BlockSpec((1, d), lambda i, dest: (0, 0)),
                      pl.BlockSpec(memory_space=pl.ANY)],
            out_specs=pl.BlockSpec((tm, d), lambda i, dest: (i, 0)),
            scratch_shapes=[pltpu.VMEM((2, tm, d), F32), pltpu.SemaphoreType.DMA(())]),
        out_shape=jax.ShapeDtypeStruct((t_rows, d), F32),
        compiler_params=_cparams(1),
        name="moe_combine",
    )(dest_flat, x, gates_col, g_final.reshape(1, d), y_slots)


def moe_final(x, g_ffn, g_final, router_w, router_b, wg, wu, wd):
    t_rows, d = x.shape
    rt = _row_tile(t_rows, ROUTER_TILE) if t_rows % LANES else ROUTER_TILE
    if t_rows % rt or rt % LANES:
        rt = t_rows
    xn, idx, gates, rank, counts = moe_router(x, g_ffn, router_w, router_b, rt)
    counts = counts[:, 0]
    tm = MOE_TILE
    n_tiles = -(-2 * t_rows // tm) + N_EXPERTS
    padded = (counts + tm - 1) // tm * tm
    pend = jnp.cumsum(padded)
    pstart = pend - padded
    dest = jnp.sum(jnp.where(idx[:, :, None] == jnp.arange(N_EXPERTS)[None, None, :], pstart[None, None, :], 0),
                   axis=-1) + rank
    dest_flat = dest.reshape(-1).astype(jnp.int32)
    tile_start = jnp.arange(n_tiles, dtype=jnp.int32) * tm
    tile_valid = (tile_start < pend[-1]).astype(jnp.int32)
    last_start = jnp.maximum(pend[-1] - tm, 0)
    tile_expert = jnp.minimum(jnp.searchsorted(pend, jnp.minimum(tile_start, last_start), side="right"),
                              N_EXPERTS - 1).astype(jnp.int32)
    gt = _row_tile(t_rows, GATHER_TILE)
    slots = moe_dispatch(xn, dest_flat, n_tiles * tm, gt)
    y_slots = moe_experts(slots, tile_expert, tile_valid, wg, wu, wd, tm, MOE_FF_TILE)
    return moe_combine(x, gates.T, dest_flat, y_slots, g_final, gt)


def _block_diag(w):
    h, i, j = w.shape
    eye = jnp.eye(h, dtype=w.dtype)
    return jnp.einsum("hij,hg->higj", w, eye).reshape(h * i, h * j)


def kernel(x_prompt, x_sample, state_lru_conv, state_lru_h, state_rwkv_shift, state_rwkv_wkv, state_ssm_conv, state_ssm, norm_mix, norm_ffn, norm_final, w_in0, lru_conv_w, lru_conv_b, lru_wa, lru_ba, lru_wx, lru_bx, lru_lambda, rwkv_mu, rwkv_w0, rwkv_w_decay_up, rwkv_a0, rwkv_w_iclr_up, rwkv_w_gate_up, rwkv_k_k, rwkv_k_a, rwkv_r_k, rwkv_ln_w, rwkv_ln_b, w_out0, ffn_wg, ffn_wu, ffn_wd, w_in1, ssm_conv_w, ssm_conv_b, ssm_dt_bias, ssm_a_log, ssm_d, ssm_norm_g, w_out1, router_w, router_b, moe_wg, moe_wu, moe_wd):
    nb, seq, d = x_prompt.shape
    ns = x_sample.shape[0]
    tp = nb * seq
    t_all = tp + ns
    lru_w = lru_conv_w.shape[-1]
    rw_w = rwkv_w0.shape[-1]
    shift_cols = rwkv_mu.shape[-1]
    n_rheads = rw_w // RWKV_HEAD
    inner = ssm_norm_g.shape[-1]
    n_sheads = ssm_a_log.shape[-1]
    conv_ch = ssm_conv_w.shape[-1]

    x = jnp.concatenate([x_prompt.reshape(tp, d), x_sample.reshape(ns, d)], axis=0)
    tm = _row_tile(t_all, 1032)

    row = lambda v: v.reshape(1, -1)
    lp = dict(cw=lru_conv_w[0], cb=row(lru_conv_b[0]), wa=_block_diag(lru_wa[0]).astype(BF16),
              ba=row(lru_ba[0]), wx=_block_diag(lru_wx[0]).astype(BF16), bx=row(lru_bx[0]),
              lam=row(lru_lambda[0]))
    hsum = _block_diag(jnp.ones((n_rheads, RWKV_HEAD, RWKV_HEAD), F32))
    rp = dict(mu=row(rwkv_mu[0]), w0=row(rwkv_w0[0]), wdec=rwkv_w_decay_up[0], a0=row(rwkv_a0[0]),
              wiclr=rwkv_w_iclr_up[0], wgate=rwkv_w_gate_up[0], kk=row(rwkv_k_k[0]), ka=row(rwkv_k_a[0]),
              rk=row(rwkv_r_k[0]), lnw=row(rwkv_ln_w[0]), lnb=row(rwkv_ln_b[0]), hsum=hsum)
    mp = dict(cw=ssm_conv_w[0], cb=row(ssm_conv_b[0]), dtb=row(ssm_dt_bias[0]), alog=row(ssm_a_log[0]),
              dexp=row(jnp.repeat(ssm_d[0], SSM_HEAD)), ng=row(ssm_norm_g[0]))

    tn0 = 256
    proj_lru = norm_matmul(x, norm_mix[0], w_in0[0], 0, 2 * lru_w // tn0, tn0, tm, "in0_lru")
    proj_rwkv = norm_matmul(x, norm_mix[0], w_in0[0], 2 * lru_w // tn0, shift_cols // tn0, tn0, tm, "in0_rwkv")

    zeros = lambda *shape: jnp.zeros(shape, F32)
    out_a_p, p_lru_conv, p_lru_h = lru_prompt(proj_lru, nb, seq, zeros(nb, 3, lru_w), zeros(nb, lru_w), lp)
    out_b_p, p_shift, p_wkv = rwkv_prompt(proj_rwkv, nb, seq, zeros(nb, shift_cols),
                                          zeros(nb, n_rheads, RWKV_HEAD, RWKV_HEAD), rp)
    s_lru = proj_lru[tp:]
    out_a_s, s_lru_conv, s_lru_h = lru_step(s_lru[:, :lru_w], s_lru[:, lru_w:], state_lru_conv[0],
                                            state_lru_h[0], lp)
    s_p_rwkv = proj_rwkv[tp:]
    out_b_s, s_wkv = rwkv_step(s_p_rwkv, state_rwkv_shift[0], state_rwkv_wkv[0], rp)

    out_a = jnp.concatenate([out_a_p, out_a_s], axis=0)
    out_b = jnp.concatenate([out_b_p, out_b_s], axis=0)
    x = matmul_residual([out_a, out_b], w_out0[0], x, tm, 512, "out0")
    x = ffn_residual(x, norm_ffn[0], ffn_wg[0], ffn_wu[0], ffn_wd[0], tm, FFN_FF_TILE)

    tn1 = 512
    proj1 = norm_matmul(x, norm_mix[1], w_in1[0], 0, (inner + conv_ch) // tn1, tn1, tm, "in1_main")
    dt_raw = norm_matmul(x, norm_mix[1], w_in1[0][:, inner + conv_ch:], 0, 1, n_sheads, tm, "in1_dt")
    y_p, p_ssm_conv, p_ssm = ssd_prompt(proj1, dt_raw, nb, seq, zeros(nb, 3, conv_ch),
                                        zeros(nb, n_sheads, SSM_HEAD, SSM_STATE), mp)
    s_proj1 = proj1[tp:]
    y_s, s_ssm_conv, s_ssm = ssd_step(s_proj1[:, :inner], s_proj1[:, inner:], dt_raw[tp:], state_ssm_conv[0],
                                      state_ssm[0], mp)
    y_mix = jnp.concatenate([y_p, y_s], axis=0)
    x = matmul_residual([y_mix], w_out1[0], x, tm, 512, "out1")
    y = moe_final(x, norm_ffn[1], norm_final, router_w[0], router_b[0], moe_wg[0], moe_wu[0], moe_wd[0])

    return (y[:tp].reshape(nb, seq, d), y[tp:].reshape(ns, 1, d),
            p_lru_conv[None], p_lru_h.reshape(1, nb, lru_w), p_shift.reshape(1, nb, shift_cols), p_wkv[None],
            p_ssm_conv[None], p_ssm[None],
            s_lru_conv.reshape(1, ns, 3, lru_w), s_lru_h[None], s_p_rwkv[None], s_wkv[None],
            s_ssm_conv.reshape(1, ns, 3, conv_ch), s_ssm[None])
```

```python
import functools

import jax
import jax.numpy as jnp
from jax import lax
from jax.experimental import pallas as pl
from jax.experimental.pallas import tpu as pltpu

F32 = jnp.float32
BF16 = jnp.bfloat16
HIGHEST = lax.Precision.HIGHEST

NORM_EPS = 1e-6
CONV_W = 4
LRU_HEADS = 8
LRU_C = 8.0
RWKV_HEAD = 64
DECAY_RANK = 64
ICLR_RANK = 64
GATE_RANK = 128
RWKV_GN_EPS = 64e-5
SSM_HEAD = 64
SSM_GROUPS = 8
SSM_STATE = 128
SSM_CHUNK = 128
SSM_NORM_EPS = 1e-5
N_EXPERTS = 8

V7X_VMEM_BYTES = 64 * 1024 * 1024
VMEM_LIMIT = V7X_VMEM_BYTES - 8 * 1024 * 1024
SUBLANES = 8
LANES = 128

RWKV_CHUNK = 64
LRU_CHUNK = 256
MOE_TILE = 512
MOE_FF_TILE = 512
FFN_FF_TILE = 512
ROUTER_TILE = 384
GATHER_TILE = 384


def _cparams(n_axes):
    return pltpu.CompilerParams(dimension_semantics=("arbitrary",) * n_axes,
                                vmem_limit_bytes=VMEM_LIMIT)


def _row_tile(n_rows, cap):
    best = None
    for t in range(SUBLANES, min(cap, n_rows) + 1, SUBLANES):
        if n_rows % t == 0:
            best = t
    assert best is not None, (n_rows, cap)
    return best


def _dot(a, b, precision=None):
    return jnp.dot(a, b, preferred_element_type=F32, precision=precision)


def _dot_nt(a, b, precision=None):
    return lax.dot_general(a, b, (((1,), (1,)), ((), ())), preferred_element_type=F32,
                           precision=precision)


def _dot_tn(a, b, precision=None):
    return lax.dot_general(a, b, (((0,), (0,)), ((), ())), preferred_element_type=F32,
                           precision=precision)


_NN = (((1,), (0,)), ((), ()))
_NT = (((1,), (1,)), ((), ()))
_TN = (((0,), (0,)), ((), ()))


def _split2(x):
    hi = x.astype(BF16)
    lo = (x - hi.astype(F32)).astype(BF16)
    return hi, lo


def _split3(x):
    hi = x.astype(BF16)
    r1 = x - hi.astype(F32)
    mid = r1.astype(BF16)
    lo = (r1 - mid.astype(F32)).astype(BF16)
    return hi, mid, lo


def _dg(a, b, dims):
    return lax.dot_general(a, b, dims, preferred_element_type=F32)


def _dot3(a2, b2, dims=_NN):
    (ah, al), (bh, bl) = a2, b2
    return _dg(ah, bh, dims) + _dg(al, bh, dims) + _dg(ah, bl, dims)


def _dot_exact_rhs(x, m_bf16):
    return sum(_dg(part, m_bf16, _NN) for part in _split3(x))


def _dot_exact_lhs(m_bf16, x):
    return sum(_dg(m_bf16, part, _NN) for part in _split3(x))


def _softplus(x):
    return jnp.maximum(x, 0.0) + jnp.log1p(jnp.exp(-jnp.abs(x)))


def _silu(x):
    return x * jax.nn.sigmoid(x)


def _gelu_tanh(x):
    return 0.5 * x * (1.0 + jnp.tanh(0.7978845608028654 * (x + 0.044715 * (x * x * x))))


def _rms(x, g, eps):
    return x * lax.rsqrt(jnp.mean(x * x, axis=-1, keepdims=True) + eps) * g


def _norm_mm_kernel(x_ref, g_ref, w_ref, o_ref, xn_sc):
    @pl.when(pl.program_id(1) == 0)
    def _():
        xn_sc[...] = _rms(x_ref[...], g_ref[...], NORM_EPS).astype(BF16)

    o_ref[...] = _dot(xn_sc[...], w_ref[...].astype(BF16))


def norm_matmul(x, g, w, col0_blk, n_blk, tn, tm, name):
    t_rows, d = x.shape
    return pl.pallas_call(
        _norm_mm_kernel,
        grid=(t_rows // tm, n_blk),
        in_specs=[pl.BlockSpec((tm, d), lambda i, j: (i, 0)),
                  pl.BlockSpec((1, d), lambda i, j: (0, 0)),
                  pl.BlockSpec((d, tn), lambda i, j: (0, j + col0_blk))],
        out_specs=pl.BlockSpec((tm, tn), lambda i, j: (i, j)),
        out_shape=jax.ShapeDtypeStruct((t_rows, n_blk * tn), F32),
        scratch_shapes=[pltpu.VMEM((tm, d), BF16)],
        compiler_params=_cparams(2),
        name=name,
    )(x, g.reshape(1, d), w)


def _mm_res_kernel(*refs, n_in):
    x_refs, w_refs = refs[:n_in], refs[n_in:2 * n_in]
    res_ref, o_ref = refs[2 * n_in], refs[2 * n_in + 1]
    acc = res_ref[...]
    for x_ref, w_ref in zip(x_refs, w_refs):
        acc = acc + _dot(x_ref[...].astype(BF16), w_ref[...].astype(BF16))
    o_ref[...] = acc


def matmul_residual(xs, w, res, tm, tn, name):
    n_in = len(xs)
    t_rows, kp = xs[0].shape
    n_cols = w.shape[1]
    in_specs = [pl.BlockSpec((tm, kp), lambda i, j: (i, 0)) for _ in xs]
    in_specs += [pl.BlockSpec((kp, tn), functools.partial(lambda i, j, p: (p, j), p=p))
                 for p in range(n_in)]
    in_specs += [pl.BlockSpec((tm, tn), lambda i, j: (i, j))]
    return pl.pallas_call(
        functools.partial(_mm_res_kernel, n_in=n_in),
        grid=(t_rows // tm, n_cols // tn),
        in_specs=in_specs,
        out_specs=pl.BlockSpec((tm, tn), lambda i, j: (i, j)),
        out_shape=jax.ShapeDtypeStruct((t_rows, n_cols), F32),
        compiler_params=_cparams(2),
        name=name,
    )(*xs, *([w] * n_in), res)


def _ffn_kernel(x_ref, g_ref, wg_ref, wu_ref, wd_ref, o_ref, xn_sc, acc_sc):
    j = pl.program_id(1)

    @pl.when(j == 0)
    def _():
        xn_sc[...] = _rms(x_ref[...], g_ref[...], NORM_EPS).astype(BF16)
        acc_sc[...] = jnp.zeros_like(acc_sc)

    xn = xn_sc[...]
    hg = _dot(xn, wg_ref[...].astype(BF16))
    hu = _dot(xn, wu_ref[...].astype(BF16))
    h = (_silu(hg) * hu).astype(BF16)
    acc_sc[...] += _dot(h, wd_ref[...].astype(BF16))

    @pl.when(j == pl.num_programs(1) - 1)
    def _():
        o_ref[...] = x_ref[...] + acc_sc[...]


def ffn_residual(x, g, wg, wu, wd, tm, tf):
    t_rows, d = x.shape
    d_ff = wg.shape[1]
    return pl.pallas_call(
        _ffn_kernel,
        grid=(t_rows // tm, d_ff // tf),
        in_specs=[pl.BlockSpec((tm, d), lambda i, j: (i, 0)),
                  pl.BlockSpec((1, d), lambda i, j: (0, 0)),
                  pl.BlockSpec((d, tf), lambda i, j: (0, j)),
                  pl.BlockSpec((d, tf), lambda i, j: (0, j)),
                  pl.BlockSpec((tf, d), lambda i, j: (j, 0))],
        out_specs=pl.BlockSpec((tm, d), lambda i, j: (i, 0)),
        out_shape=jax.ShapeDtypeStruct((t_rows, d), F32),
        scratch_shapes=[pltpu.VMEM((tm, d), BF16), pltpu.VMEM((tm, d), F32)],
        compiler_params=_cparams(2),
        name="ffn_swiglu",
    )(x, g.reshape(1, d), wg, wu, wd)


def _conv4(u, u1, u2, u3, cw, cb):
    return cb + cw[3:4] * u + cw[2:3] * u1 + cw[1:2] * u2 + cw[0:1] * u3


def _lru_gates(xc, wa, ba, wx, bx, lam):
    xb = xc.astype(BF16)
    r = jax.nn.sigmoid(_dot(xb, wa) + ba)
    i = jax.nn.sigmoid(_dot(xb, wx) + bx)
    log_a = -LRU_C * r * _softplus(-lam)
    a = jnp.exp(log_a)
    u = jnp.sqrt(1.0 - jnp.exp(2.0 * log_a)) * (i * xc)
    return a, u


def _lru_prompt_kernel(x_ref, g_ref, conv0_ref, h0_ref, cw_ref, cb_ref, wa_ref, ba_ref, wx_ref,
                       bx_ref, lam_ref, o_ref, nconv_ref, nh_ref, ext_sc, h_sc, *, lc):
    width = x_ref.shape[1]

    @pl.when(pl.program_id(1) == 0)
    def _():
        ext_sc[0:SUBLANES, :] = jnp.zeros((SUBLANES, width), F32)
        ext_sc[SUBLANES - 3:SUBLANES, :] = conv0_ref[0]
        h_sc[...] = h0_ref[0]

    u = x_ref[...]
    ext_sc[SUBLANES:SUBLANES + lc, :] = u
    xc = _conv4(u, ext_sc[SUBLANES - 1:SUBLANES - 1 + lc, :], ext_sc[SUBLANES - 2:SUBLANES - 2 + lc, :],
                ext_sc[SUBLANES - 3:SUBLANES - 3 + lc, :], cw_ref[...], cb_ref[...])
    tail = ext_sc[lc + SUBLANES - 3:lc + SUBLANES, :]
    ext_sc[SUBLANES - 3:SUBLANES, :] = tail
    nconv_ref[0] = tail

    a, h = _lru_gates(xc, wa_ref[...], ba_ref[...], wx_ref[...], bx_ref[...], lam_ref[...])
    row = lax.broadcasted_iota(jnp.int32, (lc, width), 0)
    s = 1
    while s < lc:
        keep = row >= s
        a_sh = jnp.where(keep, pltpu.roll(a, s, 0), 1.0)
        h_sh = jnp.where(keep, pltpu.roll(h, s, 0), 0.0)
        h = a * h_sh + h
        a = a * a_sh
        s *= 2
    hs = h + a * h_sc[...]
    h_last = hs[lc - 1:lc, :]
    h_sc[...] = h_last
    nh_ref[0] = h_last
    o_ref[...] = hs * _gelu_tanh(g_ref[...])


def lru_prompt(proj_lru, n_batch, seq, conv0, h0, lp):
    width = conv0.shape[-1]
    lc = min(LRU_CHUNK, seq)
    n_chunks = seq // lc
    full = lambda shape: pl.BlockSpec(shape, lambda b, c: (0,) * len(shape))
    return pl.pallas_call(
        functools.partial(_lru_prompt_kernel, lc=lc),
        grid=(n_batch, n_chunks),
        in_specs=[pl.BlockSpec((lc, width), lambda b, c: (b * n_chunks + c, 0)),
                  pl.BlockSpec((lc, width), lambda b, c: (b * n_chunks + c, 1)),
                  pl.BlockSpec((1, 3, width), lambda b, c: (b, 0, 0)),
                  pl.BlockSpec((1, 1, width), lambda b, c: (b, 0, 0)),
                  full((CONV_W, width)), full((1, width)), full((width, width)), full((1, width)),
                  full((width, width)), full((1, width)), full((1, width))],
        out_specs=[pl.BlockSpec((lc, width), lambda b, c: (b * n_chunks + c, 0)),
                   pl.BlockSpec((1, 3, width), lambda b, c: (b, 0, 0)),
                   pl.BlockSpec((1, 1, width), lambda b, c: (b, 0, 0))],
        out_shape=[jax.ShapeDtypeStruct((n_batch * seq, width), F32),
                   jax.ShapeDtypeStruct((n_batch, 3, width), F32),
                   jax.ShapeDtypeStruct((n_batch, 1, width), F32)],
        scratch_shapes=[pltpu.VMEM((lc + SUBLANES, width), F32), pltpu.VMEM((1, width), F32)],
        compiler_params=_cparams(2),
        name="lru_prompt",
    )(proj_lru, proj_lru, conv0, h0.reshape(n_batch, 1, width), lp["cw"], lp["cb"], lp["wa"], lp["ba"],
      lp["wx"], lp["bx"], lp["lam"])


def _lru_step_kernel(x_ref, g_ref, buf_ref, h0_ref, cw_ref, cb_ref, wa_ref, ba_ref, wx_ref, bx_ref,
                     lam_ref, o_ref, nbuf_ref, nh_ref):
    width = x_ref.shape[1]
    u = x_ref[...]
    b0, b1, b2 = (buf_ref[:, k * width:(k + 1) * width] for k in range(3))
    xc = _conv4(u, b2, b1, b0, cw_ref[...], cb_ref[...])
    a, uu = _lru_gates(xc, wa_ref[...], ba_ref[...], wx_ref[...], bx_ref[...], lam_ref[...])
    h = a * h0_ref[...] + uu
    nh_ref[...] = h
    o_ref[...] = h * _gelu_tanh(g_ref[...])
    nbuf_ref[:, 0:width] = b1
    nbuf_ref[:, width:2 * width] = b2
    nbuf_ref[:, 2 * width:3 * width] = u


def lru_step(x_lru, g_lru, conv_buf, h0, lp):
    n, width = x_lru.shape
    return pl.pallas_call(
        _lru_step_kernel,
        out_shape=[jax.ShapeDtypeStruct((n, width), F32),
                   jax.ShapeDtypeStruct((n, 3 * width), F32),
                   jax.ShapeDtypeStruct((n, width), F32)],
        compiler_params=pltpu.CompilerParams(vmem_limit_bytes=VMEM_LIMIT),
        name="lru_step",
    )(x_lru, g_lru, conv_buf.reshape(n, 3 * width), h0, lp["cw"], lp["cb"], lp["wa"], lp["ba"],
      lp["wx"], lp["bx"], lp["lam"])


def _rwkv_rows(mixed, rp):
    w = rp["w0"].shape[1]
    r, k, v = mixed[:, 0:w], mixed[:, w:2 * w], mixed[:, 2 * w:3 * w]
    o = 3 * w
    wd = mixed[:, o:o + DECAY_RANK]
    ad = mixed[:, o + DECAY_RANK:o + DECAY_RANK + ICLR_RANK]
    gd = mixed[:, o + DECAY_RANK + ICLR_RANK:o + DECAY_RANK + ICLR_RANK + GATE_RANK]
    dec_in = rp["w0"] + _dot(jnp.tanh(wd).astype(BF16), rp["wdec"].astype(BF16))
    w_log = -_softplus(-dec_in) - 0.5
    lw = -jnp.exp(w_log)
    iclr = jax.nn.sigmoid(rp["a0"] + _dot(ad.astype(BF16), rp["wiclr"].astype(BF16)))
    gate = _dot(jax.nn.sigmoid(gd).astype(BF16), rp["wgate"].astype(BF16))
    kk = k * rp["kk"]
    ss = _dot_exact_rhs(kk * kk, rp["hsum"])
    kkn = kk / jnp.maximum(jnp.sqrt(ss), 1e-12)
    k2 = k * (1.0 + (iclr - 1.0) * rp["ka"])
    return r, k2, v, lw, -kkn, kkn * iclr, gate


def _rwkv_post(o, r, k2, v, gate, rp):
    inv = 1.0 / RWKV_HEAD
    mu = _dot_exact_rhs(o, rp["hsum"]) * inv
    d = o - mu
    var = _dot_exact_rhs(d * d, rp["hsum"]) * inv
    on = d * lax.rsqrt(var + RWKV_GN_EPS) * rp["lnw"] + rp["lnb"]
    bonus = _dot_exact_rhs(r * k2 * rp["rk"], rp["hsum"]) * v
    return (on + bonus) * gate


_RWKV_PARAM_NAMES = ("mu", "w0", "wdec", "a0", "wiclr", "wgate", "kk", "ka", "rk", "lnw", "lnb", "hsum")


def _rwkv_prompt_kernel(p_ref, shift0_ref, s0_ref, *rest, chunk):
    n_prm = len(_RWKV_PARAM_NAMES)
    rp = {name: ref[...] for name, ref in zip(_RWKV_PARAM_NAMES, rest[:n_prm])}
    o_ref, nshift_ref, nwkv_ref, prev_sc, s_sc, o_sc = rest[n_prm:]
    n_heads = s_sc.shape[0]
    hd = RWKV_HEAD

    @pl.when(pl.program_id(1) == 0)
    def _():
        prev_sc[...] = shift0_ref[0]
        s_sc[...] = s0_ref[0]

    p = p_ref[...]
    row = lax.broadcasted_iota(jnp.int32, p.shape, 0)
    p_prev = jnp.where(row >= 1, pltpu.roll(p, 1, 0), prev_sc[...])
    last = p[chunk - 1:chunk, :]
    prev_sc[...] = last
    nshift_ref[0] = last
    mixed = p + (p_prev - p) * rp["mu"]
    r, k2, v, lw, a, b, gate = _rwkv_rows(mixed, rp)

    ti = lax.broadcasted_iota(jnp.int32, (chunk, chunk), 0)
    si = lax.broadcasted_iota(jnp.int32, (chunk, chunk), 1)
    strict, incl = ti > si, ti >= si
    cs = _dot_exact_lhs(incl.astype(BF16), lw)
    g_in, g_ex, g_inv = jnp.exp(cs), jnp.exp(cs - lw), jnp.exp(-cs)
    at, bt, kt, rt = a * g_ex, b * g_inv, k2 * g_inv, r * g_in
    g_end = g_in[chunk - 1:chunk, :]
    bc, kc = bt * g_end, kt * g_end
    eye = (lax.broadcasted_iota(jnp.int32, (hd, hd), 0) == lax.broadcasted_iota(jnp.int32, (hd, hd), 1))
    n_fac = max(1, (chunk - 1).bit_length())

    heads = range(n_heads)
    sls = [slice(h * hd, (h + 1) * hd) for h in heads]
    ahs, rhs, vhs = ([x[:, sl] for sl in sls] for x in (at, rt, v))
    ms = [_dot3(_split2(jnp.concatenate([ahs[h], rhs[h]], axis=0)),
                _split2(jnp.concatenate([bt[:, sls[h]], kt[:, sls[h]]], axis=0)), _NT)
          for h in heads]
    labs = [jnp.where(strict, m[:chunk, :chunk], 0.0) for m in ms]
    lrbs = [jnp.where(incl, m[chunk:, :chunk], 0.0).astype(BF16) for m in ms]
    lkvs = [_dot3(_split2(jnp.concatenate([jnp.where(strict, ms[h][:chunk, chunk:], 0.0),
                                           jnp.where(incl, ms[h][chunk:, chunk:], 0.0)], axis=0)),
                  _split2(vhs[h])) for h in heads]
    ys = [jnp.concatenate([ahs[h], lkvs[h][:chunk]], axis=1) for h in heads]
    pws = [_split2(lab) for lab in labs]
    for f in range(n_fac):
        ys = [ys[h] + _dot3(pws[h], _split2(ys[h])) for h in heads]
        if f + 1 < n_fac:
            pws = [_split2(_dot3(pw, pw)) for pw in pws]
    y2s = [_split2(y) for y in ys]
    qos = [jnp.concatenate([rhs[h], lkvs[h][chunk:]], axis=1) + _dg(lrbs[h], y2s[h][0], _NN)
           for h in heads]
    mns = [_dot3(y2s[h], _split2(bc[:, sls[h]]), _TN) for h in heads]
    m_hs = [_split2(jnp.where(eye, g_end[:, sls[h]], 0.0) + mns[h][:hd]) for h in heads]
    n0s = [mns[h][hd:] + _dot3(_split2(vhs[h]), _split2(kc[:, sls[h]]), _TN) for h in heads]
    s0s = [_split2(s_sc[h]) for h in heads]
    for h in heads:
        o_sc[:, sls[h]] = _dg(qos[h][:, :hd].astype(BF16), s0s[h][0], _NT) + qos[h][:, hd:]
    for h in heads:
        s_sc[h] = _dot3(s0s[h], m_hs[h]) + n0s[h]

    @pl.when(pl.program_id(1) == pl.num_programs(1) - 1)
    def _():
        nwkv_ref[0] = s_sc[...]

    o_ref[...] = _rwkv_post(o_sc[...], r, k2, v, gate, rp)


def _rwkv_param_list(rp):
    return [rp[name] for name in _RWKV_PARAM_NAMES]


def rwkv_prompt(p_rwkv, n_batch, seq, shift0, wkv0, rp):
    cols = shift0.shape[-1]
    n_heads, hd = wkv0.shape[1], wkv0.shape[2]
    width = n_heads * hd
    chunk = min(RWKV_CHUNK, seq)
    n_chunks = seq // chunk
    prm = _rwkv_param_list(rp)
    prm_specs = [pl.BlockSpec(x.shape, lambda b, c: (0, 0)) for x in prm]
    return pl.pallas_call(
        functools.partial(_rwkv_prompt_kernel, chunk=chunk),
        grid=(n_batch, n_chunks),
        in_specs=[pl.BlockSpec((chunk, cols), lambda b, c: (b * n_chunks + c, 0)),
                  pl.BlockSpec((1, 1, cols), lambda b, c: (b, 0, 0)),
                  pl.BlockSpec((1, n_heads, hd, hd), lambda b, c: (b, 0, 0, 0))] + prm_specs,
        out_specs=[pl.BlockSpec((chunk, width), lambda b, c: (b * n_chunks + c, 0)),
                   pl.BlockSpec((1, 1, cols), lambda b, c: (b, 0, 0)),
                   pl.BlockSpec((1, n_heads, hd, hd), lambda b, c: (b, 0, 0, 0))],
        out_shape=[jax.ShapeDtypeStruct((n_batch * seq, width), F32),
                   jax.ShapeDtypeStruct((n_batch, 1, cols), F32),
                   jax.ShapeDtypeStruct((n_batch, n_heads, hd, hd), F32)],
        scratch_shapes=[pltpu.VMEM((1, cols), F32), pltpu.VMEM((n_heads, hd, hd), F32),
                        pltpu.VMEM((chunk, width), F32)],
        compiler_params=_cparams(2),
        name="rwkv_prompt",
    )(p_rwkv, shift0.reshape(n_batch, 1, cols), wkv0, *prm)


def _rwkv_step_pre_kernel(p_ref, prev_ref, *rest):
    n_prm = len(_RWKV_PARAM_NAMES)
    rp = {name: ref[...] for name, ref in zip(_RWKV_PARAM_NAMES, rest[:n_prm])}
    r_ref, k_ref, v_ref, w_ref, a_ref, b_ref, gate_ref = rest[n_prm:]
    p = p_ref[...]
    mixed = p + (prev_ref[...] - p) * rp["mu"]
    r, k2, v, lw, a, b, gate = _rwkv_rows(mixed, rp)
    r_ref[...] = r
    k_ref[...] = k2
    v_ref[...] = v
    w_ref[...] = jnp.exp(lw)
    a_ref[...] = a
    b_ref[...] = b
    gate_ref[...] = gate


def _rwkv_step_kernel(s_ref, w_ref, a_ref, b_ref, k_ref, r_ref, vt_ref, ns_ref, ot_ref):
    bt, n_heads = s_ref.shape[0], s_ref.shape[1]
    lane = lax.broadcasted_iota(jnp.int32, ot_ref.shape[1:], 1)
    for i in range(bt):
        vt = vt_ref[i]
        ss = [s_ref[i, h] for h in range(n_heads)]
        sas = [jnp.sum(ss[h] * a_ref[i, h], axis=-1, keepdims=True) for h in range(n_heads)]
        s_news = [ss[h] * w_ref[i, h] + sas[h] * b_ref[i, h] + vt[:, h:h + 1] * k_ref[i, h]
                  for h in range(n_heads)]
        for h in range(n_heads):
            ns_ref[i, h] = s_news[h]
        os_ = [jnp.sum(s_news[h] * r_ref[i, h], axis=-1, keepdims=True) for h in range(n_heads)]
        ot = jnp.zeros(ot_ref.shape[1:], F32)
        for h in range(n_heads):
            ot = jnp.where(lane == h, os_[h], ot)
        ot_ref[i] = ot


def _rwkv_step_post_kernel(o_ref, r_ref, k_ref, v_ref, gate_ref, *rest):
    n_prm = len(_RWKV_PARAM_NAMES)
    rp = {name: ref[...] for name, ref in zip(_RWKV_PARAM_NAMES, rest[:n_prm])}
    out_ref = rest[n_prm]
    out_ref[...] = _rwkv_post(o_ref[...], r_ref[...], k_ref[...], v_ref[...], gate_ref[...], rp)


def rwkv_step(p_rwkv, shift_prev, wkv0, rp):
    n = p_rwkv.shape[0]
    n_heads, hd = wkv0.shape[1], wkv0.shape[2]
    width = n_heads * hd
    prm = _rwkv_param_list(rp)
    cp = pltpu.CompilerParams(vmem_limit_bytes=VMEM_LIMIT)
    row = jax.ShapeDtypeStruct((n, width), F32)
    r, k2, v, w, a, b, gate = pl.pallas_call(
        _rwkv_step_pre_kernel, out_shape=[row] * 7, compiler_params=cp, name="rwkv_step_pre",
    )(p_rwkv, shift_prev, *prm)
    hrow = lambda z: z.reshape(n, n_heads, 1, hd)
    vt = jnp.transpose(v.reshape(n, n_heads, hd), (0, 2, 1))
    bt = SUBLANES
    vec_spec = pl.BlockSpec((bt, n_heads, 1, hd), lambda i: (i, 0, 0, 0))
    st_spec = pl.BlockSpec((bt, n_heads, hd, hd), lambda i: (i, 0, 0, 0))
    t_spec = pl.BlockSpec((bt, hd, n_heads), lambda i: (i, 0, 0))
    new_wkv, ot = pl.pallas_call(
        _rwkv_step_kernel,
        grid=(n // bt,),
        in_specs=[st_spec] + [vec_spec] * 5 + [t_spec],
        out_specs=[st_spec, t_spec],
        out_shape=[jax.ShapeDtypeStruct(wkv0.shape, F32), jax.ShapeDtypeStruct((n, hd, n_heads), F32)],
        compiler_params=_cparams(1),
        name="rwkv_step",
    )(wkv0, hrow(w), hrow(a), hrow(b), hrow(k2), hrow(r), vt)
    o = jnp.transpose(ot, (0, 2, 1)).reshape(n, width)
    out_b = pl.pallas_call(
        _rwkv_step_post_kernel, out_shape=row, compiler_params=cp, name="rwkv_step_post",
    )(o, r, k2, v, gate, *prm)
    return out_b, new_wkv


def _mamba_post(y, xs, z, dexp, ng):
    y = (y + dexp * xs) * _silu(z)
    gw = y.shape[1] // SSM_GROUPS
    parts = []
    for g in range(SSM_GROUPS):
        yg = y[:, g * gw:(g + 1) * gw]
        parts.append(yg * lax.rsqrt(jnp.mean(yg * yg, axis=-1, keepdims=True) + SSM_NORM_EPS))
    return jnp.concatenate(parts, axis=1) * ng


def _ssd_prompt_kernel(z_ref, xlo_ref, xhi_ref, dt_ref, conv0_ref, s0_ref, cw_ref, cb_ref, dtb_ref,
                       alog_ref, dexp_ref, ng_ref, hexp_ref, qexp_ref, y_ref, nconv_ref, nssm_ref,
                       ext_sc, s_sc, y_sc, yo_sc, *, q):
    inner = xlo_ref.shape[1]
    n_heads = s_sc.shape[0]
    hpg = n_heads // SSM_GROUPS

    @pl.when(pl.program_id(1) == 0)
    def _():
        ext_sc[0:SUBLANES, :] = jnp.zeros((SUBLANES, ext_sc.shape[1]), F32)
        ext_sc[SUBLANES - 3:SUBLANES, :] = conv0_ref[0]
        s_sc[...] = s0_ref[0]

    ext_sc[SUBLANES:SUBLANES + q, 0:inner] = xlo_ref[...]
    ext_sc[SUBLANES:SUBLANES + q, inner:] = xhi_ref[...]
    xbc = _silu(_conv4(ext_sc[SUBLANES:SUBLANES + q, :], ext_sc[SUBLANES - 1:SUBLANES - 1 + q, :],
                       ext_sc[SUBLANES - 2:SUBLANES - 2 + q, :], ext_sc[SUBLANES - 3:SUBLANES - 3 + q, :],
                       cw_ref[...], cb_ref[...]))
    tail = ext_sc[q + SUBLANES - 3:q + SUBLANES, :]
    ext_sc[SUBLANES - 3:SUBLANES, :] = tail
    nconv_ref[0] = tail

    xs = xbc[:, 0:inner]
    gn = SSM_GROUPS * SSM_STATE
    bm = xbc[:, inner:inner + gn].astype(BF16)
    cm = xbc[:, inner + gn:].astype(BF16)
    dt = _softplus(dt_ref[...] + dtb_ref[...])
    dta = dt * (-jnp.exp(alog_ref[...]))
    ti = lax.broadcasted_iota(jnp.int32, (q, q), 0)
    si = lax.broadcasted_iota(jnp.int32, (q, q), 1)
    causal = ti >= si
    da = _dot_exact_lhs(causal.astype(BF16), dta)
    upper = (ti <= si).astype(BF16)
    da_t = sum(_dg(part, upper, _TN) for part in _split3(dta))
    da_end = da[q - 1:q, :]
    end_decay = jnp.exp(da_end)
    hexp, qexp = hexp_ref[...], qexp_ref[...]
    xdt = xs * _dot_exact_rhs(dt, hexp)
    xdt_b = xdt.astype(BF16)
    xdt_end = (xdt * _dot_exact_rhs(jnp.exp(da_end - da), hexp)).astype(BF16)
    from_start = _dot_exact_rhs(jnp.exp(da), hexp)
    da_col = _dot_exact_rhs(da, qexp)

    heads = range(n_heads)
    bgs = [bm[:, g * SSM_STATE:(g + 1) * SSM_STATE] for g in range(SSM_GROUPS)]
    cgs = [cm[:, g * SSM_STATE:(g + 1) * SSM_STATE] for g in range(SSM_GROUPS)]
    scores = [_dot_nt(cgs[g], bgs[g]) for g in range(SSM_GROUPS)]
    hss = [slice(h * SSM_HEAD, (h + 1) * SSM_HEAD) for h in heads]
    s_hs = [s_sc[h] for h in heads]
    for h in heads:
        yo_sc[:, hss[h]] = _dot_nt(cgs[h // hpg], s_hs[h].astype(BF16))
    wts = [(scores[h // hpg]
            * jnp.exp(jnp.where(causal, da_col[:, h * q:(h + 1) * q] - da_t[h:h + 1, :], -jnp.inf))
            ).astype(BF16) for h in heads]
    for h in heads:
        y_sc[:, hss[h]] = _dot(wts[h], xdt_b[:, hss[h]])
    upd = [_dot_tn(xdt_end[:, hss[h]], bgs[h // hpg]) for h in heads]
    for h in heads:
        s_sc[h] = s_hs[h] * end_decay[:, h:h + 1] + upd[h]

    @pl.when(pl.program_id(1) == pl.num_programs(1) - 1)
    def _():
        nssm_ref[0] = s_sc[...]

    y = y_sc[...] + yo_sc[...] * from_start
    y_ref[...] = _mamba_post(y, xs, z_ref[...], dexp_ref[...], ng_ref[...])


def ssd_prompt(proj1, dt_raw, n_batch, seq, conv0, ssm0, mp):
    n_heads, hd, n_state = ssm0.shape[1:]
    inner = n_heads * hd
    conv_ch = conv0.shape[-1]
    q = SSM_CHUNK if seq % SSM_CHUNK == 0 else seq
    n_chunks = seq // q
    full = lambda shape: pl.BlockSpec(shape, lambda b, c: (0,) * len(shape))
    return pl.pallas_call(
        functools.partial(_ssd_prompt_kernel, q=q),
        grid=(n_batch, n_chunks),
        in_specs=[pl.BlockSpec((q, inner), lambda b, c: (b * n_chunks + c, 0)),
                  pl.BlockSpec((q, inner), lambda b, c: (b * n_chunks + c, 1)),
                  pl.BlockSpec((q, inner), lambda b, c: (b * n_chunks + c, 2)),
                  pl.BlockSpec((q, n_heads), lambda b, c: (b * n_chunks + c, 0)),
                  pl.BlockSpec((1, 3, conv_ch), lambda b, c: (b, 0, 0)),
                  pl.BlockSpec((1, n_heads, hd, n_state), lambda b, c: (b, 0, 0, 0)),
                  full((CONV_W, conv_ch)), full((1, conv_ch)), full((1, n_heads)), full((1, n_heads)),
                  full((1, inner)), full((1, inner)), full((n_heads, inner)), full((n_heads, n_heads * q))],
        out_specs=[pl.BlockSpec((q, inner), lambda b, c: (b * n_chunks + c, 0)),
                   pl.BlockSpec((1, 3, conv_ch), lambda b, c: (b, 0, 0)),
                   pl.BlockSpec((1, n_heads, hd, n_state), lambda b, c: (b, 0, 0, 0))],
        out_shape=[jax.ShapeDtypeStruct((n_batch * seq, inner), F32),
                   jax.ShapeDtypeStruct((n_batch, 3, conv_ch), F32),
                   jax.ShapeDtypeStruct(ssm0.shape, F32)],
        scratch_shapes=[pltpu.VMEM((q + SUBLANES, conv_ch), F32), pltpu.VMEM((n_heads, hd, n_state), F32),
                        pltpu.VMEM((q, inner), F32), pltpu.VMEM((q, inner), F32)],
        compiler_params=_cparams(2),
        name="ssd_prompt",
    )(proj1, proj1, proj1, dt_raw, conv0, ssm0, mp["cw"], mp["cb"], mp["dtb"], mp["alog"], mp["dexp"],
      mp["ng"], jnp.repeat(jnp.eye(n_heads, dtype=BF16), hd, axis=1),
      jnp.repeat(jnp.eye(n_heads, dtype=BF16), q, axis=1))


def _ssd_step_pre_kernel(x_ref, dt_ref, buf_ref, cw_ref, cb_ref, dtb_ref, alog_ref,
                         xs_ref, b_ref, c_ref, dt_out_ref, dec_ref, nbuf_ref):
    ch = x_ref.shape[1]
    inner = xs_ref.shape[1]
    gn = b_ref.shape[1]
    u = x_ref[...]
    b0, b1, b2 = (buf_ref[:, k * ch:(k + 1) * ch] for k in range(3))
    xbc = _silu(_conv4(u, b2, b1, b0, cw_ref[...], cb_ref[...]))
    xs_ref[...] = xbc[:, 0:inner]
    b_ref[...] = xbc[:, inner:inner + gn]
    c_ref[...] = xbc[:, inner + gn:]
    dt = _softplus(dt_ref[...] + dtb_ref[...])
    dt_out_ref[...] = dt
    dec_ref[...] = jnp.exp(dt * (-jnp.exp(alog_ref[...])))
    nbuf_ref[:, 0:ch] = b1
    nbuf_ref[:, ch:2 * ch] = b2
    nbuf_ref[:, 2 * ch:3 * ch] = u


def _ssd_step_kernel(s_ref, xt_ref, b_ref, c_ref, dt_ref, dec_ref, ns_ref, yt_ref):
    bt, n_heads = s_ref.shape[0], s_ref.shape[1]
    hpg = n_heads // SSM_GROUPS
    lane = lax.broadcasted_iota(jnp.int32, yt_ref.shape[1:], 1)
    for i in range(bt):
        xt = xt_ref[i]
        dt = dt_ref[i]
        dec = dec_ref[i]
        xdt = xt * dt
        s_news = [s_ref[i, h] * dec[:, h:h + 1] + xdt[:, h:h + 1] * b_ref[i, h // hpg]
                  for h in range(n_heads)]
        for h in range(n_heads):
            ns_ref[i, h] = s_news[h]
        ys = [jnp.sum(s_news[h] * c_ref[i, h // hpg], axis=-1, keepdims=True) for h in range(n_heads)]
        yt = jnp.zeros(yt_ref.shape[1:], F32)
        for h in range(n_heads):
            yt = jnp.where(lane == h, ys[h], yt)
        yt_ref[i] = yt


def _ssd_step_post_kernel(y_ref, xs_ref, z_ref, dexp_ref, ng_ref, o_ref):
    o_ref[...] = _mamba_post(y_ref[...], xs_ref[...], z_ref[...], dexp_ref[...], ng_ref[...])


def ssd_step(z, xbc_raw, dt_raw, conv_buf, ssm0, mp):
    n, conv_ch = xbc_raw.shape
    n_heads, hd, n_state = ssm0.shape[1:]
    inner = n_heads * hd
    gn = SSM_GROUPS * n_state
    cp = pltpu.CompilerParams(vmem_limit_bytes=VMEM_LIMIT)
    xs, bm, cm, dt, dec, nbuf = pl.pallas_call(
        _ssd_step_pre_kernel,
        out_shape=[jax.ShapeDtypeStruct((n, inner), F32), jax.ShapeDtypeStruct((n, gn), F32),
                   jax.ShapeDtypeStruct((n, gn), F32), jax.ShapeDtypeStruct((n, n_heads), F32),
                   jax.ShapeDtypeStruct((n, n_heads), F32), jax.ShapeDtypeStruct((n, 3 * conv_ch), F32)],
        compiler_params=cp, name="ssd_step_pre",
    )(xbc_raw, dt_raw, conv_buf.reshape(n, 3 * conv_ch), mp["cw"], mp["cb"], mp["dtb"], mp["alog"])
    xt = jnp.transpose(xs.reshape(n, n_heads, hd), (0, 2, 1))
    bt = 4
    st_spec = pl.BlockSpec((bt, n_heads, hd, n_state), lambda i: (i, 0, 0, 0))
    t_spec = pl.BlockSpec((bt, hd, n_heads), lambda i: (i, 0, 0))
    g_spec = pl.BlockSpec((bt, SSM_GROUPS, 1, n_state), lambda i: (i, 0, 0, 0))
    h_spec = pl.BlockSpec((bt, 1, n_heads), lambda i: (i, 0, 0))
    new_ssm, yt = pl.pallas_call(
        _ssd_step_kernel,
        grid=(n // bt,),
        in_specs=[st_spec, t_spec, g_spec, g_spec, h_spec, h_spec],
        out_specs=[st_spec, t_spec],
        out_shape=[jax.ShapeDtypeStruct(ssm0.shape, F32), jax.ShapeDtypeStruct((n, hd, n_heads), F32)],
        compiler_params=_cparams(1),
        name="ssd_step",
    )(ssm0, xt, bm.reshape(n, SSM_GROUPS, 1, n_state), cm.reshape(n, SSM_GROUPS, 1, n_state),
      dt.reshape(n, 1, n_heads), dec.reshape(n, 1, n_heads))
    y = jnp.transpose(yt, (0, 2, 1)).reshape(n, inner)
    y = pl.pallas_call(
        _ssd_step_post_kernel, out_shape=jax.ShapeDtypeStruct((n, inner), F32), compiler_params=cp,
        name="ssd_step_post",
    )(y, xs, z, mp["dexp"], mp["ng"])
    return y, nbuf, new_ssm


def _router_kernel(x_ref, g_ref, rwt_ref, rb_ref, xn_ref, idx_ref, gate_ref, rank_ref, cnt_ref, cnt_sc):
    tm = x_ref.shape[0]

    @pl.when(pl.program_id(0) == 0)
    def _():
        cnt_sc[...] = jnp.zeros_like(cnt_sc)

    xn = _rms(x_ref[...], g_ref[...], NORM_EPS)
    xn_ref[...] = xn
    logits = _dot_nt(rwt_ref[...], xn, HIGHEST) + rb_ref[...]
    e_iota = lax.broadcasted_iota(jnp.int32, logits.shape, 0)
    m1 = jnp.max(logits, axis=0, keepdims=True)
    i1 = jnp.min(jnp.where(logits == m1, e_iota, N_EXPERTS), axis=0, keepdims=True)
    rest = jnp.where(e_iota == i1, -jnp.inf, logits)
    m2 = jnp.max(rest, axis=0, keepdims=True)
    i2 = jnp.min(jnp.where(rest == m2, e_iota, N_EXPERTS), axis=0, keepdims=True)
    e2 = jnp.exp(m2 - m1)
    denom = 1.0 + e2
    idx_ref[...] = jnp.concatenate([i1, i2], axis=0)
    gate_ref[...] = jnp.concatenate([1.0 / denom, e2 / denom], axis=0)
    oh1 = (e_iota == i1).astype(F32)
    oh2 = (e_iota == i2).astype(F32)
    oh = oh1 + oh2
    ti = lax.broadcasted_iota(jnp.int32, (tm, tm), 0)
    si = lax.broadcasted_iota(jnp.int32, (tm, tm), 1)
    before = _dot(oh.astype(BF16), (ti < si).astype(BF16)) + cnt_sc[:, 0:1]
    rank_ref[...] = jnp.concatenate(
        [jnp.sum(oh1 * before, axis=0, keepdims=True), jnp.sum(oh2 * before, axis=0, keepdims=True)],
        axis=0).astype(jnp.int32)
    cnt_sc[...] = cnt_sc[...] + jnp.sum(oh, axis=1, keepdims=True)
    cnt_ref[...] = cnt_sc[...].astype(jnp.int32)


def moe_router(x, g, router_w, router_b, tm):
    t_rows, d = x.shape
    return pl.pallas_call(
        _router_kernel,
        grid=(t_rows // tm,),
        in_specs=[pl.BlockSpec((tm, d), lambda i: (i, 0)),
                  pl.BlockSpec((1, d), lambda i: (0, 0)),
                  pl.BlockSpec((N_EXPERTS, d), lambda i: (0, 0)),
                  pl.BlockSpec((N_EXPERTS, 1), lambda i: (0, 0))],
        out_specs=[pl.BlockSpec((tm, d), lambda i: (i, 0)),
                   pl.BlockSpec((2, tm), lambda i: (0, i)),
                   pl.BlockSpec((2, tm), lambda i: (0, i)),
                   pl.BlockSpec((2, tm), lambda i: (0, i)),
                   pl.BlockSpec((N_EXPERTS, LANES), lambda i: (0, 0))],
        out_shape=[jax.ShapeDtypeStruct((t_rows, d), F32),
                   jax.ShapeDtypeStruct((2, t_rows), jnp.int32),
                   jax.ShapeDtypeStruct((2, t_rows), F32),
                   jax.ShapeDtypeStruct((2, t_rows), jnp.int32),
                   jax.ShapeDtypeStruct((N_EXPERTS, LANES), jnp.int32)],
        scratch_shapes=[pltpu.VMEM((N_EXPERTS, LANES), F32)],
        compiler_params=_cparams(1),
        name="moe_router",
    )(x, g.reshape(1, d), router_w.T, router_b.reshape(N_EXPERTS, 1))


def _row_copy(src_ref, src_row, dst_ref, dst_row, sem):
    return pltpu.make_async_copy(src_ref.at[pl.ds(src_row, 1), :], dst_ref.at[pl.ds(dst_row, 1), :], sem)


def _dispatch_kernel(dest_ref, x_ref, slots_in_ref, slots_ref, sem, *, t_rows):
    del slots_in_ref
    tm = x_ref.shape[0]
    base = pl.program_id(0) * tm

    def copies(r):
        return [_row_copy(x_ref, r, slots_ref, dest_ref[k * t_rows + base + r], sem) for k in range(2)]

    def start(r, carry):
        for cp in copies(r):
            cp.start()
        return carry

    def wait(r, carry):
        for cp in copies(r):
            cp.wait()
        return carry

    lax.fori_loop(0, tm, start, 0)
    lax.fori_loop(0, tm, wait, 0)


def moe_dispatch(xn, dest_flat, n_slots, tm):
    t_rows, d = xn.shape
    return pl.pallas_call(
        functools.partial(_dispatch_kernel, t_rows=t_rows),
        grid_spec=pltpu.PrefetchScalarGridSpec(
            num_scalar_prefetch=1,
            grid=(t_rows // tm,),
            in_specs=[pl.BlockSpec((tm, d), lambda i, dest: (i, 0)),
                      pl.BlockSpec(memory_space=pl.ANY)],
            out_specs=pl.BlockSpec(memory_space=pl.ANY),
            scratch_shapes=[pltpu.SemaphoreType.DMA(())]),
        out_shape=jax.ShapeDtypeStruct((n_slots, d), F32),
        input_output_aliases={2: 0},
        compiler_params=_cparams(1),
        name="moe_dispatch",
    )(dest_flat, xn, jnp.zeros((n_slots, d), F32))


def _moe_kernel(te_ref, tv_ref, x_ref, wg_ref, wu_ref, wd_ref, o_ref, xb_sc, acc_sc):
    i, j = pl.program_id(0), pl.program_id(1)
    valid = tv_ref[i] == 1

    @pl.when(j == 0)
    def _():
        xb_sc[...] = x_ref[...].astype(BF16)
        acc_sc[...] = jnp.zeros_like(acc_sc)

    @pl.when(valid)
    def _():
        xb = xb_sc[...]
        hg = _dot(xb, wg_ref[0].astype(BF16))
        hu = _dot(xb, wu_ref[0].astype(BF16))
        h = (_silu(hg) * hu).astype(BF16)
        acc_sc[...] += _dot(h, wd_ref[0].astype(BF16))

    @pl.when(j == pl.num_programs(1) - 1)
    def _():
        o_ref[...] = acc_sc[...]


def moe_experts(slots, tile_expert, tile_valid, wg, wu, wd, tm, tf):
    n_slots, d = slots.shape
    d_ff = wg.shape[2]
    n_f = d_ff // tf

    def f_idx(i, j, te, tv):
        return jnp.where(tv[i] == 1, j, n_f - 1)

    return pl.pallas_call(
        _moe_kernel,
        grid_spec=pltpu.PrefetchScalarGridSpec(
            num_scalar_prefetch=2,
            grid=(n_slots // tm, n_f),
            in_specs=[pl.BlockSpec((tm, d), lambda i, j, te, tv: (i, 0)),
                      pl.BlockSpec((1, d, tf), lambda i, j, te, tv: (te[i], 0, f_idx(i, j, te, tv))),
                      pl.BlockSpec((1, d, tf), lambda i, j, te, tv: (te[i], 0, f_idx(i, j, te, tv))),
                      pl.BlockSpec((1, tf, d), lambda i, j, te, tv: (te[i], f_idx(i, j, te, tv), 0))],
            out_specs=pl.BlockSpec((tm, d), lambda i, j, te, tv: (i, 0)),
            scratch_shapes=[pltpu.VMEM((tm, d), BF16), pltpu.VMEM((tm, d), F32)]),
        out_shape=jax.ShapeDtypeStruct((n_slots, d), F32),
        compiler_params=_cparams(2),
        name="moe_experts",
    )(tile_expert, tile_valid, slots, wg, wu, wd)


def _combine_kernel(dest_ref, x_ref, gates_ref, g_ref, y_hbm_ref, o_ref, buf_sc, sem, *, t_rows):
    tm = x_ref.shape[0]
    base = pl.program_id(0) * tm

    def copies(r):
        return [_row_copy(y_hbm_ref, dest_ref[k * t_rows + base + r], buf_sc.at[k], r, sem)
                for k in range(2)]

    def start(r, carry):
        for cp in copies(r):
            cp.start()
        return carry

    def wait(r, carry):
        for cp in copies(r):
            cp.wait()
        return carry

    lax.fori_loop(0, tm, start, 0)
    lax.fori_loop(0, tm, wait, 0)
    gates = gates_ref[...]
    out = x_ref[...] + (gates[:, 0:1] * buf_sc[0] + gates[:, 1:2] * buf_sc[1])
    o_ref[...] = _rms(out, g_ref[...], NORM_EPS)


def moe_combine(x, gates_col, dest_flat, y_slots, g_final, tm):
    t_rows, d = x.shape
    return pl.pallas_call(
        functools.partial(_combine_kernel, t_rows=t_rows),
        grid_spec=pltpu.PrefetchScalarGridSpec(
            num_scalar_prefetch=1,
            grid=(t_rows // tm,),
            in_specs=[pl.BlockSpec((tm, d), lambda i, dest: (i, 0)),
                      pl.BlockSpec((tm, 2), lambda i, dest: (i, 0)),
                      pl.BlockSpec((1, d), lambda i, dest: (0, 0)),
                      pl.BlockSpec(memory_space=pl.ANY)],
            out_specs=pl.BlockSpec((tm, d), lambda i, dest: (i, 0)),
            scratch_shapes=[pltpu.VMEM((2, tm, d), F32), pltpu.SemaphoreType.DMA(())]),
        out_shape=jax.ShapeDtypeStruct((t_rows, d), F32),
        compiler_params=_cparams(1),
        name="moe_combine",
    )(dest_flat, x, gates_col, g_final.reshape(1, d), y_slots)


def moe_final(x, g_ffn, g_final, router_w, router_b, wg, wu, wd):
    t_rows, d = x.shape
    rt = _row_tile(t_rows, ROUTER_TILE) if t_rows % LANES else ROUTER_TILE
    if t_rows % rt or rt % LANES:
        rt = t_rows
    xn, idx, gates, rank, counts = moe_router(x, g_ffn, router_w, router_b, rt)
    counts = counts[:, 0]
    tm = MOE_TILE
    n_tiles = -(-2 * t_rows // tm) + N_EXPERTS
    padded = (counts + tm - 1) // tm * tm
    pend = jnp.cumsum(padded)
    pstart = pend - padded
    dest = jnp.sum(jnp.where(idx[:, :, None] == jnp.arange(N_EXPERTS)[None, None, :], pstart[None, None, :], 0),
                   axis=-1) + rank
    dest_flat = dest.reshape(-1).astype(jnp.int32)
    tile_start = jnp.arange(n_tiles, dtype=jnp.int32) * tm
    tile_valid = (tile_start < pend[-1]).astype(jnp.int32)
    last_start = jnp.maximum(pend[-1] - tm, 0)
    probe = jnp.minimum(tile_start, last_start)
    tile_expert = jnp.minimum(jnp.sum((pend[None, :] <= probe[:, None]).astype(jnp.int32), axis=1),
                              N_EXPERTS - 1)
    gt = _row_tile(t_rows, GATHER_TILE)
    slots = moe_dispatch(xn, dest_flat, n_tiles * tm, gt)
    y_slots = moe_experts(slots, tile_expert, tile_valid, wg, wu, wd, tm, MOE_FF_TILE)
    return moe_combine(x, gates.T, dest_flat, y_slots, g_final, gt)


def _block_diag(w):
    h, i, j = w.shape
    eye = jnp.eye(h, dtype=w.dtype)
    return jnp.einsum("hij,hg->higj", w, eye).reshape(h * i, h * j)


def kernel(x_prompt, x_sample, state_lru_conv, state_lru_h, state_rwkv_shift, state_rwkv_wkv, state_ssm_conv, state_ssm, norm_mix, norm_ffn, norm_final, w_in0, lru_conv_w, lru_conv_b, lru_wa, lru_ba, lru_wx, lru_bx, lru_lambda, rwkv_mu, rwkv_w0, rwkv_w_decay_up, rwkv_a0, rwkv_w_iclr_up, rwkv_w_gate_up, rwkv_k_k, rwkv_k_a, rwkv_r_k, rwkv_ln_w, rwkv_ln_b, w_out0, ffn_wg, ffn_wu, ffn_wd, w_in1, ssm_conv_w, ssm_conv_b, ssm_dt_bias, ssm_a_log, ssm_d, ssm_norm_g, w_out1, router_w, router_b, moe_wg, moe_wu, moe_wd):
    nb, seq, d = x_prompt.shape
    ns = x_sample.shape[0]
    tp = nb * seq
    t_all = tp + ns
    lru_w = lru_conv_w.shape[-1]
    rw_w = rwkv_w0.shape[-1]
    shift_cols = rwkv_mu.shape[-1]
    n_rheads = rw_w // RWKV_HEAD
    inner = ssm_norm_g.shape[-1]
    n_sheads = ssm_a_log.shape[-1]
    conv_ch = ssm_conv_w.shape[-1]

    x = jnp.concatenate([x_prompt.reshape(tp, d), x_sample.reshape(ns, d)], axis=0)
    tm = _row_tile(t_all, 1032)

    row = lambda v: v.reshape(1, -1)
    lp = dict(cw=lru_conv_w[0], cb=row(lru_conv_b[0]), wa=_block_diag(lru_wa[0]).astype(BF16),
              ba=row(lru_ba[0]), wx=_block_diag(lru_wx[0]).astype(BF16), bx=row(lru_bx[0]),
              lam=row(lru_lambda[0]))
    hsum = _block_diag(jnp.ones((n_rheads, RWKV_HEAD, RWKV_HEAD), BF16))
    rp = dict(mu=row(rwkv_mu[0]), w0=row(rwkv_w0[0]), wdec=rwkv_w_decay_up[0], a0=row(rwkv_a0[0]),
              wiclr=rwkv_w_iclr_up[0], wgate=rwkv_w_gate_up[0], kk=row(rwkv_k_k[0]), ka=row(rwkv_k_a[0]),
              rk=row(rwkv_r_k[0]), lnw=row(rwkv_ln_w[0]), lnb=row(rwkv_ln_b[0]), hsum=hsum)
    mp = dict(cw=ssm_conv_w[0], cb=row(ssm_conv_b[0]), dtb=row(ssm_dt_bias[0]), alog=row(ssm_a_log[0]),
              dexp=row(jnp.repeat(ssm_d[0], SSM_HEAD)), ng=row(ssm_norm_g[0]))

    tn0 = 256
    proj_lru = norm_matmul(x, norm_mix[0], w_in0[0], 0, 2 * lru_w // tn0, tn0, tm, "in0_lru")
    proj_rwkv = norm_matmul(x, norm_mix[0], w_in0[0], 2 * lru_w // tn0, shift_cols // tn0, tn0, tm, "in0_rwkv")

    zeros = lambda *shape: jnp.zeros(shape, F32)
    out_a_p, p_lru_conv, p_lru_h = lru_prompt(proj_lru, nb, seq, zeros(nb, 3, lru_w), zeros(nb, lru_w), lp)
    out_b_p, p_shift, p_wkv = rwkv_prompt(proj_rwkv, nb, seq, zeros(nb, shift_cols),
                                          zeros(nb, n_rheads, RWKV_HEAD, RWKV_HEAD), rp)
    s_lru = proj_lru[tp:]
    out_a_s, s_lru_conv, s_lru_h = lru_step(s_lru[:, :lru_w], s_lru[:, lru_w:], state_lru_conv[0],
                                            state_lru_h[0], lp)
    s_p_rwkv = proj_rwkv[tp:]
    out_b_s, s_wkv = rwkv_step(s_p_rwkv, state_rwkv_shift[0], state_rwkv_wkv[0], rp)

    out_a = jnp.concatenate([out_a_p, out_a_s], axis=0)
    out_b = jnp.concatenate([out_b_p, out_b_s], axis=0)
    x = matmul_residual([out_a, out_b], w_out0[0], x, tm, 512, "out0")
    x = ffn_residual(x, norm_ffn[0], ffn_wg[0], ffn_wu[0], ffn_wd[0], tm, FFN_FF_TILE)

    tn1 = 512
    proj1 = norm_matmul(x, norm_mix[1], w_in1[0], 0, (inner + conv_ch) // tn1, tn1, tm, "in1_main")
    dt_raw = norm_matmul(x, norm_mix[1], w_in1[0][:, inner + conv_ch:], 0, 1, n_sheads, tm, "in1_dt")
    y_p, p_ssm_conv, p_ssm = ssd_prompt(proj1, dt_raw, nb, seq, zeros(nb, 3, conv_ch),
                                        zeros(nb, n_sheads, SSM_HEAD, SSM_STATE), mp)
    s_proj1 = proj1[tp:]
    y_s, s_ssm_conv, s_ssm = ssd_step(s_proj1[:, :inner], s_proj1[:, inner:], dt_raw[tp:], state_ssm_conv[0],
                                      state_ssm[0], mp)
    y_mix = jnp.concatenate([y_p, y_s], axis=0)
    x = matmul_residual([y_mix], w_out1[0], x, tm, 512, "out1")
    y = moe_final(x, norm_ffn[1], norm_final, router_w[0], router_b[0], moe_wg[0], moe_wu[0], moe_wd[0])

    return (y[:tp].reshape(nb, seq, d), y[tp:].reshape(ns, 1, d),
            p_lru_conv[None], p_lru_h.reshape(1, nb, lru_w), p_shift.reshape(1, nb, shift_cols), p_wkv[None],
            p_ssm_conv[None], p_ssm[None],
            s_lru_conv.reshape(1, ns, 3, lru_w), s_lru_h[None], s_p_rwkv[None], s_wkv[None],
            s_ssm_conv.reshape(1, ns, 3, conv_ch), s_ssm[None])
```

```python
import functools

import jax
import jax.numpy as jnp
from jax import lax
from jax.experimental import pallas as pl
from jax.experimental.pallas import tpu as pltpu

F32 = jnp.float32
BF16 = jnp.bfloat16
HIGHEST = lax.Precision.HIGHEST

NORM_EPS = 1e-6
CONV_W = 4
LRU_HEADS = 8
LRU_C = 8.0
RWKV_HEAD = 64
DECAY_RANK = 64
ICLR_RANK = 64
GATE_RANK = 128
RWKV_GN_EPS = 64e-5
SSM_HEAD = 64
SSM_GROUPS = 8
SSM_STATE = 128
SSM_CHUNK = 128
SSM_NORM_EPS = 1e-5
N_EXPERTS = 8

V7X_VMEM_BYTES = 64 * 1024 * 1024
VMEM_LIMIT = V7X_VMEM_BYTES - 8 * 1024 * 1024
SUBLANES = 8
LANES = 128

RWKV_CHUNK = 64
LRU_CHUNK = 256
MOE_TILE = 1024
MOE_FF_TILE = 512
FFN_FF_TILE = 512
ROUTER_TILE = 384
GATHER_TILE = 384
COMBINE_TILE = 512
DMA_UNROLL = 8


def _cparams(n_axes):
    return pltpu.CompilerParams(dimension_semantics=("arbitrary",) * n_axes,
                                vmem_limit_bytes=VMEM_LIMIT)


def _row_tile(n_rows, cap, mult=SUBLANES):
    best = None
    for t in range(mult, min(cap, n_rows) + 1, mult):
        if n_rows % t == 0:
            best = t
    assert best is not None, (n_rows, cap)
    return best


def _dot(a, b, precision=None):
    return jnp.dot(a, b, preferred_element_type=F32, precision=precision)


def _dot_nt(a, b, precision=None):
    return lax.dot_general(a, b, (((1,), (1,)), ((), ())), preferred_element_type=F32,
                           precision=precision)


def _dot_tn(a, b, precision=None):
    return lax.dot_general(a, b, (((0,), (0,)), ((), ())), preferred_element_type=F32,
                           precision=precision)


_NN = (((1,), (0,)), ((), ()))
_NT = (((1,), (1,)), ((), ()))
_TN = (((0,), (0,)), ((), ()))


def _split2(x):
    hi = x.astype(BF16)
    lo = (x - hi.astype(F32)).astype(BF16)
    return hi, lo


def _split3(x):
    hi = x.astype(BF16)
    r1 = x - hi.astype(F32)
    mid = r1.astype(BF16)
    lo = (r1 - mid.astype(F32)).astype(BF16)
    return hi, mid, lo


def _dg(a, b, dims):
    return lax.dot_general(a, b, dims, preferred_element_type=F32)


def _dot3(a2, b2, dims=_NN):
    (ah, al), (bh, bl) = a2, b2
    return _dg(ah, bh, dims) + _dg(al, bh, dims) + _dg(ah, bl, dims)


def _dot_exact_rhs(x, m_bf16):
    return sum(_dg(part, m_bf16, _NN) for part in _split3(x))


def _dot_exact_lhs(m_bf16, x):
    return sum(_dg(m_bf16, part, _NN) for part in _split3(x))


def _softplus(x):
    return jnp.maximum(x, 0.0) + jnp.log1p(jnp.exp(-jnp.abs(x)))


def _silu(x):
    return x * jax.nn.sigmoid(x)


def _gelu_tanh(x):
    return 0.5 * x * (1.0 + jnp.tanh(0.7978845608028654 * (x + 0.044715 * (x * x * x))))


def _rms(x, g, eps):
    return x * lax.rsqrt(jnp.mean(x * x, axis=-1, keepdims=True) + eps) * g


def _norm_mm_kernel(x_ref, g_ref, w_ref, o_ref, xn_sc):
    @pl.when(pl.program_id(1) == 0)
    def _():
        xn_sc[...] = _rms(x_ref[...], g_ref[...], NORM_EPS).astype(BF16)

    o_ref[...] = _dot(xn_sc[...], w_ref[...].astype(BF16)).astype(o_ref.dtype)


def norm_matmul(x, g, w, col0_blk, n_blk, tn, tm, name, out_dtype=F32):
    t_rows, d = x.shape
    return pl.pallas_call(
        _norm_mm_kernel,
        grid=(t_rows // tm, n_blk),
        in_specs=[pl.BlockSpec((tm, d), lambda i, j: (i, 0)),
                  pl.BlockSpec((1, d), lambda i, j: (0, 0)),
                  pl.BlockSpec((d, tn), lambda i, j: (0, j + col0_blk))],
        out_specs=pl.BlockSpec((tm, tn), lambda i, j: (i, j)),
        out_shape=jax.ShapeDtypeStruct((t_rows, n_blk * tn), out_dtype),
        scratch_shapes=[pltpu.VMEM((tm, d), BF16)],
        compiler_params=_cparams(2),
        name=name,
    )(x, g.reshape(1, d), w)


def _mm_res_kernel(*refs, n_in):
    x_refs, w_refs = refs[:n_in], refs[n_in:2 * n_in]
    res_ref, o_ref = refs[2 * n_in], refs[2 * n_in + 1]
    acc = res_ref[...]
    for x_ref, w_ref in zip(x_refs, w_refs):
        acc = acc + _dot(x_ref[...].astype(BF16), w_ref[...].astype(BF16))
    o_ref[...] = acc


def matmul_residual(xs, w, res, tm, tn, name):
    n_in = len(xs)
    t_rows, kp = xs[0].shape
    n_cols = w.shape[1]
    in_specs = [pl.BlockSpec((tm, kp), lambda i, j: (i, 0)) for _ in xs]
    in_specs += [pl.BlockSpec((kp, tn), functools.partial(lambda i, j, p: (p, j), p=p))
                 for p in range(n_in)]
    in_specs += [pl.BlockSpec((tm, tn), lambda i, j: (i, j))]
    return pl.pallas_call(
        functools.partial(_mm_res_kernel, n_in=n_in),
        grid=(t_rows // tm, n_cols // tn),
        in_specs=in_specs,
        out_specs=pl.BlockSpec((tm, tn), lambda i, j: (i, j)),
        out_shape=jax.ShapeDtypeStruct((t_rows, n_cols), F32),
        compiler_params=_cparams(2),
        name=name,
    )(*xs, *([w] * n_in), res)


def _ffn_kernel(x_ref, g_ref, wg_ref, wu_ref, wd_ref, o_ref, xn_sc, acc_sc):
    j = pl.program_id(1)

    @pl.when(j == 0)
    def _():
        xn_sc[...] = _rms(x_ref[...], g_ref[...], NORM_EPS).astype(BF16)
        acc_sc[...] = jnp.zeros_like(acc_sc)

    xn = xn_sc[...]
    hg = _dot(xn, wg_ref[...].astype(BF16))
    hu = _dot(xn, wu_ref[...].astype(BF16))
    h = (_silu(hg) * hu).astype(BF16)
    acc_sc[...] += _dot(h, wd_ref[...].astype(BF16))

    @pl.when(j == pl.num_programs(1) - 1)
    def _():
        o_ref[...] = x_ref[...] + acc_sc[...]


def ffn_residual(x, g, wg, wu, wd, tm, tf):
    t_rows, d = x.shape
    d_ff = wg.shape[1]
    return pl.pallas_call(
        _ffn_kernel,
        grid=(t_rows // tm, d_ff // tf),
        in_specs=[pl.BlockSpec((tm, d), lambda i, j: (i, 0)),
                  pl.BlockSpec((1, d), lambda i, j: (0, 0)),
                  pl.BlockSpec((d, tf), lambda i, j: (0, j)),
                  pl.BlockSpec((d, tf), lambda i, j: (0, j)),
                  pl.BlockSpec((tf, d), lambda i, j: (j, 0))],
        out_specs=pl.BlockSpec((tm, d), lambda i, j: (i, 0)),
        out_shape=jax.ShapeDtypeStruct((t_rows, d), F32),
        scratch_shapes=[pltpu.VMEM((tm, d), BF16), pltpu.VMEM((tm, d), F32)],
        compiler_params=_cparams(2),
        name="ffn_swiglu",
    )(x, g.reshape(1, d), wg, wu, wd)


def _conv4(u, u1, u2, u3, cw, cb):
    return cb + cw[3:4] * u + cw[2:3] * u1 + cw[1:2] * u2 + cw[0:1] * u3


def _lru_gates(xc, wa, ba, wx, bx, lam):
    xb = xc.astype(BF16)
    r = jax.nn.sigmoid(_dot(xb, wa) + ba)
    i = jax.nn.sigmoid(_dot(xb, wx) + bx)
    log_a = -LRU_C * r * _softplus(-lam)
    a = jnp.exp(log_a)
    u = jnp.sqrt(1.0 - jnp.exp(2.0 * log_a)) * (i * xc)
    return a, u


def _lru_prompt_kernel(x_ref, g_ref, conv0_ref, h0_ref, cw_ref, cb_ref, wa_ref, ba_ref, wx_ref,
                       bx_ref, lam_ref, o_ref, nconv_ref, nh_ref, ext_sc, h_sc, *, lc):
    width = x_ref.shape[1]

    @pl.when(pl.program_id(1) == 0)
    def _():
        ext_sc[0:SUBLANES, :] = jnp.zeros((SUBLANES, width), F32)
        ext_sc[SUBLANES - 3:SUBLANES, :] = conv0_ref[0]
        h_sc[...] = h0_ref[0]

    u = x_ref[...]
    ext_sc[SUBLANES:SUBLANES + lc, :] = u
    xc = _conv4(u, ext_sc[SUBLANES - 1:SUBLANES - 1 + lc, :], ext_sc[SUBLANES - 2:SUBLANES - 2 + lc, :],
                ext_sc[SUBLANES - 3:SUBLANES - 3 + lc, :], cw_ref[...], cb_ref[...])
    tail = ext_sc[lc + SUBLANES - 3:lc + SUBLANES, :]
    ext_sc[SUBLANES - 3:SUBLANES, :] = tail
    nconv_ref[0] = tail

    a, h = _lru_gates(xc, wa_ref[...], ba_ref[...], wx_ref[...], bx_ref[...], lam_ref[...])
    row = lax.broadcasted_iota(jnp.int32, (lc, width), 0)
    s = 1
    while s < lc:
        keep = row >= s
        a_sh = jnp.where(keep, pltpu.roll(a, s, 0), 1.0)
        h_sh = jnp.where(keep, pltpu.roll(h, s, 0), 0.0)
        h = a * h_sh + h
        a = a * a_sh
        s *= 2
    hs = h + a * h_sc[...]
    h_last = hs[lc - 1:lc, :]
    h_sc[...] = h_last
    nh_ref[0] = h_last
    o_ref[...] = hs * _gelu_tanh(g_ref[...])


def lru_prompt(proj_lru, n_batch, seq, conv0, h0, lp):
    width = conv0.shape[-1]
    lc = min(LRU_CHUNK, seq)
    n_chunks = seq // lc
    full = lambda shape: pl.BlockSpec(shape, lambda b, c: (0,) * len(shape))
    return pl.pallas_call(
        functools.partial(_lru_prompt_kernel, lc=lc),
        grid=(n_batch, n_chunks),
        in_specs=[pl.BlockSpec((lc, width), lambda b, c: (b * n_chunks + c, 0)),
                  pl.BlockSpec((lc, width), lambda b, c: (b * n_chunks + c, 1)),
                  pl.BlockSpec((1, 3, width), lambda b, c: (b, 0, 0)),
                  pl.BlockSpec((1, 1, width), lambda b, c: (b, 0, 0)),
                  full((CONV_W, width)), full((1, width)), full((width, width)), full((1, width)),
                  full((width, width)), full((1, width)), full((1, width))],
        out_specs=[pl.BlockSpec((lc, width), lambda b, c: (b * n_chunks + c, 0)),
                   pl.BlockSpec((1, 3, width), lambda b, c: (b, 0, 0)),
                   pl.BlockSpec((1, 1, width), lambda b, c: (b, 0, 0))],
        out_shape=[jax.ShapeDtypeStruct((n_batch * seq, width), F32),
                   jax.ShapeDtypeStruct((n_batch, 3, width), F32),
                   jax.ShapeDtypeStruct((n_batch, 1, width), F32)],
        scratch_shapes=[pltpu.VMEM((lc + SUBLANES, width), F32), pltpu.VMEM((1, width), F32)],
        compiler_params=_cparams(2),
        name="lru_prompt",
    )(proj_lru, proj_lru, conv0, h0.reshape(n_batch, 1, width), lp["cw"], lp["cb"], lp["wa"], lp["ba"],
      lp["wx"], lp["bx"], lp["lam"])


def _rows_call(body, inputs, out_shapes, name):
    return pl.pallas_call(body, out_shape=out_shapes, name=name,
                          compiler_params=pltpu.CompilerParams(vmem_limit_bytes=VMEM_LIMIT))(*inputs)


def _lru_step_kernel(x_ref, g_ref, buf_ref, h0_ref, cw_ref, cb_ref, wa_ref, ba_ref, wx_ref, bx_ref,
                     lam_ref, o_ref, nbuf_ref, nh_ref):
    width = x_ref.shape[1]
    u = x_ref[...]
    b0, b1, b2 = (buf_ref[:, k * width:(k + 1) * width] for k in range(3))
    xc = _conv4(u, b2, b1, b0, cw_ref[...], cb_ref[...])
    a, uu = _lru_gates(xc, wa_ref[...], ba_ref[...], wx_ref[...], bx_ref[...], lam_ref[...])
    h = a * h0_ref[...] + uu
    nh_ref[...] = h
    o_ref[...] = h * _gelu_tanh(g_ref[...])
    nbuf_ref[:, 0:width] = b1
    nbuf_ref[:, width:2 * width] = b2
    nbuf_ref[:, 2 * width:3 * width] = u


def lru_step(x_lru, g_lru, conv_buf, h0, lp):
    n, width = x_lru.shape
    return _rows_call(
        _lru_step_kernel,
        [x_lru, g_lru, conv_buf.reshape(n, 3 * width), h0, lp["cw"], lp["cb"], lp["wa"], lp["ba"],
         lp["wx"], lp["bx"], lp["lam"]],
        [jax.ShapeDtypeStruct((n, width), F32), jax.ShapeDtypeStruct((n, 3 * width), F32),
         jax.ShapeDtypeStruct((n, width), F32)],
        "lru_step")


def _rwkv_rows(mixed, rp):
    w = rp["w0"].shape[1]
    r, k, v = mixed[:, 0:w], mixed[:, w:2 * w], mixed[:, 2 * w:3 * w]
    o = 3 * w
    wd = mixed[:, o:o + DECAY_RANK]
    ad = mixed[:, o + DECAY_RANK:o + DECAY_RANK + ICLR_RANK]
    gd = mixed[:, o + DECAY_RANK + ICLR_RANK:o + DECAY_RANK + ICLR_RANK + GATE_RANK]
    dec_in = rp["w0"] + _dot(jnp.tanh(wd).astype(BF16), rp["wdec"].astype(BF16))
    w_log = -_softplus(-dec_in) - 0.5
    lw = -jnp.exp(w_log)
    iclr = jax.nn.sigmoid(rp["a0"] + _dot(ad.astype(BF16), rp["wiclr"].astype(BF16)))
    gate = _dot(jax.nn.sigmoid(gd).astype(BF16), rp["wgate"].astype(BF16))
    kk = k * rp["kk"]
    ss = _dot_exact_rhs(kk * kk, rp["hsum"])
    kkn = kk / jnp.maximum(jnp.sqrt(ss), 1e-12)
    k2 = k * (1.0 + (iclr - 1.0) * rp["ka"])
    return r, k2, v, lw, -kkn, kkn * iclr, gate


def _rwkv_post(o, r, k2, v, gate, rp):
    inv = 1.0 / RWKV_HEAD
    mu = _dot_exact_rhs(o, rp["hsum"]) * inv
    d = o - mu
    var = _dot_exact_rhs(d * d, rp["hsum"]) * inv
    on = d * lax.rsqrt(var + RWKV_GN_EPS) * rp["lnw"] + rp["lnb"]
    bonus = _dot_exact_rhs(r * k2 * rp["rk"], rp["hsum"]) * v
    return (on + bonus) * gate


_RWKV_PARAM_NAMES = ("mu", "w0", "wdec", "a0", "wiclr", "wgate", "kk", "ka", "rk", "lnw", "lnb", "hsum")


def _rwkv_prompt_kernel(p_ref, shift0_ref, s0_ref, *rest, chunk):
    n_prm = len(_RWKV_PARAM_NAMES)
    rp = {name: ref[...] for name, ref in zip(_RWKV_PARAM_NAMES, rest[:n_prm])}
    o_ref, nshift_ref, nwkv_ref, prev_sc, s_sc, o_sc = rest[n_prm:]
    n_pairs = s_sc.shape[0]
    hd = RWKV_HEAD
    assert chunk == hd, "the pair-packed layout below uses chunk == head size"

    @pl.when(pl.program_id(1) == 0)
    def _():
        prev_sc[...] = shift0_ref[0]
        for p in range(n_pairs):
            s_sc[p] = jnp.concatenate([s0_ref[0, 2 * p], s0_ref[0, 2 * p + 1]], axis=1)

    p = p_ref[...]
    row = lax.broadcasted_iota(jnp.int32, p.shape, 0)
    p_prev = jnp.where(row >= 1, pltpu.roll(p, 1, 0), prev_sc[...])
    last = p[chunk - 1:chunk, :]
    prev_sc[...] = last
    nshift_ref[0] = last
    mixed = p + (p_prev - p) * rp["mu"]
    r, k2, v, lw, a, b, gate = _rwkv_rows(mixed, rp)

    ti = lax.broadcasted_iota(jnp.int32, (chunk, chunk), 0)
    si = lax.broadcasted_iota(jnp.int32, (chunk, chunk), 1)
    cs = _dot_exact_lhs((ti >= si).astype(BF16), lw)
    g_in, g_ex, g_inv = jnp.exp(cs), jnp.exp(cs - lw), jnp.exp(-cs)
    at, bt, kt, rt = a * g_ex, b * g_inv, k2 * g_inv, r * g_in
    g_end = g_in[chunk - 1:chunk, :]
    bc, kc = bt * g_end, kt * g_end
    n_fac = max(1, (chunk - 1).bit_length())
    pw_ = 2 * hd
    lane1 = lax.broadcasted_iota(jnp.int32, (1, pw_), 1)
    lo1 = lane1 < hd
    lo2 = jnp.concatenate([lo1, lo1], axis=1)
    ti2 = lax.broadcasted_iota(jnp.int32, (chunk, pw_), 0)
    si2 = lax.broadcasted_iota(jnp.int32, (chunk, pw_), 1) % hd
    strict2, incl2 = ti2 > si2, ti2 >= si2
    rr = lax.broadcasted_iota(jnp.int32, (pw_, pw_), 0)
    cc = lax.broadcasted_iota(jnp.int32, (pw_, pw_), 1)
    same_head, eye2 = (rr < hd) == (cc < hd), rr == cc

    def bdiag(x, lo):
        return jnp.concatenate([jnp.where(lo, x, 0), jnp.where(lo, 0, x)], axis=0)

    def bdiag2(x2, lo):
        return bdiag(x2[0], lo), bdiag(x2[1], lo)

    pairs = range(n_pairs)
    pls = [slice(p * pw_, (p + 1) * pw_) for p in pairs]
    a_p, r_p, v_p = ([x[:, s] for s in pls] for x in (at, rt, v))
    ms = []
    for p in pairs:
        b2, k2_ = _split2(bt[:, pls[p]]), _split2(kt[:, pls[p]])
        rhs_rows = tuple(jnp.concatenate([bdiag(b2[i], lo1), bdiag(k2_[i], lo1)], axis=0) for i in range(2))
        ms.append(_dot3(_split2(jnp.concatenate([a_p[p], r_p[p]], axis=0)), rhs_rows, _NT))
    labs = [jnp.where(strict2, m[:chunk, :pw_], 0.0) for m in ms]
    lrbs = [jnp.where(incl2, m[chunk:, :pw_], 0.0).astype(BF16) for m in ms]
    lkvs = [_dot3(_split2(jnp.concatenate([jnp.where(strict2, ms[p][:chunk, pw_:], 0.0),
                                           jnp.where(incl2, ms[p][chunk:, pw_:], 0.0)], axis=0)),
                  bdiag2(_split2(v_p[p]), lo1)) for p in pairs]
    ys = [jnp.concatenate([a_p[p], lkvs[p][:chunk]], axis=1) for p in pairs]
    pws = [_split2(lab) for lab in labs]
    for f in range(n_fac):
        ys = [ys[p] + _dot3(pws[p], bdiag2(_split2(ys[p]), lo2)) for p in pairs]
        if f + 1 < n_fac:
            pws = [_split2(_dot3(pw, bdiag2(pw, lo1))) for pw in pws]
    y2s = [_split2(y) for y in ys]
    qos = [jnp.concatenate([r_p[p], lkvs[p][chunk:]], axis=1) + _dg(lrbs[p], bdiag(y2s[p][0], lo2), _NN)
           for p in pairs]
    mns = [_dot3(y2s[p], _split2(bc[:, pls[p]]), _TN) for p in pairs]
    vks = [_dot3(_split2(v_p[p]), _split2(kc[:, pls[p]]), _TN) for p in pairs]
    m_ps = [_split2(jnp.where(same_head, mns[p][:pw_], 0.0) + jnp.where(eye2, g_end[:, pls[p]], 0.0))
            for p in pairs]
    n0s = [jnp.where(lo1, mns[p][pw_:pw_ + hd] + vks[p][:hd], mns[p][pw_ + hd:] + vks[p][hd:])
           for p in pairs]
    s0s = [_split2(s_sc[p]) for p in pairs]
    for p in pairs:
        o_sc[:, pls[p]] = (_dg(qos[p][:, :pw_].astype(BF16), bdiag(s0s[p][0], lo1), _NT)
                           + qos[p][:, pw_:])
    for p in pairs:
        s_sc[p] = _dot3(s0s[p], m_ps[p]) + n0s[p]

    @pl.when(pl.program_id(1) == pl.num_programs(1) - 1)
    def _():
        for p in pairs:
            s_pair = s_sc[p]
            nwkv_ref[0, 2 * p] = s_pair[:, :hd]
            nwkv_ref[0, 2 * p + 1] = s_pair[:, hd:]

    o_ref[...] = _rwkv_post(o_sc[...], r, k2, v, gate, rp)


def _rwkv_param_list(rp):
    return [rp[name] for name in _RWKV_PARAM_NAMES]


def rwkv_prompt(p_rwkv, n_batch, seq, shift0, wkv0, rp):
    cols = shift0.shape[-1]
    n_heads, hd = wkv0.shape[1], wkv0.shape[2]
    width = n_heads * hd
    chunk = min(RWKV_CHUNK, seq)
    n_chunks = seq // chunk
    prm = _rwkv_param_list(rp)
    prm_specs = [pl.BlockSpec(x.shape, lambda b, c: (0, 0)) for x in prm]
    return pl.pallas_call(
        functools.partial(_rwkv_prompt_kernel, chunk=chunk),
        grid=(n_batch, n_chunks),
        in_specs=[pl.BlockSpec((chunk, cols), lambda b, c: (b * n_chunks + c, 0)),
                  pl.BlockSpec((1, 1, cols), lambda b, c: (b, 0, 0)),
                  pl.BlockSpec((1, n_heads, hd, hd), lambda b, c: (b, 0, 0, 0))] + prm_specs,
        out_specs=[pl.BlockSpec((chunk, width), lambda b, c: (b * n_chunks + c, 0)),
                   pl.BlockSpec((1, 1, cols), lambda b, c: (b, 0, 0)),
                   pl.BlockSpec((1, n_heads, hd, hd), lambda b, c: (b, 0, 0, 0))],
        out_shape=[jax.ShapeDtypeStruct((n_batch * seq, width), F32),
                   jax.ShapeDtypeStruct((n_batch, 1, cols), F32),
                   jax.ShapeDtypeStruct((n_batch, n_heads, hd, hd), F32)],
        scratch_shapes=[pltpu.VMEM((1, cols), F32), pltpu.VMEM((n_heads // 2, hd, 2 * hd), F32),
                        pltpu.VMEM((chunk, width), F32)],
        compiler_params=_cparams(2),
        name="rwkv_prompt",
    )(p_rwkv, shift0.reshape(n_batch, 1, cols), wkv0, *prm)


def _rwkv_step_pre_kernel(p_ref, prev_ref, *rest):
    n_prm = len(_RWKV_PARAM_NAMES)
    rp = {name: ref[...] for name, ref in zip(_RWKV_PARAM_NAMES, rest[:n_prm])}
    r_ref, k_ref, v_ref, w_ref, a_ref, b_ref, gate_ref = rest[n_prm:]
    p = p_ref[...]
    mixed = p + (prev_ref[...] - p) * rp["mu"]
    r, k2, v, lw, a, b, gate = _rwkv_rows(mixed, rp)
    r_ref[...] = r
    k_ref[...] = k2
    v_ref[...] = v
    w_ref[...] = jnp.exp(lw)
    a_ref[...] = a
    b_ref[...] = b
    gate_ref[...] = gate


def _rwkv_step_kernel(s_ref, w_ref, a_ref, b_ref, k_ref, r_ref, vt_ref, ns_ref, ot_ref):
    bt, n_heads = s_ref.shape[0], s_ref.shape[1]
    lane = lax.broadcasted_iota(jnp.int32, ot_ref.shape[1:], 1)
    for i in range(bt):
        vt = vt_ref[i]
        ss = [s_ref[i, h] for h in range(n_heads)]
        sas = [jnp.sum(ss[h] * a_ref[i, h], axis=-1, keepdims=True) for h in range(n_heads)]
        s_news = [ss[h] * w_ref[i, h] + sas[h] * b_ref[i, h] + vt[:, h:h + 1] * k_ref[i, h]
                  for h in range(n_heads)]
        for h in range(n_heads):
            ns_ref[i, h] = s_news[h]
        os_ = [jnp.sum(s_news[h] * r_ref[i, h], axis=-1, keepdims=True) for h in range(n_heads)]
        ot = jnp.zeros(ot_ref.shape[1:], F32)
        for h in range(n_heads):
            ot = jnp.where(lane == h, os_[h], ot)
        ot_ref[i] = ot


def _rwkv_step_post_kernel(o_ref, r_ref, k_ref, v_ref, gate_ref, *rest):
    n_prm = len(_RWKV_PARAM_NAMES)
    rp = {name: ref[...] for name, ref in zip(_RWKV_PARAM_NAMES, rest[:n_prm])}
    out_ref = rest[n_prm]
    out_ref[...] = _rwkv_post(o_ref[...], r_ref[...], k_ref[...], v_ref[...], gate_ref[...], rp)


def rwkv_step(p_rwkv, shift_prev, wkv0, rp):
    n = p_rwkv.shape[0]
    n_heads, hd = wkv0.shape[1], wkv0.shape[2]
    width = n_heads * hd
    prm = _rwkv_param_list(rp)
    row = jax.ShapeDtypeStruct((n, width), F32)
    r, k2, v, w, a, b, gate = _rows_call(_rwkv_step_pre_kernel, [p_rwkv, shift_prev, *prm], [row] * 7,
                                         "rwkv_step_pre")
    hrow = lambda z: z.reshape(n, n_heads, 1, hd)
    vt = jnp.transpose(v.reshape(n, n_heads, hd), (0, 2, 1))
    bt = SUBLANES
    vec_spec = pl.BlockSpec((bt, n_heads, 1, hd), lambda i: (i, 0, 0, 0))
    st_spec = pl.BlockSpec((bt, n_heads, hd, hd), lambda i: (i, 0, 0, 0))
    t_spec = pl.BlockSpec((bt, hd, n_heads), lambda i: (i, 0, 0))
    new_wkv, ot = pl.pallas_call(
        _rwkv_step_kernel,
        grid=(n // bt,),
        in_specs=[st_spec] + [vec_spec] * 5 + [t_spec],
        out_specs=[st_spec, t_spec],
        out_shape=[jax.ShapeDtypeStruct(wkv0.shape, F32), jax.ShapeDtypeStruct((n, hd, n_heads), F32)],
        compiler_params=_cparams(1),
        name="rwkv_step",
    )(wkv0, hrow(w), hrow(a), hrow(b), hrow(k2), hrow(r), vt)
    o = jnp.transpose(ot, (0, 2, 1)).reshape(n, width)
    (out_b,) = _rows_call(_rwkv_step_post_kernel, [o, r, k2, v, gate, *prm], [row], "rwkv_step_post")
    return out_b, new_wkv


def _mamba_post(y, xs, z, dexp, ng):
    y = (y + dexp * xs) * _silu(z)
    gw = y.shape[1] // SSM_GROUPS
    parts = []
    for g in range(SSM_GROUPS):
        yg = y[:, g * gw:(g + 1) * gw]
        parts.append(yg * lax.rsqrt(jnp.mean(yg * yg, axis=-1, keepdims=True) + SSM_NORM_EPS))
    return jnp.concatenate(parts, axis=1) * ng


def _ssd_prompt_kernel(z_ref, xlo_ref, xhi_ref, dt_ref, conv0_ref, s0_ref, cw_ref, cb_ref, dtb_ref,
                       alog_ref, dexp_ref, ng_ref, hexp_ref, qexp_ref, y_ref, nconv_ref, nssm_ref,
                       ext_sc, s_sc, y_sc, yo_sc, *, q):
    inner = xlo_ref.shape[1]
    n_heads = s_sc.shape[0]
    hpg = n_heads // SSM_GROUPS

    @pl.when(pl.program_id(1) == 0)
    def _():
        ext_sc[0:SUBLANES, :] = jnp.zeros((SUBLANES, ext_sc.shape[1]), F32)
        ext_sc[SUBLANES - 3:SUBLANES, :] = conv0_ref[0]
        s_sc[...] = s0_ref[0]

    ext_sc[SUBLANES:SUBLANES + q, 0:inner] = xlo_ref[...].astype(F32)
    ext_sc[SUBLANES:SUBLANES + q, inner:] = xhi_ref[...].astype(F32)
    xbc = _silu(_conv4(ext_sc[SUBLANES:SUBLANES + q, :], ext_sc[SUBLANES - 1:SUBLANES - 1 + q, :],
                       ext_sc[SUBLANES - 2:SUBLANES - 2 + q, :], ext_sc[SUBLANES - 3:SUBLANES - 3 + q, :],
                       cw_ref[...], cb_ref[...]))
    tail = ext_sc[q + SUBLANES - 3:q + SUBLANES, :]
    ext_sc[SUBLANES - 3:SUBLANES, :] = tail
    nconv_ref[0] = tail

    xs = xbc[:, 0:inner]
    gn = SSM_GROUPS * SSM_STATE
    bm = xbc[:, inner:inner + gn].astype(BF16)
    cm = xbc[:, inner + gn:].astype(BF16)
    dt = _softplus(dt_ref[...] + dtb_ref[...])
    dta = dt * (-jnp.exp(alog_ref[...]))
    ti = lax.broadcasted_iota(jnp.int32, (q, q), 0)
    si = lax.broadcasted_iota(jnp.int32, (q, q), 1)
    causal = ti >= si
    da = _dot_exact_lhs(causal.astype(BF16), dta)
    upper = (ti <= si).astype(BF16)
    da_t = sum(_dg(part, upper, _TN) for part in _split3(dta))
    da_end = da[q - 1:q, :]
    end_decay = jnp.exp(da_end)
    hexp, qexp = hexp_ref[...], qexp_ref[...]
    xdt = xs * _dot_exact_rhs(dt, hexp)
    xdt_b = xdt.astype(BF16)
    xdt_end = (xdt * _dot_exact_rhs(jnp.exp(da_end - da), hexp)).astype(BF16)
    from_start = _dot_exact_rhs(jnp.exp(da), hexp)
    da_col = _dot_exact_rhs(da, qexp)

    heads = range(n_heads)
    bgs = [bm[:, g * SSM_STATE:(g + 1) * SSM_STATE] for g in range(SSM_GROUPS)]
    cgs = [cm[:, g * SSM_STATE:(g + 1) * SSM_STATE] for g in range(SSM_GROUPS)]
    scores = [_dot_nt(cgs[g], bgs[g]) for g in range(SSM_GROUPS)]
    hss = [slice(h * SSM_HEAD, (h + 1) * SSM_HEAD) for h in heads]
    s_hs = [s_sc[h] for h in heads]
    for h in heads:
        yo_sc[:, hss[h]] = _dot_nt(cgs[h // hpg], s_hs[h].astype(BF16))
    wts = [(scores[h // hpg]
            * jnp.exp(jnp.where(causal, da_col[:, h * q:(h + 1) * q] - da_t[h:h + 1, :], -jnp.inf))
            ).astype(BF16) for h in heads]
    for h in heads:
        y_sc[:, hss[h]] = _dot(wts[h], xdt_b[:, hss[h]])
    upd = [_dot_tn(xdt_end[:, hss[h]], bgs[h // hpg]) for h in heads]
    for h in heads:
        s_sc[h] = s_hs[h] * end_decay[:, h:h + 1] + upd[h]

    @pl.when(pl.program_id(1) == pl.num_programs(1) - 1)
    def _():
        nssm_ref[0] = s_sc[...]

    y = y_sc[...] + yo_sc[...] * from_start
    y_ref[...] = _mamba_post(y, xs, z_ref[...].astype(F32), dexp_ref[...], ng_ref[...])


def ssd_prompt(proj1, dt_raw, n_batch, seq, conv0, ssm0, mp):
    n_heads, hd, n_state = ssm0.shape[1:]
    inner = n_heads * hd
    conv_ch = conv0.shape[-1]
    q = SSM_CHUNK if seq % SSM_CHUNK == 0 else seq
    n_chunks = seq // q
    full = lambda shape: pl.BlockSpec(shape, lambda b, c: (0,) * len(shape))
    return pl.pallas_call(
        functools.partial(_ssd_prompt_kernel, q=q),
        grid=(n_batch, n_chunks),
        in_specs=[pl.BlockSpec((q, inner), lambda b, c: (b * n_chunks + c, 0)),
                  pl.BlockSpec((q, inner), lambda b, c: (b * n_chunks + c, 1)),
                  pl.BlockSpec((q, inner), lambda b, c: (b * n_chunks + c, 2)),
                  pl.BlockSpec((q, n_heads), lambda b, c: (b * n_chunks + c, 0)),
                  pl.BlockSpec((1, 3, conv_ch), lambda b, c: (b, 0, 0)),
                  pl.BlockSpec((1, n_heads, hd, n_state), lambda b, c: (b, 0, 0, 0)),
                  full((CONV_W, conv_ch)), full((1, conv_ch)), full((1, n_heads)), full((1, n_heads)),
                  full((1, inner)), full((1, inner)), full((n_heads, inner)), full((n_heads, n_heads * q))],
        out_specs=[pl.BlockSpec((q, inner), lambda b, c: (b * n_chunks + c, 0)),
                   pl.BlockSpec((1, 3, conv_ch), lambda b, c: (b, 0, 0)),
                   pl.BlockSpec((1, n_heads, hd, n_state), lambda b, c: (b, 0, 0, 0))],
        out_shape=[jax.ShapeDtypeStruct((n_batch * seq, inner), F32),
                   jax.ShapeDtypeStruct((n_batch, 3, conv_ch), F32),
                   jax.ShapeDtypeStruct(ssm0.shape, F32)],
        scratch_shapes=[pltpu.VMEM((q + SUBLANES, conv_ch), F32), pltpu.VMEM((n_heads, hd, n_state), F32),
                        pltpu.VMEM((q, inner), F32), pltpu.VMEM((q, inner), F32)],
        compiler_params=_cparams(2),
        name="ssd_prompt",
    )(proj1, proj1, proj1, dt_raw, conv0, ssm0, mp["cw"], mp["cb"], mp["dtb"], mp["alog"], mp["dexp"],
      mp["ng"], jnp.repeat(jnp.eye(n_heads, dtype=BF16), hd, axis=1),
      jnp.repeat(jnp.eye(n_heads, dtype=BF16), q, axis=1))


def _ssd_step_pre_kernel(x_ref, dt_ref, buf_ref, cw_ref, cb_ref, dtb_ref, alog_ref,
                         xs_ref, b_ref, c_ref, dt_out_ref, dec_ref, nbuf_ref):
    ch = x_ref.shape[1]
    inner = xs_ref.shape[1]
    gn = b_ref.shape[1]
    u = x_ref[...].astype(F32)
    b0, b1, b2 = (buf_ref[:, k * ch:(k + 1) * ch] for k in range(3))
    xbc = _silu(_conv4(u, b2, b1, b0, cw_ref[...], cb_ref[...]))
    xs_ref[...] = xbc[:, 0:inner]
    b_ref[...] = xbc[:, inner:inner + gn]
    c_ref[...] = xbc[:, inner + gn:]
    dt = _softplus(dt_ref[...] + dtb_ref[...])
    dt_out_ref[...] = dt
    dec_ref[...] = jnp.exp(dt * (-jnp.exp(alog_ref[...])))
    nbuf_ref[:, 0:ch] = b1
    nbuf_ref[:, ch:2 * ch] = b2
    nbuf_ref[:, 2 * ch:3 * ch] = u


def _ssd_step_kernel(s_ref, xt_ref, b_ref, c_ref, dt_ref, dec_ref, ns_ref, yt_ref):
    bt, n_heads = s_ref.shape[0], s_ref.shape[1]
    hpg = n_heads // SSM_GROUPS
    lane = lax.broadcasted_iota(jnp.int32, yt_ref.shape[1:], 1)
    for i in range(bt):
        xt = xt_ref[i]
        dt = dt_ref[i]
        dec = dec_ref[i]
        xdt = xt * dt
        s_news = [s_ref[i, h] * dec[:, h:h + 1] + xdt[:, h:h + 1] * b_ref[i, h // hpg]
                  for h in range(n_heads)]
        for h in range(n_heads):
            ns_ref[i, h] = s_news[h]
        ys = [jnp.sum(s_news[h] * c_ref[i, h // hpg], axis=-1, keepdims=True) for h in range(n_heads)]
        yt = jnp.zeros(yt_ref.shape[1:], F32)
        for h in range(n_heads):
            yt = jnp.where(lane == h, ys[h], yt)
        yt_ref[i] = yt


def _ssd_step_post_kernel(y_ref, xs_ref, z_ref, dexp_ref, ng_ref, o_ref):
    o_ref[...] = _mamba_post(y_ref[...], xs_ref[...], z_ref[...].astype(F32), dexp_ref[...], ng_ref[...])


def ssd_step(z, xbc_raw, dt_raw, conv_buf, ssm0, mp):
    n, conv_ch = xbc_raw.shape
    n_heads, hd, n_state = ssm0.shape[1:]
    inner = n_heads * hd
    gn = SSM_GROUPS * n_state
    xs, bm, cm, dt, dec, nbuf = _rows_call(
        _ssd_step_pre_kernel,
        [xbc_raw, dt_raw, conv_buf.reshape(n, 3 * conv_ch), mp["cw"], mp["cb"], mp["dtb"], mp["alog"]],
        [jax.ShapeDtypeStruct((n, inner), F32), jax.ShapeDtypeStruct((n, gn), F32),
         jax.ShapeDtypeStruct((n, gn), F32), jax.ShapeDtypeStruct((n, n_heads), F32),
         jax.ShapeDtypeStruct((n, n_heads), F32), jax.ShapeDtypeStruct((n, 3 * conv_ch), F32)],
        "ssd_step_pre")
    xt = jnp.transpose(xs.reshape(n, n_heads, hd), (0, 2, 1))
    bt = 4
    st_spec = pl.BlockSpec((bt, n_heads, hd, n_state), lambda i: (i, 0, 0, 0))
    t_spec = pl.BlockSpec((bt, hd, n_heads), lambda i: (i, 0, 0))
    g_spec = pl.BlockSpec((bt, SSM_GROUPS, 1, n_state), lambda i: (i, 0, 0, 0))
    h_spec = pl.BlockSpec((bt, 1, n_heads), lambda i: (i, 0, 0))
    new_ssm, yt = pl.pallas_call(
        _ssd_step_kernel,
        grid=(n // bt,),
        in_specs=[st_spec, t_spec, g_spec, g_spec, h_spec, h_spec],
        out_specs=[st_spec, t_spec],
        out_shape=[jax.ShapeDtypeStruct(ssm0.shape, F32), jax.ShapeDtypeStruct((n, hd, n_heads), F32)],
        compiler_params=_cparams(1),
        name="ssd_step",
    )(ssm0, xt, bm.reshape(n, SSM_GROUPS, 1, n_state), cm.reshape(n, SSM_GROUPS, 1, n_state),
      dt.reshape(n, 1, n_heads), dec.reshape(n, 1, n_heads))
    y = jnp.transpose(yt, (0, 2, 1)).reshape(n, inner)
    (y,) = _rows_call(_ssd_step_post_kernel, [y, xs, z, mp["dexp"], mp["ng"]],
                      [jax.ShapeDtypeStruct((n, inner), F32)], "ssd_step_post")
    return y, nbuf, new_ssm


def _router_kernel(x_ref, g_ref, rwt_ref, rb_ref, xn_ref, idx_ref, gate_ref, rank_ref, cnt_ref, cnt_sc):
    tm = x_ref.shape[0]

    @pl.when(pl.program_id(0) == 0)
    def _():
        cnt_sc[...] = jnp.zeros_like(cnt_sc)

    xn = _rms(x_ref[...], g_ref[...], NORM_EPS)
    xn_ref[...] = xn
    logits = _dot_nt(rwt_ref[...], xn, HIGHEST) + rb_ref[...]
    e_iota = lax.broadcasted_iota(jnp.int32, logits.shape, 0)
    m1 = jnp.max(logits, axis=0, keepdims=True)
    i1 = jnp.min(jnp.where(logits == m1, e_iota, N_EXPERTS), axis=0, keepdims=True)
    rest = jnp.where(e_iota == i1, -jnp.inf, logits)
    m2 = jnp.max(rest, axis=0, keepdims=True)
    i2 = jnp.min(jnp.where(rest == m2, e_iota, N_EXPERTS), axis=0, keepdims=True)
    e2 = jnp.exp(m2 - m1)
    denom = 1.0 + e2
    idx_ref[...] = jnp.concatenate([i1, i2], axis=0)
    gate_ref[...] = jnp.concatenate([1.0 / denom, e2 / denom], axis=0)
    oh1 = (e_iota == i1).astype(F32)
    oh2 = (e_iota == i2).astype(F32)
    oh = oh1 + oh2
    ti = lax.broadcasted_iota(jnp.int32, (tm, tm), 0)
    si = lax.broadcasted_iota(jnp.int32, (tm, tm), 1)
    before = _dot(oh.astype(BF16), (ti < si).astype(BF16)) + cnt_sc[:, 0:1]
    rank_ref[...] = jnp.concatenate(
        [jnp.sum(oh1 * before, axis=0, keepdims=True), jnp.sum(oh2 * before, axis=0, keepdims=True)],
        axis=0).astype(jnp.int32)
    cnt_sc[...] = cnt_sc[...] + jnp.sum(oh, axis=1, keepdims=True)
    cnt_ref[...] = cnt_sc[...].astype(jnp.int32)


def moe_router(x, g, router_w, router_b, tm):
    t_rows, d = x.shape
    return pl.pallas_call(
        _router_kernel,
        grid=(t_rows // tm,),
        in_specs=[pl.BlockSpec((tm, d), lambda i: (i, 0)),
                  pl.BlockSpec((1, d), lambda i: (0, 0)),
                  pl.BlockSpec((N_EXPERTS, d), lambda i: (0, 0)),
                  pl.BlockSpec((N_EXPERTS, 1), lambda i: (0, 0))],
        out_specs=[pl.BlockSpec((tm, d), lambda i: (i, 0)),
                   pl.BlockSpec((2, tm), lambda i: (0, i)),
                   pl.BlockSpec((2, tm), lambda i: (0, i)),
                   pl.BlockSpec((2, tm), lambda i: (0, i)),
                   pl.BlockSpec((N_EXPERTS, LANES), lambda i: (0, 0))],
        out_shape=[jax.ShapeDtypeStruct((t_rows, d), F32),
                   jax.ShapeDtypeStruct((2, t_rows), jnp.int32),
                   jax.ShapeDtypeStruct((2, t_rows), F32),
                   jax.ShapeDtypeStruct((2, t_rows), jnp.int32),
                   jax.ShapeDtypeStruct((N_EXPERTS, LANES), jnp.int32)],
        scratch_shapes=[pltpu.VMEM((N_EXPERTS, LANES), F32)],
        compiler_params=_cparams(1),
        name="moe_router",
    )(x, g.reshape(1, d), router_w.T, router_b.reshape(N_EXPERTS, 1))


def _row_copy(src_ref, src_row, dst_ref, dst_row, sem):
    return pltpu.make_async_copy(src_ref.at[pl.ds(src_row, 1), :], dst_ref.at[pl.ds(dst_row, 1), :], sem)


def _dispatch_kernel(dest_ref, x_ref, slots_in_ref, slots_ref, sem, *, t_rows):
    del slots_in_ref
    tm = x_ref.shape[0]
    base = pl.program_id(0) * tm

    def copies(r):
        return [_row_copy(x_ref, r, slots_ref, dest_ref[k * t_rows + base + r], sem) for k in range(2)]

    def start(r, carry):
        for cp in copies(r):
            cp.start()
        return carry

    def wait(r, carry):
        for cp in copies(r):
            cp.wait()
        return carry

    lax.fori_loop(0, tm, start, 0, unroll=DMA_UNROLL)
    lax.fori_loop(0, tm, wait, 0, unroll=DMA_UNROLL)


def moe_dispatch(xn, dest_flat, n_slots, tm):
    t_rows, d = xn.shape
    return pl.pallas_call(
        functools.partial(_dispatch_kernel, t_rows=t_rows),
        grid_spec=pltpu.PrefetchScalarGridSpec(
            num_scalar_prefetch=1,
            grid=(t_rows // tm,),
            in_specs=[pl.BlockSpec((tm, d), lambda i, dest: (i, 0)),
                      pl.BlockSpec(memory_space=pl.ANY)],
            out_specs=pl.BlockSpec(memory_space=pl.ANY),
            scratch_shapes=[pltpu.SemaphoreType.DMA(())]),
        out_shape=jax.ShapeDtypeStruct((n_slots, d), F32),
        input_output_aliases={2: 0},
        compiler_params=_cparams(1),
        name="moe_dispatch",
    )(dest_flat, xn, jnp.zeros((n_slots, d), F32))


def _moe_kernel(te_ref, tv_ref, x_ref, wg_ref, wu_ref, wd_ref, o_ref, xb_sc, acc_sc):
    i, j = pl.program_id(0), pl.program_id(1)
    valid = tv_ref[i] == 1

    @pl.when(j == 0)
    def _():
        xb_sc[...] = x_ref[...].astype(BF16)
        acc_sc[...] = jnp.zeros_like(acc_sc)

    @pl.when(valid)
    def _():
        xb = xb_sc[...]
        hg = _dot(xb, wg_ref[0].astype(BF16))
        hu = _dot(xb, wu_ref[0].astype(BF16))
        h = (_silu(hg) * hu).astype(BF16)
        acc_sc[...] += _dot(h, wd_ref[0].astype(BF16))

    @pl.when(j == pl.num_programs(1) - 1)
    def _():
        o_ref[...] = acc_sc[...]


def moe_experts(slots, tile_expert, tile_valid, wg, wu, wd, tm, tf):
    n_slots, d = slots.shape
    d_ff = wg.shape[2]
    n_f = d_ff // tf

    def f_idx(i, j, te, tv):
        return jnp.where(tv[i] == 1, j, n_f - 1)

    return pl.pallas_call(
        _moe_kernel,
        grid_spec=pltpu.PrefetchScalarGridSpec(
            num_scalar_prefetch=2,
            grid=(n_slots // tm, n_f),
            in_specs=[pl.BlockSpec((tm, d), lambda i, j, te, tv: (i, 0)),
                      pl.BlockSpec((1, d, tf), lambda i, j, te, tv: (te[i], 0, f_idx(i, j, te, tv))),
                      pl.BlockSpec((1, d, tf), lambda i, j, te, tv: (te[i], 0, f_idx(i, j, te, tv))),
                      pl.BlockSpec((1, tf, d), lambda i, j, te, tv: (te[i], f_idx(i, j, te, tv), 0))],
            out_specs=pl.BlockSpec((tm, d), lambda i, j, te, tv: (i, 0)),
            scratch_shapes=[pltpu.VMEM((tm, d), BF16), pltpu.VMEM((tm, d), F32)]),
        out_shape=jax.ShapeDtypeStruct((n_slots, d), F32),
        compiler_params=_cparams(2),
        name="moe_experts",
    )(tile_expert, tile_valid, slots, wg, wu, wd)


def _combine_kernel(dest_ref, x_ref, gates_ref, g_ref, y_hbm_ref, o_ref, buf_sc, sem, *, t_rows, row0):
    tm = x_ref.shape[0]
    base = row0 + pl.program_id(0) * tm

    def copies(r):
        return [_row_copy(y_hbm_ref, dest_ref[k * t_rows + base + r], buf_sc.at[k], r, sem)
                for k in range(2)]

    def start(r, carry):
        for cp in copies(r):
            cp.start()
        return carry

    def wait(r, carry):
        for cp in copies(r):
            cp.wait()
        return carry

    lax.fori_loop(0, tm, start, 0, unroll=DMA_UNROLL)
    lax.fori_loop(0, tm, wait, 0, unroll=DMA_UNROLL)
    gates = gates_ref[...]
    out = x_ref[...] + (gates[:, 0:1] * buf_sc[0] + gates[:, 1:2] * buf_sc[1])
    o_ref[...] = _rms(out, g_ref[...], NORM_EPS)


def moe_combine(x, gates_col, dest_flat, y_slots, g_final, tm, row0, n_rows):
    t_rows, d = x.shape
    assert row0 % tm == 0 and n_rows % tm == 0
    blk0 = row0 // tm
    return pl.pallas_call(
        functools.partial(_combine_kernel, t_rows=t_rows, row0=row0),
        grid_spec=pltpu.PrefetchScalarGridSpec(
            num_scalar_prefetch=1,
            grid=(n_rows // tm,),
            in_specs=[pl.BlockSpec((tm, d), lambda i, dest: (i + blk0, 0)),
                      pl.BlockSpec((tm, 2), lambda i, dest: (i + blk0, 0)),
                      pl.BlockSpec((1, d), lambda i, dest: (0, 0)),
                      pl.BlockSpec(memory_space=pl.ANY)],
            out_specs=pl.BlockSpec((tm, d), lambda i, dest: (i, 0)),
            scratch_shapes=[pltpu.VMEM((2, tm, d), F32), pltpu.SemaphoreType.DMA(())]),
        out_shape=jax.ShapeDtypeStruct((n_rows, d), F32),
        compiler_params=_cparams(1),
        name="moe_combine",
    )(dest_flat, x, gates_col, g_final.reshape(1, d), y_slots)


def moe_final(x, n_prompt, g_ffn, g_final, router_w, router_b, wg, wu, wd):
    t_rows, d = x.shape
    rt = _row_tile(t_rows, ROUTER_TILE) if t_rows % LANES else ROUTER_TILE
    if t_rows % rt or rt % LANES:
        rt = t_rows
    xn, idx, gates, rank, counts = moe_router(x, g_ffn, router_w, router_b, rt)
    counts = counts[:, 0]
    tm = MOE_TILE
    n_tiles = -(-2 * t_rows // tm) + N_EXPERTS
    padded = (counts + tm - 1) // tm * tm
    pend = jnp.cumsum(padded)
    pstart = pend - padded
    dest = jnp.sum(jnp.where(idx[:, :, None] == jnp.arange(N_EXPERTS)[None, None, :], pstart[None, None, :], 0),
                   axis=-1) + rank
    dest_flat = dest.reshape(-1).astype(jnp.int32)
    tile_start = jnp.arange(n_tiles, dtype=jnp.int32) * tm
    tile_valid = (tile_start < pend[-1]).astype(jnp.int32)
    last_start = jnp.maximum(pend[-1] - tm, 0)
    probe = jnp.minimum(tile_start, last_start)
    tile_expert = jnp.minimum(jnp.sum((pend[None, :] <= probe[:, None]).astype(jnp.int32), axis=1),
                              N_EXPERTS - 1)
    slots = moe_dispatch(xn, dest_flat, n_tiles * tm, _row_tile(t_rows, GATHER_TILE))
    y_slots = moe_experts(slots, tile_expert, tile_valid, wg, wu, wd, tm, MOE_FF_TILE)
    gates_col = gates.T
    n_sample = t_rows - n_prompt
    y_prompt = moe_combine(x, gates_col, dest_flat, y_slots, g_final, _row_tile(n_prompt, COMBINE_TILE),
                           0, n_prompt)
    y_sample = moe_combine(x, gates_col, dest_flat, y_slots, g_final, n_sample, n_prompt, n_sample)
    return y_prompt, y_sample


def _block_diag(w):
    h, i, j = w.shape
    eye = jnp.eye(h, dtype=w.dtype)
    return jnp.einsum("hij,hg->higj", w, eye).reshape(h * i, h * j)


def kernel(x_prompt, x_sample, state_lru_conv, state_lru_h, state_rwkv_shift, state_rwkv_wkv, state_ssm_conv, state_ssm, norm_mix, norm_ffn, norm_final, w_in0, lru_conv_w, lru_conv_b, lru_wa, lru_ba, lru_wx, lru_bx, lru_lambda, rwkv_mu, rwkv_w0, rwkv_w_decay_up, rwkv_a0, rwkv_w_iclr_up, rwkv_w_gate_up, rwkv_k_k, rwkv_k_a, rwkv_r_k, rwkv_ln_w, rwkv_ln_b, w_out0, ffn_wg, ffn_wu, ffn_wd, w_in1, ssm_conv_w, ssm_conv_b, ssm_dt_bias, ssm_a_log, ssm_d, ssm_norm_g, w_out1, router_w, router_b, moe_wg, moe_wu, moe_wd):
    nb, seq, d = x_prompt.shape
    ns = x_sample.shape[0]
    tp = nb * seq
    t_all = tp + ns
    lru_w = lru_conv_w.shape[-1]
    rw_w = rwkv_w0.shape[-1]
    shift_cols = rwkv_mu.shape[-1]
    n_rheads = rw_w // RWKV_HEAD
    inner = ssm_norm_g.shape[-1]
    n_sheads = ssm_a_log.shape[-1]
    conv_ch = ssm_conv_w.shape[-1]

    x = jnp.concatenate([x_prompt.reshape(tp, d), x_sample.reshape(ns, d)], axis=0)
    tm = _row_tile(t_all, 1032)

    row = lambda v: v.reshape(1, -1)
    lp = dict(cw=lru_conv_w[0], cb=row(lru_conv_b[0]), wa=_block_diag(lru_wa[0]).astype(BF16),
              ba=row(lru_ba[0]), wx=_block_diag(lru_wx[0]).astype(BF16), bx=row(lru_bx[0]),
              lam=row(lru_lambda[0]))
    hsum = _block_diag(jnp.ones((n_rheads, RWKV_HEAD, RWKV_HEAD), BF16))
    rp = dict(mu=row(rwkv_mu[0]), w0=row(rwkv_w0[0]), wdec=rwkv_w_decay_up[0], a0=row(rwkv_a0[0]),
              wiclr=rwkv_w_iclr_up[0], wgate=rwkv_w_gate_up[0], kk=row(rwkv_k_k[0]), ka=row(rwkv_k_a[0]),
              rk=row(rwkv_r_k[0]), lnw=row(rwkv_ln_w[0]), lnb=row(rwkv_ln_b[0]), hsum=hsum)
    mp = dict(cw=ssm_conv_w[0], cb=row(ssm_conv_b[0]), dtb=row(ssm_dt_bias[0]), alog=row(ssm_a_log[0]),
              dexp=row(jnp.repeat(ssm_d[0], SSM_HEAD)), ng=row(ssm_norm_g[0]))

    tn0 = 256
    proj_lru = norm_matmul(x, norm_mix[0], w_in0[0], 0, 2 * lru_w // tn0, tn0, tm, "in0_lru")
    proj_rwkv = norm_matmul(x, norm_mix[0], w_in0[0], 2 * lru_w // tn0, shift_cols // tn0, tn0, tm, "in0_rwkv")

    zeros = lambda *shape: jnp.zeros(shape, F32)
    out_a_p, p_lru_conv, p_lru_h = lru_prompt(proj_lru, nb, seq, zeros(nb, 3, lru_w), zeros(nb, lru_w), lp)
    out_b_p, p_shift, p_wkv = rwkv_prompt(proj_rwkv, nb, seq, zeros(nb, shift_cols),
                                          zeros(nb, n_rheads, RWKV_HEAD, RWKV_HEAD), rp)
    s_lru = proj_lru[tp:]
    out_a_s, s_lru_conv, s_lru_h = lru_step(s_lru[:, :lru_w], s_lru[:, lru_w:], state_lru_conv[0],
                                            state_lru_h[0], lp)
    s_p_rwkv = proj_rwkv[tp:]
    out_b_s, s_wkv = rwkv_step(s_p_rwkv, state_rwkv_shift[0], state_rwkv_wkv[0], rp)
    out_a = jnp.concatenate([out_a_p, out_a_s], axis=0)
    out_b = jnp.concatenate([out_b_p, out_b_s], axis=0)
    x = matmul_residual([out_a, out_b], w_out0[0], x, tm, 512, "out0")
    x = ffn_residual(x, norm_ffn[0], ffn_wg[0], ffn_wu[0], ffn_wd[0], tm, FFN_FF_TILE)

    tn1 = 512
    proj1 = norm_matmul(x, norm_mix[1], w_in1[0], 0, (inner + conv_ch) // tn1, tn1,
                        _row_tile(t_all, 1376, 2 * SUBLANES), "in1_main", BF16)
    dt_raw = norm_matmul(x, norm_mix[1], w_in1[0][:, inner + conv_ch:], 0, 1, n_sheads, tm, "in1_dt")
    y_mix_p, p_ssm_conv, p_ssm = ssd_prompt(proj1, dt_raw, nb, seq, zeros(nb, 3, conv_ch),
                                            zeros(nb, n_sheads, SSM_HEAD, SSM_STATE), mp)
    s_proj1 = proj1[tp:]
    y_mix_s, s_ssm_conv, s_ssm = ssd_step(s_proj1[:, :inner], s_proj1[:, inner:], dt_raw[tp:],
                                          state_ssm_conv[0], state_ssm[0], mp)
    x = matmul_residual([jnp.concatenate([y_mix_p, y_mix_s], axis=0)], w_out1[0], x, tm, 512, "out1")
    y_p, y_s = moe_final(x, tp, norm_ffn[1], norm_final, router_w[0], router_b[0], moe_wg[0], moe_wu[0],
                         moe_wd[0])

    return (y_p.reshape(nb, seq, d), y_s.reshape(ns, 1, d),
            p_lru_conv[None], p_lru_h.reshape(1, nb, lru_w), p_shift.reshape(1, nb, shift_cols), p_wkv[None],
            p_ssm_conv[None], p_ssm[None],
            s_lru_conv.reshape(1, ns, 3, lru_w), s_lru_h[None], s_p_rwkv[None], s_wkv[None],
            s_ssm_conv.reshape(1, ns, 3, conv_ch), s_ssm[None])
```

```python
import functools

import jax
import jax.numpy as jnp
from jax import lax
from jax.experimental import pallas as pl
from jax.experimental.pallas import tpu as pltpu

F32 = jnp.float32
BF16 = jnp.bfloat16
HIGHEST = lax.Precision.HIGHEST

NORM_EPS = 1e-6
CONV_W = 4
LRU_HEADS = 8
LRU_C = 8.0
RWKV_HEAD = 64
DECAY_RANK = 64
ICLR_RANK = 64
GATE_RANK = 128
RWKV_GN_EPS = 64e-5
SSM_HEAD = 64
SSM_GROUPS = 8
SSM_STATE = 128
SSM_CHUNK = 128
SSM_NORM_EPS = 1e-5
N_EXPERTS = 8

V7X_VMEM_BYTES = 64 * 1024 * 1024
VMEM_LIMIT = V7X_VMEM_BYTES - 8 * 1024 * 1024
SUBLANES = 8
LANES = 128

RWKV_CHUNK = 64
LRU_CHUNK = 256
MOE_TILE = 1024
MOE_FF_TILE = 512
FFN_FF_TILE = 512
TOKEN_TILE = 1024
ROUTER_TILE = LANES
GATHER_TILE = 384
COMBINE_TILE = 512
DMA_UNROLL = 8


def _cparams(n_axes):
    return pltpu.CompilerParams(dimension_semantics=("arbitrary",) * n_axes,
                                vmem_limit_bytes=VMEM_LIMIT)


def _row_tile(n_rows, cap, mult=SUBLANES):
    best = None
    for t in range(mult, min(cap, n_rows) + 1, mult):
        if n_rows % t == 0:
            best = t
    assert best is not None, (n_rows, cap)
    return best


def _dot(a, b, precision=None):
    return jnp.dot(a, b, preferred_element_type=F32, precision=precision)


def _dot_nt(a, b, precision=None):
    return lax.dot_general(a, b, (((1,), (1,)), ((), ())), preferred_element_type=F32,
                           precision=precision)


def _dot_tn(a, b, precision=None):
    return lax.dot_general(a, b, (((0,), (0,)), ((), ())), preferred_element_type=F32,
                           precision=precision)


_NN = (((1,), (0,)), ((), ()))
_NT = (((1,), (1,)), ((), ()))
_TN = (((0,), (0,)), ((), ()))


def _split2(x):
    hi = x.astype(BF16)
    lo = (x - hi.astype(F32)).astype(BF16)
    return hi, lo


def _split3(x):
    hi = x.astype(BF16)
    r1 = x - hi.astype(F32)
    mid = r1.astype(BF16)
    lo = (r1 - mid.astype(F32)).astype(BF16)
    return hi, mid, lo


def _dg(a, b, dims):
    return lax.dot_general(a, b, dims, preferred_element_type=F32)


def _dot3(a2, b2, dims=_NN):
    (ah, al), (bh, bl) = a2, b2
    return _dg(ah, bh, dims) + _dg(al, bh, dims) + _dg(ah, bl, dims)


def _dot_exact_rhs(x, m_bf16):
    return sum(_dg(part, m_bf16, _NN) for part in _split3(x))


def _dot_exact_lhs(m_bf16, x):
    return sum(_dg(m_bf16, part, _NN) for part in _split3(x))


def _softplus(x):
    return jnp.maximum(x, 0.0) + jnp.log1p(jnp.exp(-jnp.abs(x)))


def _silu(x):
    return x * jax.nn.sigmoid(x)


def _gelu_tanh(x):
    return 0.5 * x * (1.0 + jnp.tanh(0.7978845608028654 * (x + 0.044715 * (x * x * x))))


def _rms(x, g, eps):
    return x * lax.rsqrt(jnp.mean(x * x, axis=-1, keepdims=True) + eps) * g


def _norm_mm_kernel(x_ref, g_ref, w_ref, o_ref, xn_sc):
    @pl.when(pl.program_id(1) == 0)
    def _():
        xn_sc[...] = _rms(x_ref[...], g_ref[...], NORM_EPS).astype(BF16)

    o_ref[...] = _dot(xn_sc[...], w_ref[...].astype(BF16)).astype(o_ref.dtype)


def norm_matmul(x, g, w, col0_blk, n_blk, tn, tm, name, out_dtype=F32):
    t_rows, d = x.shape
    return pl.pallas_call(
        _norm_mm_kernel,
        grid=(t_rows // tm, n_blk),
        in_specs=[pl.BlockSpec((tm, d), lambda i, j: (i, 0)),
                  pl.BlockSpec((1, d), lambda i, j: (0, 0)),
                  pl.BlockSpec((d, tn), lambda i, j: (0, j + col0_blk))],
        out_specs=pl.BlockSpec((tm, tn), lambda i, j: (i, j)),
        out_shape=jax.ShapeDtypeStruct((t_rows, n_blk * tn), out_dtype),
        scratch_shapes=[pltpu.VMEM((tm, d), BF16)],
        compiler_params=_cparams(2),
        name=name,
    )(x, g.reshape(1, d), w)


def _mm_res_kernel(*refs, n_in):
    x_refs, w_refs = refs[:n_in], refs[n_in:2 * n_in]
    res_ref, o_ref = refs[2 * n_in], refs[2 * n_in + 1]
    acc = res_ref[...]
    for x_ref, w_ref in zip(x_refs, w_refs):
        acc = acc + _dot(x_ref[...].astype(BF16), w_ref[...].astype(BF16))
    o_ref[...] = acc


def matmul_residual(xs, w, res, tm, tn, name):
    n_in = len(xs)
    t_rows, kp = xs[0].shape
    n_cols = w.shape[1]
    in_specs = [pl.BlockSpec((tm, kp), lambda i, j: (i, 0)) for _ in xs]
    in_specs += [pl.BlockSpec((kp, tn), functools.partial(lambda i, j, p: (p, j), p=p))
                 for p in range(n_in)]
    in_specs += [pl.BlockSpec((tm, tn), lambda i, j: (i, j))]
    return pl.pallas_call(
        functools.partial(_mm_res_kernel, n_in=n_in),
        grid=(t_rows // tm, n_cols // tn),
        in_specs=in_specs,
        out_specs=pl.BlockSpec((tm, tn), lambda i, j: (i, j)),
        out_shape=jax.ShapeDtypeStruct((t_rows, n_cols), F32),
        compiler_params=_cparams(2),
        name=name,
    )(*xs, *([w] * n_in), res)


def _ffn_kernel(x_ref, g_ref, wg_ref, wu_ref, wd_ref, o_ref, xn_sc, acc_sc):
    j = pl.program_id(1)

    @pl.when(j == 0)
    def _():
        xn_sc[...] = _rms(x_ref[...], g_ref[...], NORM_EPS).astype(BF16)
        acc_sc[...] = jnp.zeros_like(acc_sc)

    xn = xn_sc[...]
    hg = _dot(xn, wg_ref[...].astype(BF16))
    hu = _dot(xn, wu_ref[...].astype(BF16))
    h = (_silu(hg) * hu).astype(BF16)
    acc_sc[...] += _dot(h, wd_ref[...].astype(BF16))

    @pl.when(j == pl.num_programs(1) - 1)
    def _():
        o_ref[...] = x_ref[...] + acc_sc[...]


def ffn_residual(x, g, wg, wu, wd, tm, tf):
    t_rows, d = x.shape
    d_ff = wg.shape[1]
    return pl.pallas_call(
        _ffn_kernel,
        grid=(t_rows // tm, d_ff // tf),
        in_specs=[pl.BlockSpec((tm, d), lambda i, j: (i, 0)),
                  pl.BlockSpec((1, d), lambda i, j: (0, 0)),
                  pl.BlockSpec((d, tf), lambda i, j: (0, j)),
                  pl.BlockSpec((d, tf), lambda i, j: (0, j)),
                  pl.BlockSpec((tf, d), lambda i, j: (j, 0))],
        out_specs=pl.BlockSpec((tm, d), lambda i, j: (i, 0)),
        out_shape=jax.ShapeDtypeStruct((t_rows, d), F32),
        scratch_shapes=[pltpu.VMEM((tm, d), BF16), pltpu.VMEM((tm, d), F32)],
        compiler_params=_cparams(2),
        name="ffn_swiglu",
    )(x, g.reshape(1, d), wg, wu, wd)


def _conv4(u, u1, u2, u3, cw, cb):
    return cb + cw[3:4] * u + cw[2:3] * u1 + cw[1:2] * u2 + cw[0:1] * u3


def _lru_gates(xc, wa, ba, wx, bx, lam):
    xb = xc.astype(BF16)
    r = jax.nn.sigmoid(_dot(xb, wa) + ba)
    i = jax.nn.sigmoid(_dot(xb, wx) + bx)
    log_a = -LRU_C * r * _softplus(-lam)
    a = jnp.exp(log_a)
    u = jnp.sqrt(1.0 - jnp.exp(2.0 * log_a)) * (i * xc)
    return a, u


def _lru_prompt_kernel(x_ref, g_ref, conv0_ref, h0_ref, cw_ref, cb_ref, wa_ref, ba_ref, wx_ref,
                       bx_ref, lam_ref, o_ref, nconv_ref, nh_ref, ext_sc, h_sc, *, lc):
    width = x_ref.shape[1]

    @pl.when(pl.program_id(1) == 0)
    def _():
        ext_sc[0:SUBLANES, :] = jnp.zeros((SUBLANES, width), F32)
        ext_sc[SUBLANES - 3:SUBLANES, :] = conv0_ref[0]
        h_sc[...] = h0_ref[0]

    u = x_ref[...]
    ext_sc[SUBLANES:SUBLANES + lc, :] = u
    xc = _conv4(u, ext_sc[SUBLANES - 1:SUBLANES - 1 + lc, :], ext_sc[SUBLANES - 2:SUBLANES - 2 + lc, :],
                ext_sc[SUBLANES - 3:SUBLANES - 3 + lc, :], cw_ref[...], cb_ref[...])
    tail = ext_sc[lc + SUBLANES - 3:lc + SUBLANES, :]
    ext_sc[SUBLANES - 3:SUBLANES, :] = tail
    nconv_ref[0] = tail

    a, h = _lru_gates(xc, wa_ref[...], ba_ref[...], wx_ref[...], bx_ref[...], lam_ref[...])
    row = lax.broadcasted_iota(jnp.int32, (lc, width), 0)
    s = 1
    while s < lc:
        keep = row >= s
        a_sh = jnp.where(keep, pltpu.roll(a, s, 0), 1.0)
        h_sh = jnp.where(keep, pltpu.roll(h, s, 0), 0.0)
        h = a * h_sh + h
        a = a * a_sh
        s *= 2
    hs = h + a * h_sc[...]
    h_last = hs[lc - 1:lc, :]
    h_sc[...] = h_last
    nh_ref[0] = h_last
    o_ref[...] = hs * _gelu_tanh(g_ref[...])


def lru_prompt(proj_lru, n_batch, seq, conv0, h0, lp):
    width = conv0.shape[-1]
    lc = min(LRU_CHUNK, seq)
    n_chunks = seq // lc
    full = lambda shape: pl.BlockSpec(shape, lambda b, c: (0,) * len(shape))
    return pl.pallas_call(
        functools.partial(_lru_prompt_kernel, lc=lc),
        grid=(n_batch, n_chunks),
        in_specs=[pl.BlockSpec((lc, width), lambda b, c: (b * n_chunks + c, 0)),
                  pl.BlockSpec((lc, width), lambda b, c: (b * n_chunks + c, 1)),
                  pl.BlockSpec((1, 3, width), lambda b, c: (b, 0, 0)),
                  pl.BlockSpec((1, 1, width), lambda b, c: (b, 0, 0)),
                  full((CONV_W, width)), full((1, width)), full((width, width)), full((1, width)),
                  full((width, width)), full((1, width)), full((1, width))],
        out_specs=[pl.BlockSpec((lc, width), lambda b, c: (b * n_chunks + c, 0)),
                   pl.BlockSpec((1, 3, width), lambda b, c: (b, 0, 0)),
                   pl.BlockSpec((1, 1, width), lambda b, c: (b, 0, 0))],
        out_shape=[jax.ShapeDtypeStruct((n_batch * seq, width), F32),
                   jax.ShapeDtypeStruct((n_batch, 3, width), F32),
                   jax.ShapeDtypeStruct((n_batch, 1, width), F32)],
        scratch_shapes=[pltpu.VMEM((lc + SUBLANES, width), F32), pltpu.VMEM((1, width), F32)],
        compiler_params=_cparams(2),
        name="lru_prompt",
    )(proj_lru, proj_lru, conv0, h0.reshape(n_batch, 1, width), lp["cw"], lp["cb"], lp["wa"], lp["ba"],
      lp["wx"], lp["bx"], lp["lam"])


def _rows_call(body, inputs, out_shapes, name):
    return pl.pallas_call(body, out_shape=out_shapes, name=name,
                          compiler_params=pltpu.CompilerParams(vmem_limit_bytes=VMEM_LIMIT))(*inputs)


def _lru_step_kernel(x_ref, g_ref, buf_ref, h0_ref, cw_ref, cb_ref, wa_ref, ba_ref, wx_ref, bx_ref,
                     lam_ref, o_ref, nbuf_ref, nh_ref):
    width = x_ref.shape[1]
    u = x_ref[...]
    b0, b1, b2 = (buf_ref[:, k * width:(k + 1) * width] for k in range(3))
    xc = _conv4(u, b2, b1, b0, cw_ref[...], cb_ref[...])
    a, uu = _lru_gates(xc, wa_ref[...], ba_ref[...], wx_ref[...], bx_ref[...], lam_ref[...])
    h = a * h0_ref[...] + uu
    nh_ref[...] = h
    o_ref[...] = h * _gelu_tanh(g_ref[...])
    nbuf_ref[:, 0:width] = b1
    nbuf_ref[:, width:2 * width] = b2
    nbuf_ref[:, 2 * width:3 * width] = u


def lru_step(x_lru, g_lru, conv_buf, h0, lp):
    n, width = x_lru.shape
    return _rows_call(
        _lru_step_kernel,
        [x_lru, g_lru, conv_buf.reshape(n, 3 * width), h0, lp["cw"], lp["cb"], lp["wa"], lp["ba"],
         lp["wx"], lp["bx"], lp["lam"]],
        [jax.ShapeDtypeStruct((n, width), F32), jax.ShapeDtypeStruct((n, 3 * width), F32),
         jax.ShapeDtypeStruct((n, width), F32)],
        "lru_step")


def _rwkv_rows(mixed, rp):
    w = rp["w0"].shape[1]
    r, k, v = mixed[:, 0:w], mixed[:, w:2 * w], mixed[:, 2 * w:3 * w]
    o = 3 * w
    wd = mixed[:, o:o + DECAY_RANK]
    ad = mixed[:, o + DECAY_RANK:o + DECAY_RANK + ICLR_RANK]
    gd = mixed[:, o + DECAY_RANK + ICLR_RANK:o + DECAY_RANK + ICLR_RANK + GATE_RANK]
    dec_in = rp["w0"] + _dot(jnp.tanh(wd).astype(BF16), rp["wdec"].astype(BF16))
    w_log = -_softplus(-dec_in) - 0.5
    lw = -jnp.exp(w_log)
    iclr = jax.nn.sigmoid(rp["a0"] + _dot(ad.astype(BF16), rp["wiclr"].astype(BF16)))
    gate = _dot(jax.nn.sigmoid(gd).astype(BF16), rp["wgate"].astype(BF16))
    kk = k * rp["kk"]
    ss = _dot_exact_rhs(kk * kk, rp["hsum"])
    kkn = kk / jnp.maximum(jnp.sqrt(ss), 1e-12)
    k2 = k * (1.0 + (iclr - 1.0) * rp["ka"])
    return r, k2, v, lw, -kkn, kkn * iclr, gate


def _rwkv_post(o, r, k2, v, gate, rp):
    inv = 1.0 / RWKV_HEAD
    mu = _dot_exact_rhs(o, rp["hsum"]) * inv
    d = o - mu
    var = _dot_exact_rhs(d * d, rp["hsum"]) * inv
    on = d * lax.rsqrt(var + RWKV_GN_EPS) * rp["lnw"] + rp["lnb"]
    bonus = _dot_exact_rhs(r * k2 * rp["rk"], rp["hsum"]) * v
    return (on + bonus) * gate


_RWKV_PARAM_NAMES = ("mu", "w0", "wdec", "a0", "wiclr", "wgate", "kk", "ka", "rk", "lnw", "lnb", "hsum")


def _rwkv_prompt_kernel(*refs, chunk, nbs):
    p_refs, shift0_ref, s0_ref, rest = refs[:nbs], refs[nbs], refs[nbs + 1], refs[nbs + 2:]
    n_prm = len(_RWKV_PARAM_NAMES)
    rp = {name: ref[...] for name, ref in zip(_RWKV_PARAM_NAMES, rest[:n_prm])}
    o_ref, nshift_ref, nwkv_ref, prev_sc, s_sc, o_sc = rest[n_prm:]
    n_pairs = s_sc.shape[0] // nbs
    hd = RWKV_HEAD
    assert chunk == hd, "the pair-packed layout below uses chunk == head size"

    @pl.when(pl.program_id(1) == 0)
    def _():
        for i in range(nbs):
            prev_sc[i] = shift0_ref[i]
            for p in range(n_pairs):
                s_sc[i * n_pairs + p] = jnp.concatenate([s0_ref[i, 2 * p], s0_ref[i, 2 * p + 1]], axis=1)

    ti = lax.broadcasted_iota(jnp.int32, (chunk, chunk), 0)
    si = lax.broadcasted_iota(jnp.int32, (chunk, chunk), 1)
    lower = (ti >= si).astype(BF16)

    def prep(i):
        p = p_refs[i][...]
        row = lax.broadcasted_iota(jnp.int32, p.shape, 0)
        p_prev = jnp.where(row >= 1, pltpu.roll(p, 1, 0), prev_sc[i])
        last = p[chunk - 1:chunk, :]
        prev_sc[i] = last
        nshift_ref[i] = last
        mixed = p + (p_prev - p) * rp["mu"]
        r, k2, v, lw, a, b, gate = _rwkv_rows(mixed, rp)
        cs = _dot_exact_lhs(lower, lw)
        g_in, g_ex, g_inv = jnp.exp(cs), jnp.exp(cs - lw), jnp.exp(-cs)
        bt, kt = b * g_inv, k2 * g_inv
        g_end = g_in[chunk - 1:chunk, :]
        return dict(r=r, k2=k2, v=v, gate=gate, at=a * g_ex, bt=bt, kt=kt, rt=r * g_in, g_end=g_end,
                    bc=bt * g_end, kc=kt * g_end)

    preps = [prep(i) for i in range(nbs)]
    n_fac = max(1, (chunk - 1).bit_length())
    pw_ = 2 * hd
    lane1 = lax.broadcasted_iota(jnp.int32, (1, pw_), 1)
    lo1 = lane1 < hd
    lo2 = jnp.concatenate([lo1, lo1], axis=1)
    ti2 = lax.broadcasted_iota(jnp.int32, (chunk, pw_), 0)
    si2 = lax.broadcasted_iota(jnp.int32, (chunk, pw_), 1) % hd
    strict2, incl2 = ti2 > si2, ti2 >= si2
    rr = lax.broadcasted_iota(jnp.int32, (pw_, pw_), 0)
    cc = lax.broadcasted_iota(jnp.int32, (pw_, pw_), 1)
    same_head, eye2 = (rr < hd) == (cc < hd), rr == cc

    def bdiag(x, lo):
        return jnp.concatenate([jnp.where(lo, x, 0), jnp.where(lo, 0, x)], axis=0)

    def bdiag2(x2, lo):
        return bdiag(x2[0], lo), bdiag(x2[1], lo)

    units = [(i, q) for i in range(nbs) for q in range(n_pairs)]
    pairs = range(len(units))
    pls = [slice(q * pw_, (q + 1) * pw_) for _, q in units]
    a_p, r_p, v_p, bt_p, kt_p, bc_p, kc_p, ge_p = (
        [preps[i][name][:, pls[u]] for u, (i, _) in enumerate(units)]
        for name in ("at", "rt", "v", "bt", "kt", "bc", "kc", "g_end"))
    ms = []
    for p in pairs:
        b2, k2_ = _split2(bt_p[p]), _split2(kt_p[p])
        rhs_rows = tuple(jnp.concatenate([bdiag(b2[i], lo1), bdiag(k2_[i], lo1)], axis=0) for i in range(2))
        ms.append(_dot3(_split2(jnp.concatenate([a_p[p], r_p[p]], axis=0)), rhs_rows, _NT))
    labs = [jnp.where(strict2, m[:chunk, :pw_], 0.0) for m in ms]
    lrbs = [jnp.where(incl2, m[chunk:, :pw_], 0.0).astype(BF16) for m in ms]
    lkvs = [_dot3(_split2(jnp.concatenate([jnp.where(strict2, ms[p][:chunk, pw_:], 0.0),
                                           jnp.where(incl2, ms[p][chunk:, pw_:], 0.0)], axis=0)),
                  bdiag2(_split2(v_p[p]), lo1)) for p in pairs]
    ys = [jnp.concatenate([a_p[p], lkvs[p][:chunk]], axis=1) for p in pairs]
    pws = [_split2(lab) for lab in labs]
    for f in range(n_fac):
        ys = [ys[p] + _dot3(pws[p], bdiag2(_split2(ys[p]), lo2)) for p in pairs]
        if f + 1 < n_fac:
            pws = [_split2(_dot3(pw, bdiag2(pw, lo1))) for pw in pws]
    y2s = [_split2(y) for y in ys]
    qos = [jnp.concatenate([r_p[p], lkvs[p][chunk:]], axis=1) + _dg(lrbs[p], bdiag(y2s[p][0], lo2), _NN)
           for p in pairs]
    mns = [_dot3(y2s[p], _split2(bc_p[p]), _TN) for p in pairs]
    vks = [_dot3(_split2(v_p[p]), _split2(kc_p[p]), _TN) for p in pairs]
    m_ps = [_split2(jnp.where(same_head, mns[p][:pw_], 0.0) + jnp.where(eye2, ge_p[p], 0.0))
            for p in pairs]
    n0s = [jnp.where(lo1, mns[p][pw_:pw_ + hd] + vks[p][:hd], mns[p][pw_ + hd:] + vks[p][hd:])
           for p in pairs]
    s0s = [_split2(s_sc[p]) for p in pairs]
    for p, (i, _) in enumerate(units):
        o_sc[i, :, pls[p]] = (_dg(qos[p][:, :pw_].astype(BF16), bdiag(s0s[p][0], lo1), _NT)
                              + qos[p][:, pw_:])
    for p in pairs:
        s_sc[p] = _dot3(s0s[p], m_ps[p]) + n0s[p]

    @pl.when(pl.program_id(1) == pl.num_programs(1) - 1)
    def _():
        for p, (i, q) in enumerate(units):
            s_pair = s_sc[p]
            nwkv_ref[i, 2 * q] = s_pair[:, :hd]
            nwkv_ref[i, 2 * q + 1] = s_pair[:, hd:]

    for i, pre in enumerate(preps):
        o_ref[i] = _rwkv_post(o_sc[i], pre["r"], pre["k2"], pre["v"], pre["gate"], rp)


def _rwkv_param_list(rp):
    return [rp[name] for name in _RWKV_PARAM_NAMES]


def rwkv_prompt(p_rwkv, n_batch, seq, shift0, wkv0, rp):
    cols = shift0.shape[-1]
    n_heads, hd = wkv0.shape[1], wkv0.shape[2]
    width = n_heads * hd
    chunk = min(RWKV_CHUNK, seq)
    n_chunks = seq // chunk
    prm = _rwkv_param_list(rp)
    prm_specs = [pl.BlockSpec(x.shape, lambda b, c: (0, 0)) for x in prm]
    nbs = 2 if n_batch % 2 == 0 else 1
    p_specs = [pl.BlockSpec((chunk, cols), functools.partial(
        lambda b, c, i: ((b * nbs + i) * n_chunks + c, 0), i=i)) for i in range(nbs)]
    out, new_shift, new_wkv = pl.pallas_call(
        functools.partial(_rwkv_prompt_kernel, chunk=chunk, nbs=nbs),
        grid=(n_batch // nbs, n_chunks),
        in_specs=p_specs + [pl.BlockSpec((nbs, 1, cols), lambda b, c: (b, 0, 0)),
                            pl.BlockSpec((nbs, n_heads, hd, hd), lambda b, c: (b, 0, 0, 0))] + prm_specs,
        out_specs=[pl.BlockSpec((nbs, chunk, width), lambda b, c: (b, c, 0)),
                   pl.BlockSpec((nbs, 1, cols), lambda b, c: (b, 0, 0)),
                   pl.BlockSpec((nbs, n_heads, hd, hd), lambda b, c: (b, 0, 0, 0))],
        out_shape=[jax.ShapeDtypeStruct((n_batch, seq, width), F32),
                   jax.ShapeDtypeStruct((n_batch, 1, cols), F32),
                   jax.ShapeDtypeStruct((n_batch, n_heads, hd, hd), F32)],
        scratch_shapes=[pltpu.VMEM((nbs, 1, cols), F32), pltpu.VMEM((nbs * n_heads // 2, hd, 2 * hd), F32),
                        pltpu.VMEM((nbs, chunk, width), F32)],
        compiler_params=_cparams(2),
        name="rwkv_prompt",
    )(*([p_rwkv] * nbs), shift0.reshape(n_batch, 1, cols), wkv0, *prm)
    return out.reshape(n_batch * seq, width), new_shift, new_wkv


def _rwkv_step_pre_kernel(p_ref, prev_ref, *rest):
    n_prm = len(_RWKV_PARAM_NAMES)
    rp = {name: ref[...] for name, ref in zip(_RWKV_PARAM_NAMES, rest[:n_prm])}
    r_ref, k_ref, v_ref, w_ref, a_ref, b_ref, gate_ref = rest[n_prm:]
    p = p_ref[...]
    mixed = p + (prev_ref[...] - p) * rp["mu"]
    r, k2, v, lw, a, b, gate = _rwkv_rows(mixed, rp)
    r_ref[...] = r
    k_ref[...] = k2
    v_ref[...] = v
    w_ref[...] = jnp.exp(lw)
    a_ref[...] = a
    b_ref[...] = b
    gate_ref[...] = gate


def _rwkv_step_kernel(s_ref, w_ref, a_ref, b_ref, k_ref, r_ref, vt_ref, ns_ref, ot_ref):
    bt, n_heads = s_ref.shape[0], s_ref.shape[1]
    lane = lax.broadcasted_iota(jnp.int32, ot_ref.shape[1:], 1)
    for i in range(bt):
        vt = vt_ref[i]
        ss = [s_ref[i, h] for h in range(n_heads)]
        sas = [jnp.sum(ss[h] * a_ref[i, h], axis=-1, keepdims=True) for h in range(n_heads)]
        s_news = [ss[h] * w_ref[i, h] + sas[h] * b_ref[i, h] + vt[:, h:h + 1] * k_ref[i, h]
                  for h in range(n_heads)]
        for h in range(n_heads):
            ns_ref[i, h] = s_news[h]
        os_ = [jnp.sum(s_news[h] * r_ref[i, h], axis=-1, keepdims=True) for h in range(n_heads)]
        ot = jnp.zeros(ot_ref.shape[1:], F32)
        for h in range(n_heads):
            ot = jnp.where(lane == h, os_[h], ot)
        ot_ref[i] = ot


def _rwkv_step_post_kernel(o_ref, r_ref, k_ref, v_ref, gate_ref, *rest):
    n_prm = len(_RWKV_PARAM_NAMES)
    rp = {name: ref[...] for name, ref in zip(_RWKV_PARAM_NAMES, rest[:n_prm])}
    out_ref = rest[n_prm]
    out_ref[...] = _rwkv_post(o_ref[...], r_ref[...], k_ref[...], v_ref[...], gate_ref[...], rp)


def rwkv_step(p_rwkv, shift_prev, wkv0, rp):
    n = p_rwkv.shape[0]
    n_heads, hd = wkv0.shape[1], wkv0.shape[2]
    width = n_heads * hd
    prm = _rwkv_param_list(rp)
    row = jax.ShapeDtypeStruct((n, width), F32)
    r, k2, v, w, a, b, gate = _rows_call(_rwkv_step_pre_kernel, [p_rwkv, shift_prev, *prm], [row] * 7,
                                         "rwkv_step_pre")
    hrow = lambda z: z.reshape(n, n_heads, 1, hd)
    vt = jnp.transpose(v.reshape(n, n_heads, hd), (0, 2, 1))
    bt = SUBLANES
    vec_spec = pl.BlockSpec((bt, n_heads, 1, hd), lambda i: (i, 0, 0, 0))
    st_spec = pl.BlockSpec((bt, n_heads, hd, hd), lambda i: (i, 0, 0, 0))
    t_spec = pl.BlockSpec((bt, hd, n_heads), lambda i: (i, 0, 0))
    new_wkv, ot = pl.pallas_call(
        _rwkv_step_kernel,
        grid=(n // bt,),
        in_specs=[st_spec] + [vec_spec] * 5 + [t_spec],
        out_specs=[st_spec, t_spec],
        out_shape=[jax.ShapeDtypeStruct(wkv0.shape, F32), jax.ShapeDtypeStruct((n, hd, n_heads), F32)],
        compiler_params=_cparams(1),
        name="rwkv_step",
    )(wkv0, hrow(w), hrow(a), hrow(b), hrow(k2), hrow(r), vt)
    o = jnp.transpose(ot, (0, 2, 1)).reshape(n, width)
    (out_b,) = _rows_call(_rwkv_step_post_kernel, [o, r, k2, v, gate, *prm], [row], "rwkv_step_post")
    return out_b, new_wkv


def _mamba_post(y, xs, z, dexp, ng):
    y = (y + dexp * xs) * _silu(z)
    gw = y.shape[1] // SSM_GROUPS
    parts = []
    for g in range(SSM_GROUPS):
        yg = y[:, g * gw:(g + 1) * gw]
        parts.append(yg * lax.rsqrt(jnp.mean(yg * yg, axis=-1, keepdims=True) + SSM_NORM_EPS))
    return jnp.concatenate(parts, axis=1) * ng


def _ssd_prompt_kernel(z_ref, xlo_ref, xhi_ref, dt_ref, conv0_ref, s0_ref, cw_ref, cb_ref, dtb_ref,
                       alog_ref, dexp_ref, ng_ref, hexp_ref, qexp_ref, y_ref, nconv_ref, nssm_ref,
                       ext_sc, s_sc, y_sc, yo_sc, *, q):
    inner = xlo_ref.shape[1]
    n_heads = s_sc.shape[0]
    hpg = n_heads // SSM_GROUPS

    @pl.when(pl.program_id(1) == 0)
    def _():
        ext_sc[0:SUBLANES, :] = jnp.zeros((SUBLANES, ext_sc.shape[1]), F32)
        ext_sc[SUBLANES - 3:SUBLANES, :] = conv0_ref[0]
        s_sc[...] = s0_ref[0]

    ext_sc[SUBLANES:SUBLANES + q, 0:inner] = xlo_ref[...].astype(F32)
    ext_sc[SUBLANES:SUBLANES + q, inner:] = xhi_ref[...].astype(F32)
    xbc = _silu(_conv4(ext_sc[SUBLANES:SUBLANES + q, :], ext_sc[SUBLANES - 1:SUBLANES - 1 + q, :],
                       ext_sc[SUBLANES - 2:SUBLANES - 2 + q, :], ext_sc[SUBLANES - 3:SUBLANES - 3 + q, :],
                       cw_ref[...], cb_ref[...]))
    tail = ext_sc[q + SUBLANES - 3:q + SUBLANES, :]
    ext_sc[SUBLANES - 3:SUBLANES, :] = tail
    nconv_ref[0] = tail

    xs = xbc[:, 0:inner]
    gn = SSM_GROUPS * SSM_STATE
    bm = xbc[:, inner:inner + gn].astype(BF16)
    cm = xbc[:, inner + gn:].astype(BF16)
    dt = _softplus(dt_ref[...] + dtb_ref[...])
    dta = dt * (-jnp.exp(alog_ref[...]))
    ti = lax.broadcasted_iota(jnp.int32, (q, q), 0)
    si = lax.broadcasted_iota(jnp.int32, (q, q), 1)
    causal = ti >= si
    da = _dot_exact_lhs(causal.astype(BF16), dta)
    upper = (ti <= si).astype(BF16)
    da_t = sum(_dg(part, upper, _TN) for part in _split3(dta))
    da_end = da[q - 1:q, :]
    end_decay = jnp.exp(da_end)
    hexp, qexp = hexp_ref[...], qexp_ref[...]
    xdt = xs * _dot_exact_rhs(dt, hexp)
    xdt_b = xdt.astype(BF16)
    xdt_end = (xdt * _dot_exact_rhs(jnp.exp(da_end - da), hexp)).astype(BF16)
    from_start = _dot_exact_rhs(jnp.exp(da), hexp)
    da_col = _dot_exact_rhs(da, qexp)

    heads = range(n_heads)
    bgs = [bm[:, g * SSM_STATE:(g + 1) * SSM_STATE] for g in range(SSM_GROUPS)]
    cgs = [cm[:, g * SSM_STATE:(g + 1) * SSM_STATE] for g in range(SSM_GROUPS)]
    scores = [_dot_nt(cgs[g], bgs[g]) for g in range(SSM_GROUPS)]
    hss = [slice(h * SSM_HEAD, (h + 1) * SSM_HEAD) for h in heads]
    s_hs = [s_sc[h] for h in heads]
    for h in heads:
        yo_sc[:, hss[h]] = _dot_nt(cgs[h // hpg], s_hs[h].astype(BF16))
    wts = [(scores[h // hpg]
            * jnp.exp(jnp.where(causal, da_col[:, h * q:(h + 1) * q] - da_t[h:h + 1, :], -jnp.inf))
            ).astype(BF16) for h in heads]
    for h in heads:
        y_sc[:, hss[h]] = _dot(wts[h], xdt_b[:, hss[h]])
    upd = [_dot_tn(xdt_end[:, hss[h]], bgs[h // hpg]) for h in heads]
    for h in heads:
        s_sc[h] = s_hs[h] * end_decay[:, h:h + 1] + upd[h]

    @pl.when(pl.program_id(1) == pl.num_programs(1) - 1)
    def _():
        nssm_ref[0] = s_sc[...]

    y = y_sc[...] + yo_sc[...] * from_start
    y_ref[...] = _mamba_post(y, xs, z_ref[...].astype(F32), dexp_ref[...], ng_ref[...])


def ssd_prompt(proj1, dt_raw, n_batch, seq, conv0, ssm0, mp):
    n_heads, hd, n_state = ssm0.shape[1:]
    inner = n_heads * hd
    conv_ch = conv0.shape[-1]
    q = SSM_CHUNK if seq % SSM_CHUNK == 0 else seq
    n_chunks = seq // q
    full = lambda shape: pl.BlockSpec(shape, lambda b, c: (0,) * len(shape))
    return pl.pallas_call(
        functools.partial(_ssd_prompt_kernel, q=q),
        grid=(n_batch, n_chunks),
        in_specs=[pl.BlockSpec((q, inner), lambda b, c: (b * n_chunks + c, 0)),
                  pl.BlockSpec((q, inner), lambda b, c: (b * n_chunks + c, 1)),
                  pl.BlockSpec((q, inner), lambda b, c: (b * n_chunks + c, 2)),
                  pl.BlockSpec((q, n_heads), lambda b, c: (b * n_chunks + c, 0)),
                  pl.BlockSpec((1, 3, conv_ch), lambda b, c: (b, 0, 0)),
                  pl.BlockSpec((1, n_heads, hd, n_state), lambda b, c: (b, 0, 0, 0)),
                  full((CONV_W, conv_ch)), full((1, conv_ch)), full((1, n_heads)), full((1, n_heads)),
                  full((1, inner)), full((1, inner)), full((n_heads, inner)), full((n_heads, n_heads * q))],
        out_specs=[pl.BlockSpec((q, inner), lambda b, c: (b * n_chunks + c, 0)),
                   pl.BlockSpec((1, 3, conv_ch), lambda b, c: (b, 0, 0)),
                   pl.BlockSpec((1, n_heads, hd, n_state), lambda b, c: (b, 0, 0, 0))],
        out_shape=[jax.ShapeDtypeStruct((n_batch * seq, inner), F32),
                   jax.ShapeDtypeStruct((n_batch, 3, conv_ch), F32),
                   jax.ShapeDtypeStruct(ssm0.shape, F32)],
        scratch_shapes=[pltpu.VMEM((q + SUBLANES, conv_ch), F32), pltpu.VMEM((n_heads, hd, n_state), F32),
                        pltpu.VMEM((q, inner), F32), pltpu.VMEM((q, inner), F32)],
        compiler_params=_cparams(2),
        name="ssd_prompt",
    )(proj1, proj1, proj1, dt_raw, conv0, ssm0, mp["cw"], mp["cb"], mp["dtb"], mp["alog"], mp["dexp"],
      mp["ng"], jnp.repeat(jnp.eye(n_heads, dtype=BF16), hd, axis=1),
      jnp.repeat(jnp.eye(n_heads, dtype=BF16), q, axis=1))


def _ssd_step_pre_kernel(x_ref, dt_ref, buf_ref, cw_ref, cb_ref, dtb_ref, alog_ref,
                         xs_ref, b_ref, c_ref, dt_out_ref, dec_ref, nbuf_ref):
    ch = x_ref.shape[1]
    inner = xs_ref.shape[1]
    gn = b_ref.shape[1]
    u = x_ref[...].astype(F32)
    b0, b1, b2 = (buf_ref[:, k * ch:(k + 1) * ch] for k in range(3))
    xbc = _silu(_conv4(u, b2, b1, b0, cw_ref[...], cb_ref[...]))
    xs_ref[...] = xbc[:, 0:inner]
    b_ref[...] = xbc[:, inner:inner + gn]
    c_ref[...] = xbc[:, inner + gn:]
    dt = _softplus(dt_ref[...] + dtb_ref[...])
    dt_out_ref[...] = dt
    dec_ref[...] = jnp.exp(dt * (-jnp.exp(alog_ref[...])))
    nbuf_ref[:, 0:ch] = b1
    nbuf_ref[:, ch:2 * ch] = b2
    nbuf_ref[:, 2 * ch:3 * ch] = u


def _ssd_step_kernel(s_ref, xt_ref, b_ref, c_ref, dt_ref, dec_ref, ns_ref, yt_ref):
    bt, n_heads = s_ref.shape[0], s_ref.shape[1]
    hpg = n_heads // SSM_GROUPS
    lane = lax.broadcasted_iota(jnp.int32, yt_ref.shape[1:], 1)
    for i in range(bt):
        xt = xt_ref[i]
        dt = dt_ref[i]
        dec = dec_ref[i]
        xdt = xt * dt
        s_news = [s_ref[i, h] * dec[:, h:h + 1] + xdt[:, h:h + 1] * b_ref[i, h // hpg]
                  for h in range(n_heads)]
        for h in range(n_heads):
            ns_ref[i, h] = s_news[h]
        ys = [jnp.sum(s_news[h] * c_ref[i, h // hpg], axis=-1, keepdims=True) for h in range(n_heads)]
        yt = jnp.zeros(yt_ref.shape[1:], F32)
        for h in range(n_heads):
            yt = jnp.where(lane == h, ys[h], yt)
        yt_ref[i] = yt


def _ssd_step_post_kernel(y_ref, xs_ref, z_ref, dexp_ref, ng_ref, o_ref):
    o_ref[...] = _mamba_post(y_ref[...], xs_ref[...], z_ref[...].astype(F32), dexp_ref[...], ng_ref[...])


def ssd_step(z, xbc_raw, dt_raw, conv_buf, ssm0, mp):
    n, conv_ch = xbc_raw.shape
    n_heads, hd, n_state = ssm0.shape[1:]
    inner = n_heads * hd
    gn = SSM_GROUPS * n_state
    xs, bm, cm, dt, dec, nbuf = _rows_call(
        _ssd_step_pre_kernel,
        [xbc_raw, dt_raw, conv_buf.reshape(n, 3 * conv_ch), mp["cw"], mp["cb"], mp["dtb"], mp["alog"]],
        [jax.ShapeDtypeStruct((n, inner), F32), jax.ShapeDtypeStruct((n, gn), F32),
         jax.ShapeDtypeStruct((n, gn), F32), jax.ShapeDtypeStruct((n, n_heads), F32),
         jax.ShapeDtypeStruct((n, n_heads), F32), jax.ShapeDtypeStruct((n, 3 * conv_ch), F32)],
        "ssd_step_pre")
    xt = jnp.transpose(xs.reshape(n, n_heads, hd), (0, 2, 1))
    bt = 4
    st_spec = pl.BlockSpec((bt, n_heads, hd, n_state), lambda i: (i, 0, 0, 0))
    t_spec = pl.BlockSpec((bt, hd, n_heads), lambda i: (i, 0, 0))
    g_spec = pl.BlockSpec((bt, SSM_GROUPS, 1, n_state), lambda i: (i, 0, 0, 0))
    h_spec = pl.BlockSpec((bt, 1, n_heads), lambda i: (i, 0, 0))
    new_ssm, yt = pl.pallas_call(
        _ssd_step_kernel,
        grid=(n // bt,),
        in_specs=[st_spec, t_spec, g_spec, g_spec, h_spec, h_spec],
        out_specs=[st_spec, t_spec],
        out_shape=[jax.ShapeDtypeStruct(ssm0.shape, F32), jax.ShapeDtypeStruct((n, hd, n_heads), F32)],
        compiler_params=_cparams(1),
        name="ssd_step",
    )(ssm0, xt, bm.reshape(n, SSM_GROUPS, 1, n_state), cm.reshape(n, SSM_GROUPS, 1, n_state),
      dt.reshape(n, 1, n_heads), dec.reshape(n, 1, n_heads))
    y = jnp.transpose(yt, (0, 2, 1)).reshape(n, inner)
    (y,) = _rows_call(_ssd_step_post_kernel, [y, xs, z, mp["dexp"], mp["ng"]],
                      [jax.ShapeDtypeStruct((n, inner), F32)], "ssd_step_post")
    return y, nbuf, new_ssm


def _router_kernel(xp_ref, xs_ref, g_ref, rwt_ref, rb_ref, xn_ref, idx_ref, gate_ref, rank_ref, cnt_ref,
                   cnt_sc, *, n_prompt_tiles):
    tm = xp_ref.shape[0]

    @pl.when(pl.program_id(0) == 0)
    def _():
        cnt_sc[...] = jnp.zeros_like(cnt_sc)

    x = jnp.where(pl.program_id(0) < n_prompt_tiles, xp_ref[...], xs_ref[...])
    xn = _rms(x, g_ref[...], NORM_EPS)
    xn_ref[...] = xn
    logits = _dot_nt(rwt_ref[...], xn, HIGHEST) + rb_ref[...]
    e_iota = lax.broadcasted_iota(jnp.int32, logits.shape, 0)
    m1 = jnp.max(logits, axis=0, keepdims=True)
    i1 = jnp.min(jnp.where(logits == m1, e_iota, N_EXPERTS), axis=0, keepdims=True)
    rest = jnp.where(e_iota == i1, -jnp.inf, logits)
    m2 = jnp.max(rest, axis=0, keepdims=True)
    i2 = jnp.min(jnp.where(rest == m2, e_iota, N_EXPERTS), axis=0, keepdims=True)
    e2 = jnp.exp(m2 - m1)
    denom = 1.0 + e2
    idx_ref[...] = jnp.concatenate([i1, i2], axis=0)
    gate_ref[...] = jnp.concatenate([1.0 / denom, e2 / denom], axis=0)
    oh1 = (e_iota == i1).astype(F32)
    oh2 = (e_iota == i2).astype(F32)
    oh = oh1 + oh2
    ti = lax.broadcasted_iota(jnp.int32, (tm, tm), 0)
    si = lax.broadcasted_iota(jnp.int32, (tm, tm), 1)
    before = _dot(oh.astype(BF16), (ti < si).astype(BF16)) + cnt_sc[:, 0:1]
    rank_ref[...] = jnp.concatenate(
        [jnp.sum(oh1 * before, axis=0, keepdims=True), jnp.sum(oh2 * before, axis=0, keepdims=True)],
        axis=0).astype(jnp.int32)
    cnt_sc[...] = cnt_sc[...] + jnp.sum(oh, axis=1, keepdims=True)
    cnt_ref[...] = cnt_sc[...].astype(jnp.int32)


def moe_router(x_prompt, x_sample, g, router_w, router_b, tm):
    d = x_prompt.shape[1]
    n_p, n_s = x_prompt.shape[0] // tm, x_sample.shape[0] // tm
    assert n_p * tm == x_prompt.shape[0] and n_s * tm == x_sample.shape[0]
    t_rows = (n_p + n_s) * tm
    return pl.pallas_call(
        functools.partial(_router_kernel, n_prompt_tiles=n_p),
        grid=(n_p + n_s,),
        in_specs=[pl.BlockSpec((tm, d), lambda i: (jnp.minimum(i, n_p - 1), 0)),
                  pl.BlockSpec((tm, d), lambda i: (jnp.maximum(i - n_p, 0), 0)),
                  pl.BlockSpec((1, d), lambda i: (0, 0)),
                  pl.BlockSpec((N_EXPERTS, d), lambda i: (0, 0)),
                  pl.BlockSpec((N_EXPERTS, 1), lambda i: (0, 0))],
        out_specs=[pl.BlockSpec((tm, d), lambda i: (i, 0)),
                   pl.BlockSpec((2, tm), lambda i: (0, i)),
                   pl.BlockSpec((2, tm), lambda i: (0, i)),
                   pl.BlockSpec((2, tm), lambda i: (0, i)),
                   pl.BlockSpec((N_EXPERTS, LANES), lambda i: (0, 0))],
        out_shape=[jax.ShapeDtypeStruct((t_rows, d), F32),
                   jax.ShapeDtypeStruct((2, t_rows), jnp.int32),
                   jax.ShapeDtypeStruct((2, t_rows), F32),
                   jax.ShapeDtypeStruct((2, t_rows), jnp.int32),
                   jax.ShapeDtypeStruct((N_EXPERTS, LANES), jnp.int32)],
        scratch_shapes=[pltpu.VMEM((N_EXPERTS, LANES), F32)],
        compiler_params=_cparams(1),
        name="moe_router",
    )(x_prompt, x_sample, g.reshape(1, d), router_w.T, router_b.reshape(N_EXPERTS, 1))


def _row_copy(src_ref, src_row, dst_ref, dst_row, sem):
    return pltpu.make_async_copy(src_ref.at[pl.ds(src_row, 1), :], dst_ref.at[pl.ds(dst_row, 1), :], sem)


def _dispatch_kernel(dest_ref, x_ref, slots_in_ref, slots_ref, sem, *, t_rows):
    del slots_in_ref
    tm = x_ref.shape[0]
    base = pl.program_id(0) * tm

    def copies(r):
        return [_row_copy(x_ref, r, slots_ref, dest_ref[k * t_rows + base + r], sem) for k in range(2)]

    def start(r, carry):
        for k, cp in enumerate(copies(r)):
            cp.start(priority=k)
        return carry

    def wait(r, carry):
        for cp in copies(r):
            cp.wait()
        return carry

    lax.fori_loop(0, tm, start, 0, unroll=DMA_UNROLL)
    lax.fori_loop(0, tm, wait, 0, unroll=DMA_UNROLL)


def moe_dispatch(xn, dest_flat, n_slots, tm):
    t_rows, d = xn.shape
    return pl.pallas_call(
        functools.partial(_dispatch_kernel, t_rows=t_rows),
        grid_spec=pltpu.PrefetchScalarGridSpec(
            num_scalar_prefetch=1,
            grid=(t_rows // tm,),
            in_specs=[pl.BlockSpec((tm, d), lambda i, dest: (i, 0)),
                      pl.BlockSpec(memory_space=pl.ANY)],
            out_specs=pl.BlockSpec(memory_space=pl.ANY),
            scratch_shapes=[pltpu.SemaphoreType.DMA(())]),
        out_shape=jax.ShapeDtypeStruct((n_slots, d), F32),
        input_output_aliases={2: 0},
        compiler_params=_cparams(1),
        name="moe_dispatch",
    )(dest_flat, xn, jnp.zeros((n_slots, d), F32))


def _moe_kernel(te_ref, tv_ref, x_ref, wg_ref, wu_ref, wd_ref, o_ref, xb_sc, acc_sc):
    i, j = pl.program_id(0), pl.program_id(1)
    valid = tv_ref[i] == 1

    @pl.when(j == 0)
    def _():
        xb_sc[...] = x_ref[...].astype(BF16)
        acc_sc[...] = jnp.zeros_like(acc_sc)

    @pl.when(valid)
    def _():
        xb = xb_sc[...]
        hg = _dot(xb, wg_ref[0].astype(BF16))
        hu = _dot(xb, wu_ref[0].astype(BF16))
        h = (_silu(hg) * hu).astype(BF16)
        acc_sc[...] += _dot(h, wd_ref[0].astype(BF16))

    @pl.when(j == pl.num_programs(1) - 1)
    def _():
        o_ref[...] = acc_sc[...]


def moe_experts(slots, tile_expert, tile_valid, wg, wu, wd, tm, tf):
    n_slots, d = slots.shape
    d_ff = wg.shape[2]
    n_f = d_ff // tf

    def f_idx(i, j, te, tv):
        return jnp.where(tv[i] == 1, j, n_f - 1)

    return pl.pallas_call(
        _moe_kernel,
        grid_spec=pltpu.PrefetchScalarGridSpec(
            num_scalar_prefetch=2,
            grid=(n_slots // tm, n_f),
            in_specs=[pl.BlockSpec((tm, d), lambda i, j, te, tv: (i, 0)),
                      pl.BlockSpec((1, d, tf), lambda i, j, te, tv: (te[i], 0, f_idx(i, j, te, tv))),
                      pl.BlockSpec((1, d, tf), lambda i, j, te, tv: (te[i], 0, f_idx(i, j, te, tv))),
                      pl.BlockSpec((1, tf, d), lambda i, j, te, tv: (te[i], f_idx(i, j, te, tv), 0))],
            out_specs=pl.BlockSpec((tm, d), lambda i, j, te, tv: (i, 0)),
            scratch_shapes=[pltpu.VMEM((tm, d), BF16), pltpu.VMEM((tm, d), F32)]),
        out_shape=jax.ShapeDtypeStruct((n_slots, d), F32),
        compiler_params=_cparams(2),
        name="moe_experts",
    )(tile_expert, tile_valid, slots, wg, wu, wd)


def _combine_kernel(dest_ref, x_ref, gates_ref, g_ref, y_hbm_ref, o_ref, buf_sc, sem, *, t_rows, row0):
    tm = x_ref.shape[0]
    base = row0 + pl.program_id(0) * tm

    def copies(r):
        return [_row_copy(y_hbm_ref, dest_ref[k * t_rows + base + r], buf_sc.at[k], r, sem)
                for k in range(2)]

    def start(r, carry):
        for k, cp in enumerate(copies(r)):
            cp.start(priority=k)
        return carry

    def wait(r, carry):
        for cp in copies(r):
            cp.wait()
        return carry

    lax.fori_loop(0, tm, start, 0, unroll=DMA_UNROLL)
    lax.fori_loop(0, tm, wait, 0, unroll=DMA_UNROLL)
    gates = gates_ref[...]
    out = x_ref[...] + (gates[:, 0:1] * buf_sc[0] + gates[:, 1:2] * buf_sc[1])
    o_ref[...] = _rms(out, g_ref[...], NORM_EPS)


def moe_combine(x, gates_col, dest_flat, y_slots, g_final, tm, row0):
    n_rows, d = x.shape
    t_rows = gates_col.shape[0]
    assert row0 % tm == 0 and n_rows % tm == 0
    blk0 = row0 // tm
    return pl.pallas_call(
        functools.partial(_combine_kernel, t_rows=t_rows, row0=row0),
        grid_spec=pltpu.PrefetchScalarGridSpec(
            num_scalar_prefetch=1,
            grid=(n_rows // tm,),
            in_specs=[pl.BlockSpec((tm, d), lambda i, dest: (i, 0)),
                      pl.BlockSpec((tm, 2), lambda i, dest: (i + blk0, 0)),
                      pl.BlockSpec((1, d), lambda i, dest: (0, 0)),
                      pl.BlockSpec(memory_space=pl.ANY)],
            out_specs=pl.BlockSpec((tm, d), lambda i, dest: (i, 0)),
            scratch_shapes=[pltpu.VMEM((2, tm, d), F32), pltpu.SemaphoreType.DMA(())]),
        out_shape=jax.ShapeDtypeStruct((n_rows, d), F32),
        compiler_params=_cparams(1),
        name="moe_combine",
    )(dest_flat, x, gates_col, g_final.reshape(1, d), y_slots)


def moe_final(x_prompt, x_sample, g_ffn, g_final, router_w, router_b, wg, wu, wd):
    n_prompt, n_sample = x_prompt.shape[0], x_sample.shape[0]
    t_rows = n_prompt + n_sample
    xn, idx, gates, rank, counts = moe_router(x_prompt, x_sample, g_ffn, router_w, router_b, ROUTER_TILE)
    counts = counts[:, 0]
    tm = MOE_TILE
    n_tiles = -(-2 * t_rows // tm) + N_EXPERTS
    padded = (counts + tm - 1) // tm * tm
    pend = jnp.cumsum(padded)
    pstart = pend - padded
    dest = jnp.sum(jnp.where(idx[:, :, None] == jnp.arange(N_EXPERTS)[None, None, :], pstart[None, None, :], 0),
                   axis=-1) + rank
    dest_flat = dest.reshape(-1).astype(jnp.int32)
    tile_start = jnp.arange(n_tiles, dtype=jnp.int32) * tm
    tile_valid = (tile_start < pend[-1]).astype(jnp.int32)
    last_start = jnp.maximum(pend[-1] - tm, 0)
    probe = jnp.minimum(tile_start, last_start)
    tile_expert = jnp.minimum(jnp.sum((pend[None, :] <= probe[:, None]).astype(jnp.int32), axis=1),
                              N_EXPERTS - 1)
    slots = moe_dispatch(xn, dest_flat, n_tiles * tm, _row_tile(t_rows, GATHER_TILE))
    y_slots = moe_experts(slots, tile_expert, tile_valid, wg, wu, wd, tm, MOE_FF_TILE)
    gates_col = gates.T
    y_prompt = moe_combine(x_prompt, gates_col, dest_flat, y_slots, g_final,
                           _row_tile(n_prompt, COMBINE_TILE), 0)
    y_sample = moe_combine(x_sample, gates_col, dest_flat, y_slots, g_final, n_sample, n_prompt)
    return y_prompt, y_sample


def _block_diag(w):
    h, i, j = w.shape
    eye = jnp.eye(h, dtype=w.dtype)
    return jnp.einsum("hij,hg->higj", w, eye).reshape(h * i, h * j)


def kernel(x_prompt, x_sample, state_lru_conv, state_lru_h, state_rwkv_shift, state_rwkv_wkv, state_ssm_conv, state_ssm, norm_mix, norm_ffn, norm_final, w_in0, lru_conv_w, lru_conv_b, lru_wa, lru_ba, lru_wx, lru_bx, lru_lambda, rwkv_mu, rwkv_w0, rwkv_w_decay_up, rwkv_a0, rwkv_w_iclr_up, rwkv_w_gate_up, rwkv_k_k, rwkv_k_a, rwkv_r_k, rwkv_ln_w, rwkv_ln_b, w_out0, ffn_wg, ffn_wu, ffn_wd, w_in1, ssm_conv_w, ssm_conv_b, ssm_dt_bias, ssm_a_log, ssm_d, ssm_norm_g, w_out1, router_w, router_b, moe_wg, moe_wu, moe_wd):
    nb, seq, d = x_prompt.shape
    ns = x_sample.shape[0]
    tp = nb * seq
    lru_w = lru_conv_w.shape[-1]
    rw_w = rwkv_w0.shape[-1]
    shift_cols = rwkv_mu.shape[-1]
    n_rheads = rw_w // RWKV_HEAD
    inner = ssm_norm_g.shape[-1]
    n_sheads = ssm_a_log.shape[-1]
    conv_ch = ssm_conv_w.shape[-1]

    xp, xs = x_prompt.reshape(tp, d), x_sample.reshape(ns, d)
    tmp, tms = (_row_tile(n, TOKEN_TILE, 2 * SUBLANES) for n in (tp, ns))

    row = lambda v: v.reshape(1, -1)
    lp = dict(cw=lru_conv_w[0], cb=row(lru_conv_b[0]), wa=_block_diag(lru_wa[0]).astype(BF16),
              ba=row(lru_ba[0]), wx=_block_diag(lru_wx[0]).astype(BF16), bx=row(lru_bx[0]),
              lam=row(lru_lambda[0]))
    hsum = _block_diag(jnp.ones((n_rheads, RWKV_HEAD, RWKV_HEAD), BF16))
    rp = dict(mu=row(rwkv_mu[0]), w0=row(rwkv_w0[0]), wdec=rwkv_w_decay_up[0], a0=row(rwkv_a0[0]),
              wiclr=rwkv_w_iclr_up[0], wgate=rwkv_w_gate_up[0], kk=row(rwkv_k_k[0]), ka=row(rwkv_k_a[0]),
              rk=row(rwkv_r_k[0]), lnw=row(rwkv_ln_w[0]), lnb=row(rwkv_ln_b[0]), hsum=hsum)
    mp = dict(cw=ssm_conv_w[0], cb=row(ssm_conv_b[0]), dtb=row(ssm_dt_bias[0]), alog=row(ssm_a_log[0]),
              dexp=row(jnp.repeat(ssm_d[0], SSM_HEAD)), ng=row(ssm_norm_g[0]))

    w_in0_rwkv = w_in0[0][:, 2 * lru_w:]

    def in0(x, tm):
        return (norm_matmul(x, norm_mix[0], w_in0[0], 0, 1, 2 * lru_w, tm, "in0_lru"),
                norm_matmul(x, norm_mix[0], w_in0_rwkv, 0, 2, shift_cols // 2, tm, "in0_rwkv"))

    proj_lru_p, proj_rwkv_p = in0(xp, tmp)
    proj_lru_s, s_p_rwkv = in0(xs, tms)
    zeros = lambda *shape: jnp.zeros(shape, F32)
    out_a_p, p_lru_conv, p_lru_h = lru_prompt(proj_lru_p, nb, seq, zeros(nb, 3, lru_w), zeros(nb, lru_w), lp)
    out_b_p, p_shift, p_wkv = rwkv_prompt(proj_rwkv_p, nb, seq, zeros(nb, shift_cols),
                                          zeros(nb, n_rheads, RWKV_HEAD, RWKV_HEAD), rp)
    out_a_s, s_lru_conv, s_lru_h = lru_step(proj_lru_s[:, :lru_w], proj_lru_s[:, lru_w:], state_lru_conv[0],
                                            state_lru_h[0], lp)
    out_b_s, s_wkv = rwkv_step(s_p_rwkv, state_rwkv_shift[0], state_rwkv_wkv[0], rp)

    def mix0_ffn(out_a, out_b, x, tm):
        x = matmul_residual([out_a, out_b], w_out0[0], x, tm, d, "out0")
        return ffn_residual(x, norm_ffn[0], ffn_wg[0], ffn_wu[0], ffn_wd[0], tm, FFN_FF_TILE)

    xp = mix0_ffn(out_a_p, out_b_p, xp, tmp)
    xs = mix0_ffn(out_a_s, out_b_s, xs, tms)

    tn1 = 1536
    w_in1_dt = w_in1[0][:, inner + conv_ch:]

    def in1(x, tm):
        return (norm_matmul(x, norm_mix[1], w_in1[0], 0, (inner + conv_ch) // tn1, tn1, tm, "in1_main", BF16),
                norm_matmul(x, norm_mix[1], w_in1_dt, 0, 1, n_sheads, tm, "in1_dt"))

    proj1_p, dt_p = in1(xp, tmp)
    proj1_s, dt_s = in1(xs, tms)
    y_mix_p, p_ssm_conv, p_ssm = ssd_prompt(proj1_p, dt_p, nb, seq, zeros(nb, 3, conv_ch),
                                            zeros(nb, n_sheads, SSM_HEAD, SSM_STATE), mp)
    y_mix_s, s_ssm_conv, s_ssm = ssd_step(proj1_s[:, :inner], proj1_s[:, inner:], dt_s,
                                          state_ssm_conv[0], state_ssm[0], mp)
    xp = matmul_residual([y_mix_p], w_out1[0], xp, tmp, 512, "out1")
    xs = matmul_residual([y_mix_s], w_out1[0], xs, tms, 512, "out1")
    y_p, y_s = moe_final(xp, xs, norm_ffn[1], norm_final, router_w[0], router_b[0], moe_wg[0], moe_wu[0],
                         moe_wd[0])

    return (y_p.reshape(nb, seq, d), y_s.reshape(ns, 1, d),
            p_lru_conv[None], p_lru_h.reshape(1, nb, lru_w), p_shift.reshape(1, nb, shift_cols), p_wkv[None],
            p_ssm_conv[None], p_ssm[None],
            s_lru_conv.reshape(1, ns, 3, lru_w), s_lru_h[None], s_p_rwkv[None], s_wkv[None],
            s_ssm_conv.reshape(1, ns, 3, conv_ch), s_ssm[None])
```

```python
import functools

import jax
import jax.numpy as jnp
from jax import lax
from jax.experimental import pallas as pl
from jax.experimental.pallas import tpu as pltpu

F32 = jnp.float32
BF16 = jnp.bfloat16
HIGHEST = lax.Precision.HIGHEST

NORM_EPS = 1e-6
CONV_W = 4
LRU_HEADS = 8
LRU_C = 8.0
RWKV_HEAD = 64
DECAY_RANK = 64
ICLR_RANK = 64
GATE_RANK = 128
RWKV_GN_EPS = 64e-5
SSM_HEAD = 64
SSM_GROUPS = 8
SSM_STATE = 128
SSM_CHUNK = 128
SSM_NORM_EPS = 1e-5
N_EXPERTS = 8

V7X_VMEM_BYTES = 64 * 1024 * 1024
VMEM_LIMIT = V7X_VMEM_BYTES - 8 * 1024 * 1024
SUBLANES = 8
LANES = 128

RWKV_CHUNK = 64
LRU_CHUNK = 256
MOE_TILE = 1024
MOE_FF_TILE = 512
FFN_FF_TILE = 512
TOKEN_TILE = 1024
ROUTER_TILE = 512
GATHER_TILE = 512
COMBINE_TILE = 512
DMA_UNROLL = 8


def _cparams(n_axes):
    return pltpu.CompilerParams(dimension_semantics=("arbitrary",) * n_axes,
                                vmem_limit_bytes=VMEM_LIMIT)


def _row_tile(n_rows, cap, mult=SUBLANES):
    best = None
    for t in range(mult, min(cap, n_rows) + 1, mult):
        if n_rows % t == 0:
            best = t
    assert best is not None, (n_rows, cap)
    return best


def _dot(a, b, precision=None):
    return jnp.dot(a, b, preferred_element_type=F32, precision=precision)


def _dot_nt(a, b, precision=None):
    return lax.dot_general(a, b, (((1,), (1,)), ((), ())), preferred_element_type=F32,
                           precision=precision)


def _dot_tn(a, b, precision=None):
    return lax.dot_general(a, b, (((0,), (0,)), ((), ())), preferred_element_type=F32,
                           precision=precision)


_NN = (((1,), (0,)), ((), ()))
_NT = (((1,), (1,)), ((), ()))
_TN = (((0,), (0,)), ((), ()))


def _split2(x):
    hi = x.astype(BF16)
    lo = (x - hi.astype(F32)).astype(BF16)
    return hi, lo


def _split3(x):
    hi = x.astype(BF16)
    r1 = x - hi.astype(F32)
    mid = r1.astype(BF16)
    lo = (r1 - mid.astype(F32)).astype(BF16)
    return hi, mid, lo


def _dg(a, b, dims):
    return lax.dot_general(a, b, dims, preferred_element_type=F32)


def _dot3(a2, b2, dims=_NN):
    (ah, al), (bh, bl) = a2, b2
    return _dg(ah, bh, dims) + _dg(al, bh, dims) + _dg(ah, bl, dims)


def _dot_exact_rhs(x, m_bf16, terms=3):
    return sum(_dg(part, m_bf16, _NN) for part in (_split3(x) if terms == 3 else _split2(x)))


def _dot_exact_lhs(m_bf16, x):
    return sum(_dg(m_bf16, part, _NN) for part in _split3(x))


def _softplus(x):
    return jnp.maximum(x, 0.0) + jnp.log1p(jnp.exp(-jnp.abs(x)))


def _silu(x):
    return x * jax.nn.sigmoid(x)


def _gelu_tanh(x):
    return 0.5 * x * (1.0 + jnp.tanh(0.7978845608028654 * (x + 0.044715 * (x * x * x))))


def _rms(x, g, eps):
    return x * lax.rsqrt(jnp.mean(x * x, axis=-1, keepdims=True) + eps) * g


def _norm_mm_kernel(x_ref, g_ref, w_ref, o_ref, xn_sc):
    @pl.when(pl.program_id(1) == 0)
    def _():
        xn_sc[...] = _rms(x_ref[...], g_ref[...], NORM_EPS).astype(BF16)

    o_ref[...] = _dot(xn_sc[...], w_ref[...].astype(BF16)).astype(o_ref.dtype)


def norm_matmul(x, g, w, col0_blk, n_blk, tn, tm, name, out_dtype=F32):
    t_rows, d = x.shape
    return pl.pallas_call(
        _norm_mm_kernel,
        grid=(t_rows // tm, n_blk),
        in_specs=[pl.BlockSpec((tm, d), lambda i, j: (i, 0)),
                  pl.BlockSpec((1, d), lambda i, j: (0, 0)),
                  pl.BlockSpec((d, tn), lambda i, j: (0, j + col0_blk))],
        out_specs=pl.BlockSpec((tm, tn), lambda i, j: (i, j)),
        out_shape=jax.ShapeDtypeStruct((t_rows, n_blk * tn), out_dtype),
        scratch_shapes=[pltpu.VMEM((tm, d), BF16)],
        compiler_params=_cparams(2),
        name=name,
    )(x, g.reshape(1, d), w)


def _mm_res_kernel(*refs, n_in):
    x_refs, w_refs = refs[:n_in], refs[n_in:2 * n_in]
    res_ref, o_ref = refs[2 * n_in], refs[2 * n_in + 1]
    acc = res_ref[...]
    for x_ref, w_ref in zip(x_refs, w_refs):
        acc = acc + _dot(x_ref[...].astype(BF16), w_ref[...].astype(BF16))
    o_ref[...] = acc


def matmul_residual(xs, w, res, tm, tn, name):
    n_in = len(xs)
    t_rows, kp = xs[0].shape
    n_cols = w.shape[1]
    in_specs = [pl.BlockSpec((tm, kp), lambda i, j: (i, 0)) for _ in xs]
    in_specs += [pl.BlockSpec((kp, tn), functools.partial(lambda i, j, p: (p, j), p=p))
                 for p in range(n_in)]
    in_specs += [pl.BlockSpec((tm, tn), lambda i, j: (i, j))]
    return pl.pallas_call(
        functools.partial(_mm_res_kernel, n_in=n_in),
        grid=(t_rows // tm, n_cols // tn),
        in_specs=in_specs,
        out_specs=pl.BlockSpec((tm, tn), lambda i, j: (i, j)),
        out_shape=jax.ShapeDtypeStruct((t_rows, n_cols), F32),
        compiler_params=_cparams(2),
        name=name,
    )(*xs, *([w] * n_in), res)


def _ffn_kernel(x_ref, g_ref, wg_ref, wu_ref, wd_ref, o_ref, xn_sc, acc_sc):
    j = pl.program_id(1)

    @pl.when(j == 0)
    def _():
        xn_sc[...] = _rms(x_ref[...], g_ref[...], NORM_EPS).astype(BF16)
        acc_sc[...] = jnp.zeros_like(acc_sc)

    xn = xn_sc[...]
    hg = _dot(xn, wg_ref[...].astype(BF16))
    hu = _dot(xn, wu_ref[...].astype(BF16))
    h = (_silu(hg) * hu).astype(BF16)
    acc_sc[...] += _dot(h, wd_ref[...].astype(BF16))

    @pl.when(j == pl.num_programs(1) - 1)
    def _():
        o_ref[...] = x_ref[...] + acc_sc[...]


def ffn_residual(x, g, wg, wu, wd, tm, tf):
    t_rows, d = x.shape
    d_ff = wg.shape[1]
    return pl.pallas_call(
        _ffn_kernel,
        grid=(t_rows // tm, d_ff // tf),
        in_specs=[pl.BlockSpec((tm, d), lambda i, j: (i, 0)),
                  pl.BlockSpec((1, d), lambda i, j: (0, 0)),
                  pl.BlockSpec((d, tf), lambda i, j: (0, j)),
                  pl.BlockSpec((d, tf), lambda i, j: (0, j)),
                  pl.BlockSpec((tf, d), lambda i, j: (j, 0))],
        out_specs=pl.BlockSpec((tm, d), lambda i, j: (i, 0)),
        out_shape=jax.ShapeDtypeStruct((t_rows, d), F32),
        scratch_shapes=[pltpu.VMEM((tm, d), BF16), pltpu.VMEM((tm, d), F32)],
        compiler_params=_cparams(2),
        name="ffn_swiglu",
    )(x, g.reshape(1, d), wg, wu, wd)


def _conv4(u, u1, u2, u3, cw, cb):
    return cb + cw[3:4] * u + cw[2:3] * u1 + cw[1:2] * u2 + cw[0:1] * u3


def _lru_gates(xc, wa, ba, wx, bx, lam):
    xb = xc.astype(BF16)
    r = jax.nn.sigmoid(_dot(xb, wa) + ba)
    i = jax.nn.sigmoid(_dot(xb, wx) + bx)
    log_a = -LRU_C * r * _softplus(-lam)
    a = jnp.exp(log_a)
    u = jnp.sqrt(1.0 - jnp.exp(2.0 * log_a)) * (i * xc)
    return a, u


def _lru_prompt_kernel(x_ref, g_ref, conv0_ref, h0_ref, cw_ref, cb_ref, wa_ref, ba_ref, wx_ref,
                       bx_ref, lam_ref, o_ref, nconv_ref, nh_ref, ext_sc, h_sc, *, lc):
    width = x_ref.shape[1]

    @pl.when(pl.program_id(1) == 0)
    def _():
        ext_sc[0:SUBLANES, :] = jnp.zeros((SUBLANES, width), F32)
        ext_sc[SUBLANES - 3:SUBLANES, :] = conv0_ref[0]
        h_sc[...] = h0_ref[0]

    u = x_ref[...]
    ext_sc[SUBLANES:SUBLANES + lc, :] = u
    xc = _conv4(u, ext_sc[SUBLANES - 1:SUBLANES - 1 + lc, :], ext_sc[SUBLANES - 2:SUBLANES - 2 + lc, :],
                ext_sc[SUBLANES - 3:SUBLANES - 3 + lc, :], cw_ref[...], cb_ref[...])
    tail = ext_sc[lc + SUBLANES - 3:lc + SUBLANES, :]
    ext_sc[SUBLANES - 3:SUBLANES, :] = tail
    nconv_ref[0] = tail

    a, h = _lru_gates(xc, wa_ref[...], ba_ref[...], wx_ref[...], bx_ref[...], lam_ref[...])
    row = lax.broadcasted_iota(jnp.int32, (lc, width), 0)
    s = 1
    while s < lc:
        keep = row >= s
        a_sh = jnp.where(keep, pltpu.roll(a, s, 0), 1.0)
        h_sh = jnp.where(keep, pltpu.roll(h, s, 0), 0.0)
        h = a * h_sh + h
        a = a * a_sh
        s *= 2
    hs = h + a * h_sc[...]
    h_last = hs[lc - 1:lc, :]
    h_sc[...] = h_last
    nh_ref[0] = h_last
    o_ref[...] = (hs * _gelu_tanh(g_ref[...])).astype(o_ref.dtype)


def lru_prompt(proj_lru, n_batch, seq, conv0, h0, lp):
    width = conv0.shape[-1]
    lc = min(LRU_CHUNK, seq)
    n_chunks = seq // lc
    full = lambda shape: pl.BlockSpec(shape, lambda b, c: (0,) * len(shape))
    return pl.pallas_call(
        functools.partial(_lru_prompt_kernel, lc=lc),
        grid=(n_batch, n_chunks),
        in_specs=[pl.BlockSpec((lc, width), lambda b, c: (b * n_chunks + c, 0)),
                  pl.BlockSpec((lc, width), lambda b, c: (b * n_chunks + c, 1)),
                  pl.BlockSpec((1, 3, width), lambda b, c: (b, 0, 0)),
                  pl.BlockSpec((1, 1, width), lambda b, c: (b, 0, 0)),
                  full((CONV_W, width)), full((1, width)), full((width, width)), full((1, width)),
                  full((width, width)), full((1, width)), full((1, width))],
        out_specs=[pl.BlockSpec((lc, width), lambda b, c: (b * n_chunks + c, 0)),
                   pl.BlockSpec((1, 3, width), lambda b, c: (b, 0, 0)),
                   pl.BlockSpec((1, 1, width), lambda b, c: (b, 0, 0))],
        out_shape=[jax.ShapeDtypeStruct((n_batch * seq, width), BF16),
                   jax.ShapeDtypeStruct((n_batch, 3, width), F32),
                   jax.ShapeDtypeStruct((n_batch, 1, width), F32)],
        scratch_shapes=[pltpu.VMEM((lc + SUBLANES, width), F32), pltpu.VMEM((1, width), F32)],
        compiler_params=_cparams(2),
        name="lru_prompt",
    )(proj_lru, proj_lru, conv0, h0.reshape(n_batch, 1, width), lp["cw"], lp["cb"], lp["wa"], lp["ba"],
      lp["wx"], lp["bx"], lp["lam"])


def _rows_call(body, inputs, out_shapes, name):
    return pl.pallas_call(body, out_shape=out_shapes, name=name,
                          compiler_params=pltpu.CompilerParams(vmem_limit_bytes=VMEM_LIMIT))(*inputs)


def _lru_step_kernel(x_ref, g_ref, buf_ref, h0_ref, cw_ref, cb_ref, wa_ref, ba_ref, wx_ref, bx_ref,
                     lam_ref, o_ref, nbuf_ref, nh_ref):
    width = x_ref.shape[1]
    u = x_ref[...]
    b0, b1, b2 = (buf_ref[:, k * width:(k + 1) * width] for k in range(3))
    xc = _conv4(u, b2, b1, b0, cw_ref[...], cb_ref[...])
    a, uu = _lru_gates(xc, wa_ref[...], ba_ref[...], wx_ref[...], bx_ref[...], lam_ref[...])
    h = a * h0_ref[...] + uu
    nh_ref[...] = h
    o_ref[...] = (h * _gelu_tanh(g_ref[...])).astype(o_ref.dtype)
    nbuf_ref[:, 0:width] = b1
    nbuf_ref[:, width:2 * width] = b2
    nbuf_ref[:, 2 * width:3 * width] = u


def lru_step(x_lru, g_lru, conv_buf, h0, lp):
    n, width = x_lru.shape
    return _rows_call(
        _lru_step_kernel,
        [x_lru, g_lru, conv_buf.reshape(n, 3 * width), h0, lp["cw"], lp["cb"], lp["wa"], lp["ba"],
         lp["wx"], lp["bx"], lp["lam"]],
        [jax.ShapeDtypeStruct((n, width), BF16), jax.ShapeDtypeStruct((n, 3 * width), F32),
         jax.ShapeDtypeStruct((n, width), F32)],
        "lru_step")


def _rwkv_rows(mixed, rp):
    w = rp["w0"].shape[1]
    r, k, v = mixed[:, 0:w], mixed[:, w:2 * w], mixed[:, 2 * w:3 * w]
    o = 3 * w
    wd = mixed[:, o:o + DECAY_RANK]
    ad = mixed[:, o + DECAY_RANK:o + DECAY_RANK + ICLR_RANK]
    gd = mixed[:, o + DECAY_RANK + ICLR_RANK:o + DECAY_RANK + ICLR_RANK + GATE_RANK]
    dec_in = rp["w0"] + _dot(jnp.tanh(wd).astype(BF16), rp["wdec"].astype(BF16))
    w_log = -_softplus(-dec_in) - 0.5
    lw = -jnp.exp(w_log)
    iclr = jax.nn.sigmoid(rp["a0"] + _dot(ad.astype(BF16), rp["wiclr"].astype(BF16)))
    gate = _dot(jax.nn.sigmoid(gd).astype(BF16), rp["wgate"].astype(BF16))
    kk = k * rp["kk"]
    ss = _dot_exact_rhs(kk * kk, rp["hsum"])
    kkn = kk / jnp.maximum(jnp.sqrt(ss), 1e-12)
    k2 = k * (1.0 + (iclr - 1.0) * rp["ka"])
    return r, k2, v, lw, -kkn, kkn * iclr, gate


def _rwkv_post(o, r, k2, v, gate, rp):
    inv = 1.0 / RWKV_HEAD
    mu = _dot_exact_rhs(o, rp["hsum"], 2) * inv
    d = o - mu
    var = _dot_exact_rhs(d * d, rp["hsum"], 2) * inv
    on = d * lax.rsqrt(var + RWKV_GN_EPS) * rp["lnw"] + rp["lnb"]
    bonus = _dot_exact_rhs(r * k2 * rp["rk"], rp["hsum"], 2) * v
    return (on + bonus) * gate


_RWKV_PARAM_NAMES = ("mu", "w0", "wdec", "a0", "wiclr", "wgate", "kk", "ka", "rk", "lnw", "lnb", "hsum")


def _rwkv_prompt_kernel(*refs, chunk, nbs):
    p_refs, shift0_ref, s0_ref, rest = refs[:nbs], refs[nbs], refs[nbs + 1], refs[nbs + 2:]
    n_prm = len(_RWKV_PARAM_NAMES)
    rp = {name: ref[...] for name, ref in zip(_RWKV_PARAM_NAMES, rest[:n_prm])}
    o_ref, nshift_ref, nwkv_ref, prev_sc, s_sc, o_sc = rest[n_prm:]
    n_pairs = s_sc.shape[0] // nbs
    hd = RWKV_HEAD
    assert chunk == hd, "the pair-packed layout below uses chunk == head size"

    @pl.when(pl.program_id(1) == 0)
    def _():
        for i in range(nbs):
            prev_sc[i] = shift0_ref[i]
            for p in range(n_pairs):
                s_sc[i * n_pairs + p] = jnp.concatenate([s0_ref[i, 2 * p], s0_ref[i, 2 * p + 1]], axis=0).T

    ti = lax.broadcasted_iota(jnp.int32, (chunk, chunk), 0)
    si = lax.broadcasted_iota(jnp.int32, (chunk, chunk), 1)
    lower = (ti >= si).astype(BF16)

    def prep(i):
        p = p_refs[i][...]
        row = lax.broadcasted_iota(jnp.int32, p.shape, 0)
        p_prev = jnp.where(row >= 1, pltpu.roll(p, 1, 0), prev_sc[i])
        last = p[chunk - 1:chunk, :]
        prev_sc[i] = last
        nshift_ref[i] = last
        mixed = p + (p_prev - p) * rp["mu"]
        r, k2, v, lw, a, b, gate = _rwkv_rows(mixed, rp)
        cs = _dot_exact_lhs(lower, lw)
        g_in, g_ex, g_inv = jnp.exp(cs), jnp.exp(cs - lw), jnp.exp(-cs)
        bt, kt = b * g_inv, k2 * g_inv
        g_end = g_in[chunk - 1:chunk, :]
        return dict(r=r, k2=k2, v=v, gate=gate, at=a * g_ex, bt=bt, kt=kt, rt=r * g_in, g_end=g_end,
                    bc=bt * g_end, kc=kt * g_end)

    preps = [prep(i) for i in range(nbs)]
    n_fac = max(1, (chunk - 1).bit_length())
    pw_ = 2 * hd
    lane1 = lax.broadcasted_iota(jnp.int32, (1, pw_), 1)
    lo1 = lane1 < hd
    lo2 = jnp.concatenate([lo1, lo1], axis=1)
    ti2 = lax.broadcasted_iota(jnp.int32, (chunk, pw_), 0)
    si2 = lax.broadcasted_iota(jnp.int32, (chunk, pw_), 1) % hd
    strict2, incl2 = ti2 > si2, ti2 >= si2

    def bdiag(x, lo):
        return jnp.concatenate([jnp.where(lo, x, 0), jnp.where(lo, 0, x)], axis=0)

    def bdiag2(x2, lo):
        return bdiag(x2[0], lo), bdiag(x2[1], lo)

    units = [(i, q) for i in range(nbs) for q in range(n_pairs)]
    pairs = range(len(units))
    pls = [slice(q * pw_, (q + 1) * pw_) for _, q in units]
    a_p, r_p, v_p, bt_p, kt_p, bc_p, kc_p, ge_p = (
        [preps[i][name][:, pls[u]] for u, (i, _) in enumerate(units)]
        for name in ("at", "rt", "v", "bt", "kt", "bc", "kc", "g_end"))
    mas, mrs, v2s = [], [], []
    for p in pairs:
        b2, k2_ = _split2(bt_p[p]), _split2(kt_p[p])
        rhs_rows = tuple(jnp.concatenate([bdiag(b2[i], lo1), bdiag(k2_[i], lo1)], axis=0) for i in range(2))
        mas.append(_dot3(_split2(a_p[p]), rhs_rows, _NT))
        mrs.append(_dg(r_p[p].astype(BF16), rhs_rows[0], _NT))
        v2s.append(bdiag2(_split2(v_p[p]), lo1))
    labs = [jnp.where(strict2, m[:, :pw_], 0.0) for m in mas]
    lrbs = [jnp.where(incl2, m[:, :pw_], 0.0).astype(BF16) for m in mrs]
    lakvs = [_dot3(_split2(jnp.where(strict2, mas[p][:, pw_:], 0.0)), v2s[p]) for p in pairs]
    lrkvs = [_dg(jnp.where(incl2, mrs[p][:, pw_:], 0.0).astype(BF16), v2s[p][0], _NN) for p in pairs]
    ys = [jnp.concatenate([a_p[p], lakvs[p]], axis=1) for p in pairs]
    pws = [_split2(lab) for lab in labs]
    for f in range(n_fac):
        ys = [ys[p] + _dot3(pws[p], bdiag2(_split2(ys[p]), lo2)) for p in pairs]
        if f + 1 < n_fac:
            pws = [_split2(_dot3(pw, bdiag2(pw, lo1))) for pw in pws]
    y2s = [_split2(y) for y in ys]
    qos = [jnp.concatenate([r_p[p], lrkvs[p]], axis=1) + _dg(lrbs[p], bdiag(y2s[p][0], lo2), _NN)
           for p in pairs]
    xs_ = [_dot3(_split2(bc_p[p]), y2s[p], _TN) for p in pairs]
    kvs = [_dot3(_split2(kc_p[p]), _split2(v_p[p]), _TN) for p in pairs]
    mts = [_split2(jnp.where(lo1, xs_[p][:hd, :pw_], xs_[p][hd:, :pw_]) + jnp.where(ti2 == si2, ge_p[p], 0.0))
           for p in pairs]
    n0s = [jnp.where(lo1, xs_[p][:hd, pw_:] + kvs[p][:hd], xs_[p][hd:, pw_:] + kvs[p][hd:]) for p in pairs]
    s0s = [bdiag2(_split2(s_sc[p]), lo1) for p in pairs]
    for p, (i, _) in enumerate(units):
        o_sc[i, :, pls[p]] = _dg(qos[p][:, :pw_].astype(BF16), s0s[p][0], _NN) + qos[p][:, pw_:]
    for p in pairs:
        s_sc[p] = _dot3(mts[p], s0s[p]) + n0s[p]

    @pl.when(pl.program_id(1) == pl.num_programs(1) - 1)
    def _():
        for p, (i, q) in enumerate(units):
            s_pair = s_sc[p].T
            nwkv_ref[i, 2 * q] = s_pair[:hd]
            nwkv_ref[i, 2 * q + 1] = s_pair[hd:]

    for i, pre in enumerate(preps):
        o_ref[i] = _rwkv_post(o_sc[i], pre["r"], pre["k2"], pre["v"], pre["gate"], rp).astype(o_ref.dtype)


def _rwkv_param_list(rp):
    return [rp[name] for name in _RWKV_PARAM_NAMES]


def rwkv_prompt(p_rwkv, n_batch, seq, shift0, wkv0, rp):
    cols = shift0.shape[-1]
    n_heads, hd = wkv0.shape[1], wkv0.shape[2]
    width = n_heads * hd
    chunk = min(RWKV_CHUNK, seq)
    n_chunks = seq // chunk
    prm = _rwkv_param_list(rp)
    prm_specs = [pl.BlockSpec(x.shape, lambda b, c: (0, 0)) for x in prm]
    nbs = 2 if n_batch % 2 == 0 else 1
    p_specs = [pl.BlockSpec((chunk, cols), functools.partial(
        lambda b, c, i: ((b * nbs + i) * n_chunks + c, 0), i=i)) for i in range(nbs)]
    out, new_shift, new_wkv = pl.pallas_call(
        functools.partial(_rwkv_prompt_kernel, chunk=chunk, nbs=nbs),
        grid=(n_batch // nbs, n_chunks),
        in_specs=p_specs + [pl.BlockSpec((nbs, 1, cols), lambda b, c: (b, 0, 0)),
                            pl.BlockSpec((nbs, n_heads, hd, hd), lambda b, c: (b, 0, 0, 0))] + prm_specs,
        out_specs=[pl.BlockSpec((nbs, chunk, width), lambda b, c: (b, c, 0)),
                   pl.BlockSpec((nbs, 1, cols), lambda b, c: (b, 0, 0)),
                   pl.BlockSpec((nbs, n_heads, hd, hd), lambda b, c: (b, 0, 0, 0))],
        out_shape=[jax.ShapeDtypeStruct((n_batch, seq, width), BF16),
                   jax.ShapeDtypeStruct((n_batch, 1, cols), F32),
                   jax.ShapeDtypeStruct((n_batch, n_heads, hd, hd), F32)],
        scratch_shapes=[pltpu.VMEM((nbs, 1, cols), F32), pltpu.VMEM((nbs * n_heads // 2, hd, 2 * hd), F32),
                        pltpu.VMEM((nbs, chunk, width), F32)],
        compiler_params=_cparams(2),
        name="rwkv_prompt",
    )(*([p_rwkv] * nbs), shift0.reshape(n_batch, 1, cols), wkv0, *prm)
    return out.reshape(n_batch * seq, width), new_shift, new_wkv


def _rwkv_step_pre_kernel(p_ref, prev_ref, *rest):
    n_prm = len(_RWKV_PARAM_NAMES)
    rp = {name: ref[...] for name, ref in zip(_RWKV_PARAM_NAMES, rest[:n_prm])}
    r_ref, k_ref, v_ref, w_ref, a_ref, b_ref, gate_ref = rest[n_prm:]
    p = p_ref[...]
    mixed = p + (prev_ref[...] - p) * rp["mu"]
    r, k2, v, lw, a, b, gate = _rwkv_rows(mixed, rp)
    r_ref[...] = r
    k_ref[...] = k2
    v_ref[...] = v
    w_ref[...] = jnp.exp(lw)
    a_ref[...] = a
    b_ref[...] = b
    gate_ref[...] = gate


def _rwkv_step_kernel(s_ref, w_ref, a_ref, b_ref, k_ref, r_ref, vt_ref, ns_ref, ot_ref):
    bt, n_heads = s_ref.shape[0], s_ref.shape[1]
    lane = lax.broadcasted_iota(jnp.int32, ot_ref.shape[1:], 1)
    for i in range(bt):
        vt = vt_ref[i]
        ss = [s_ref[i, h] for h in range(n_heads)]
        sas = [jnp.sum(ss[h] * a_ref[i, h], axis=-1, keepdims=True) for h in range(n_heads)]
        s_news = [ss[h] * w_ref[i, h] + sas[h] * b_ref[i, h] + vt[:, h:h + 1] * k_ref[i, h]
                  for h in range(n_heads)]
        for h in range(n_heads):
            ns_ref[i, h] = s_news[h]
        os_ = [jnp.sum(s_news[h] * r_ref[i, h], axis=-1, keepdims=True) for h in range(n_heads)]
        ot = jnp.zeros(ot_ref.shape[1:], F32)
        for h in range(n_heads):
            ot = jnp.where(lane == h, os_[h], ot)
        ot_ref[i] = ot


def _rwkv_step_post_kernel(o_ref, r_ref, k_ref, v_ref, gate_ref, *rest):
    n_prm = len(_RWKV_PARAM_NAMES)
    rp = {name: ref[...] for name, ref in zip(_RWKV_PARAM_NAMES, rest[:n_prm])}
    out_ref = rest[n_prm]
    out_ref[...] = _rwkv_post(o_ref[...], r_ref[...], k_ref[...], v_ref[...], gate_ref[...],
                              rp).astype(out_ref.dtype)


def rwkv_step(p_rwkv, shift_prev, wkv0, rp):
    n = p_rwkv.shape[0]
    n_heads, hd = wkv0.shape[1], wkv0.shape[2]
    width = n_heads * hd
    prm = _rwkv_param_list(rp)
    row = jax.ShapeDtypeStruct((n, width), F32)
    r, k2, v, w, a, b, gate = _rows_call(_rwkv_step_pre_kernel, [p_rwkv, shift_prev, *prm], [row] * 7,
                                         "rwkv_step_pre")
    hrow = lambda z: z.reshape(n, n_heads, 1, hd)
    vt = jnp.transpose(v.reshape(n, n_heads, hd), (0, 2, 1))
    bt = SUBLANES
    vec_spec = pl.BlockSpec((bt, n_heads, 1, hd), lambda i: (i, 0, 0, 0))
    st_spec = pl.BlockSpec((bt, n_heads, hd, hd), lambda i: (i, 0, 0, 0))
    t_spec = pl.BlockSpec((bt, hd, n_heads), lambda i: (i, 0, 0))
    new_wkv, ot = pl.pallas_call(
        _rwkv_step_kernel,
        grid=(n // bt,),
        in_specs=[st_spec] + [vec_spec] * 5 + [t_spec],
        out_specs=[st_spec, t_spec],
        out_shape=[jax.ShapeDtypeStruct(wkv0.shape, F32), jax.ShapeDtypeStruct((n, hd, n_heads), F32)],
        compiler_params=_cparams(1),
        name="rwkv_step",
    )(wkv0, hrow(w), hrow(a), hrow(b), hrow(k2), hrow(r), vt)
    o = jnp.transpose(ot, (0, 2, 1)).reshape(n, width)
    (out_b,) = _rows_call(_rwkv_step_post_kernel, [o, r, k2, v, gate, *prm],
                          [jax.ShapeDtypeStruct((n, width), BF16)], "rwkv_step_post")
    return out_b, new_wkv


def _mamba_post(y, xs, z, dexp, ng):
    y = (y + dexp * xs) * _silu(z)
    gw = y.shape[1] // SSM_GROUPS
    parts = []
    for g in range(SSM_GROUPS):
        yg = y[:, g * gw:(g + 1) * gw]
        parts.append(yg * lax.rsqrt(jnp.mean(yg * yg, axis=-1, keepdims=True) + SSM_NORM_EPS))
    return jnp.concatenate(parts, axis=1) * ng


def _ssd_prompt_kernel(z_ref, xlo_ref, xhi_ref, dt_ref, conv0_ref, s0_ref, cw_ref, cb_ref, dtb_ref,
                       alog_ref, dexp_ref, ng_ref, hexp_ref, qexp_ref, y_ref, nconv_ref, nssm_ref,
                       ext_sc, s_sc, y_sc, yo_sc, *, q):
    inner = xlo_ref.shape[1]
    n_heads = s_sc.shape[0]
    hpg = n_heads // SSM_GROUPS

    @pl.when(pl.program_id(1) == 0)
    def _():
        ext_sc[0:SUBLANES, :] = jnp.zeros((SUBLANES, ext_sc.shape[1]), F32)
        ext_sc[SUBLANES - 3:SUBLANES, :] = conv0_ref[0]
        s_sc[...] = s0_ref[0]

    ext_sc[SUBLANES:SUBLANES + q, 0:inner] = xlo_ref[...].astype(F32)
    ext_sc[SUBLANES:SUBLANES + q, inner:] = xhi_ref[...].astype(F32)
    xbc = _silu(_conv4(ext_sc[SUBLANES:SUBLANES + q, :], ext_sc[SUBLANES - 1:SUBLANES - 1 + q, :],
                       ext_sc[SUBLANES - 2:SUBLANES - 2 + q, :], ext_sc[SUBLANES - 3:SUBLANES - 3 + q, :],
                       cw_ref[...], cb_ref[...]))
    tail = ext_sc[q + SUBLANES - 3:q + SUBLANES, :]
    ext_sc[SUBLANES - 3:SUBLANES, :] = tail
    nconv_ref[0] = tail

    xs = xbc[:, 0:inner]
    gn = SSM_GROUPS * SSM_STATE
    bm = xbc[:, inner:inner + gn].astype(BF16)
    cm = xbc[:, inner + gn:].astype(BF16)
    dt = _softplus(dt_ref[...] + dtb_ref[...])
    dta = dt * (-jnp.exp(alog_ref[...]))
    ti = lax.broadcasted_iota(jnp.int32, (q, q), 0)
    si = lax.broadcasted_iota(jnp.int32, (q, q), 1)
    causal = ti >= si
    da = _dot_exact_lhs(causal.astype(BF16), dta)
    upper = (ti <= si).astype(BF16)
    da_t = sum(_dg(part, upper, _TN) for part in _split3(dta))
    da_end = da[q - 1:q, :]
    end_decay = jnp.exp(da_end)
    hexp, qexp = hexp_ref[...], qexp_ref[...]
    xdt = xs * _dot_exact_rhs(dt, hexp, 2)
    xdt_b = xdt.astype(BF16)
    xdt_end = (xdt * _dot_exact_rhs(jnp.exp(da_end - da), hexp, 2)).astype(BF16)
    from_start = _dot_exact_rhs(jnp.exp(da), hexp, 2)
    da_col = _dot_exact_rhs(da, qexp)

    heads = range(n_heads)
    bgs = [bm[:, g * SSM_STATE:(g + 1) * SSM_STATE] for g in range(SSM_GROUPS)]
    cgs = [cm[:, g * SSM_STATE:(g + 1) * SSM_STATE] for g in range(SSM_GROUPS)]
    scores = [_dot_nt(cgs[g], bgs[g]) for g in range(SSM_GROUPS)]
    hss = [slice(h * SSM_HEAD, (h + 1) * SSM_HEAD) for h in heads]
    s_hs = [s_sc[h] for h in heads]
    for h in heads:
        yo_sc[:, hss[h]] = _dot_nt(cgs[h // hpg], s_hs[h].astype(BF16))
    wts = [(scores[h // hpg]
            * jnp.exp(jnp.where(causal, da_col[:, h * q:(h + 1) * q] - da_t[h:h + 1, :], -jnp.inf))
            ).astype(BF16) for h in heads]
    for h in heads:
        y_sc[:, hss[h]] = _dot(wts[h], xdt_b[:, hss[h]])
    upd = [_dot_tn(xdt_end[:, hss[h]], bgs[h // hpg]) for h in heads]
    for h in heads:
        s_sc[h] = s_hs[h] * end_decay[:, h:h + 1] + upd[h]

    @pl.when(pl.program_id(1) == pl.num_programs(1) - 1)
    def _():
        nssm_ref[0] = s_sc[...]

    y = y_sc[...] + yo_sc[...] * from_start
    y_ref[...] = _mamba_post(y, xs, z_ref[...].astype(F32), dexp_ref[...], ng_ref[...]).astype(y_ref.dtype)


def ssd_prompt(proj1, dt_raw, n_batch, seq, conv0, ssm0, mp):
    n_heads, hd, n_state = ssm0.shape[1:]
    inner = n_heads * hd
    conv_ch = conv0.shape[-1]
    q = SSM_CHUNK if seq % SSM_CHUNK == 0 else seq
    n_chunks = seq // q
    full = lambda shape: pl.BlockSpec(shape, lambda b, c: (0,) * len(shape))
    return pl.pallas_call(
        functools.partial(_ssd_prompt_kernel, q=q),
        grid=(n_batch, n_chunks),
        in_specs=[pl.BlockSpec((q, inner), lambda b, c: (b * n_chunks + c, 0)),
                  pl.BlockSpec((q, inner), lambda b, c: (b * n_chunks + c, 1)),
                  pl.BlockSpec((q, inner), lambda b, c: (b * n_chunks + c, 2)),
                  pl.BlockSpec((q, n_heads), lambda b, c: (b * n_chunks + c, 0)),
                  pl.BlockSpec((1, 3, conv_ch), lambda b, c: (b, 0, 0)),
                  pl.BlockSpec((1, n_heads, hd, n_state), lambda b, c: (b, 0, 0, 0)),
                  full((CONV_W, conv_ch)), full((1, conv_ch)), full((1, n_heads)), full((1, n_heads)),
                  full((1, inner)), full((1, inner)), full((n_heads, inner)), full((n_heads, n_heads * q))],
        out_specs=[pl.BlockSpec((q, inner), lambda b, c: (b * n_chunks + c, 0)),
                   pl.BlockSpec((1, 3, conv_ch), lambda b, c: (b, 0, 0)),
                   pl.BlockSpec((1, n_heads, hd, n_state), lambda b, c: (b, 0, 0, 0))],
        out_shape=[jax.ShapeDtypeStruct((n_batch * seq, inner), BF16),
                   jax.ShapeDtypeStruct((n_batch, 3, conv_ch), F32),
                   jax.ShapeDtypeStruct(ssm0.shape, F32)],
        scratch_shapes=[pltpu.VMEM((q + SUBLANES, conv_ch), F32), pltpu.VMEM((n_heads, hd, n_state), F32),
                        pltpu.VMEM((q, inner), F32), pltpu.VMEM((q, inner), F32)],
        compiler_params=_cparams(2),
        name="ssd_prompt",
    )(proj1, proj1, proj1, dt_raw, conv0, ssm0, mp["cw"], mp["cb"], mp["dtb"], mp["alog"], mp["dexp"],
      mp["ng"], jnp.repeat(jnp.eye(n_heads, dtype=BF16), hd, axis=1),
      jnp.repeat(jnp.eye(n_heads, dtype=BF16), q, axis=1))


def _ssd_step_pre_kernel(x_ref, dt_ref, buf_ref, cw_ref, cb_ref, dtb_ref, alog_ref,
                         xs_ref, b_ref, c_ref, dt_out_ref, dec_ref, nbuf_ref):
    ch = x_ref.shape[1]
    inner = xs_ref.shape[1]
    gn = b_ref.shape[1]
    u = x_ref[...].astype(F32)
    b0, b1, b2 = (buf_ref[:, k * ch:(k + 1) * ch] for k in range(3))
    xbc = _silu(_conv4(u, b2, b1, b0, cw_ref[...], cb_ref[...]))
    xs_ref[...] = xbc[:, 0:inner]
    b_ref[...] = xbc[:, inner:inner + gn]
    c_ref[...] = xbc[:, inner + gn:]
    dt = _softplus(dt_ref[...] + dtb_ref[...])
    dt_out_ref[...] = dt
    dec_ref[...] = jnp.exp(dt * (-jnp.exp(alog_ref[...])))
    nbuf_ref[:, 0:ch] = b1
    nbuf_ref[:, ch:2 * ch] = b2
    nbuf_ref[:, 2 * ch:3 * ch] = u


def _ssd_step_kernel(s_ref, xt_ref, b_ref, c_ref, dt_ref, dec_ref, ns_ref, yt_ref):
    bt, n_heads = s_ref.shape[0], s_ref.shape[1]
    hpg = n_heads // SSM_GROUPS
    lane = lax.broadcasted_iota(jnp.int32, yt_ref.shape[1:], 1)
    for i in range(bt):
        xt = xt_ref[i]
        dt = dt_ref[i]
        dec = dec_ref[i]
        xdt = xt * dt
        s_news = [s_ref[i, h] * dec[:, h:h + 1] + xdt[:, h:h + 1] * b_ref[i, h // hpg]
                  for h in range(n_heads)]
        for h in range(n_heads):
            ns_ref[i, h] = s_news[h]
        ys = [jnp.sum(s_news[h] * c_ref[i, h // hpg], axis=-1, keepdims=True) for h in range(n_heads)]
        yt = jnp.zeros(yt_ref.shape[1:], F32)
        for h in range(n_heads):
            yt = jnp.where(lane == h, ys[h], yt)
        yt_ref[i] = yt


def _ssd_step_post_kernel(y_ref, xs_ref, z_ref, dexp_ref, ng_ref, o_ref):
    o_ref[...] = _mamba_post(y_ref[...], xs_ref[...], z_ref[...].astype(F32), dexp_ref[...],
                             ng_ref[...]).astype(o_ref.dtype)


def ssd_step(z, xbc_raw, dt_raw, conv_buf, ssm0, mp):
    n, conv_ch = xbc_raw.shape
    n_heads, hd, n_state = ssm0.shape[1:]
    inner = n_heads * hd
    gn = SSM_GROUPS * n_state
    xs, bm, cm, dt, dec, nbuf = _rows_call(
        _ssd_step_pre_kernel,
        [xbc_raw, dt_raw, conv_buf.reshape(n, 3 * conv_ch), mp["cw"], mp["cb"], mp["dtb"], mp["alog"]],
        [jax.ShapeDtypeStruct((n, inner), F32), jax.ShapeDtypeStruct((n, gn), F32),
         jax.ShapeDtypeStruct((n, gn), F32), jax.ShapeDtypeStruct((n, n_heads), F32),
         jax.ShapeDtypeStruct((n, n_heads), F32), jax.ShapeDtypeStruct((n, 3 * conv_ch), F32)],
        "ssd_step_pre")
    xt = jnp.transpose(xs.reshape(n, n_heads, hd), (0, 2, 1))
    bt = 4
    st_spec = pl.BlockSpec((bt, n_heads, hd, n_state), lambda i: (i, 0, 0, 0))
    t_spec = pl.BlockSpec((bt, hd, n_heads), lambda i: (i, 0, 0))
    g_spec = pl.BlockSpec((bt, SSM_GROUPS, 1, n_state), lambda i: (i, 0, 0, 0))
    h_spec = pl.BlockSpec((bt, 1, n_heads), lambda i: (i, 0, 0))
    new_ssm, yt = pl.pallas_call(
        _ssd_step_kernel,
        grid=(n // bt,),
        in_specs=[st_spec, t_spec, g_spec, g_spec, h_spec, h_spec],
        out_specs=[st_spec, t_spec],
        out_shape=[jax.ShapeDtypeStruct(ssm0.shape, F32), jax.ShapeDtypeStruct((n, hd, n_heads), F32)],
        compiler_params=_cparams(1),
        name="ssd_step",
    )(ssm0, xt, bm.reshape(n, SSM_GROUPS, 1, n_state), cm.reshape(n, SSM_GROUPS, 1, n_state),
      dt.reshape(n, 1, n_heads), dec.reshape(n, 1, n_heads))
    y = jnp.transpose(yt, (0, 2, 1)).reshape(n, inner)
    (y,) = _rows_call(_ssd_step_post_kernel, [y, xs, z, mp["dexp"], mp["ng"]],
                      [jax.ShapeDtypeStruct((n, inner), BF16)], "ssd_step_post")
    return y, nbuf, new_ssm


def _router_kernel(x_ref, cnt0_ref, g_ref, rwt_ref, rb_ref, xn_ref, idx_ref, gate_ref, rank_ref, cnt_ref,
                   cnt_sc):
    tm = x_ref.shape[0]

    @pl.when(pl.program_id(0) == 0)
    def _():
        cnt_sc[...] = cnt0_ref[...].astype(F32)

    xn = _rms(x_ref[...], g_ref[...], NORM_EPS)
    xn_ref[...] = xn
    logits = _dot_nt(rwt_ref[...], xn, HIGHEST) + rb_ref[...]
    e_iota = lax.broadcasted_iota(jnp.int32, logits.shape, 0)
    m1 = jnp.max(logits, axis=0, keepdims=True)
    i1 = jnp.min(jnp.where(logits == m1, e_iota, N_EXPERTS), axis=0, keepdims=True)
    rest = jnp.where(e_iota == i1, -jnp.inf, logits)
    m2 = jnp.max(rest, axis=0, keepdims=True)
    i2 = jnp.min(jnp.where(rest == m2, e_iota, N_EXPERTS), axis=0, keepdims=True)
    e2 = jnp.exp(m2 - m1)
    denom = 1.0 + e2
    idx_ref[...] = jnp.concatenate([i1, i2], axis=0)
    gate_ref[...] = jnp.concatenate([1.0 / denom, e2 / denom], axis=0)
    oh1 = (e_iota == i1).astype(F32)
    oh2 = (e_iota == i2).astype(F32)
    oh = oh1 + oh2
    ti = lax.broadcasted_iota(jnp.int32, (tm, tm), 0)
    si = lax.broadcasted_iota(jnp.int32, (tm, tm), 1)
    before = _dot(oh.astype(BF16), (ti < si).astype(BF16)) + cnt_sc[:, 0:1]
    rank_ref[...] = jnp.concatenate(
        [jnp.sum(oh1 * before, axis=0, keepdims=True), jnp.sum(oh2 * before, axis=0, keepdims=True)],
        axis=0).astype(jnp.int32)
    cnt_sc[...] = cnt_sc[...] + jnp.sum(oh, axis=1, keepdims=True)
    cnt_ref[...] = cnt_sc[...].astype(jnp.int32)


def moe_router(x, counts0, g, router_w, router_b, tm):
    t_rows, d = x.shape
    return pl.pallas_call(
        _router_kernel,
        grid=(t_rows // tm,),
        in_specs=[pl.BlockSpec((tm, d), lambda i: (i, 0)),
                  pl.BlockSpec((N_EXPERTS, LANES), lambda i: (0, 0)),
                  pl.BlockSpec((1, d), lambda i: (0, 0)),
                  pl.BlockSpec((N_EXPERTS, d), lambda i: (0, 0)),
                  pl.BlockSpec((N_EXPERTS, 1), lambda i: (0, 0))],
        out_specs=[pl.BlockSpec((tm, d), lambda i: (i, 0)),
                   pl.BlockSpec((2, tm), lambda i: (0, i)),
                   pl.BlockSpec((2, tm), lambda i: (0, i)),
                   pl.BlockSpec((2, tm), lambda i: (0, i)),
                   pl.BlockSpec((N_EXPERTS, LANES), lambda i: (0, 0))],
        out_shape=[jax.ShapeDtypeStruct((t_rows, d), F32),
                   jax.ShapeDtypeStruct((2, t_rows), jnp.int32),
                   jax.ShapeDtypeStruct((2, t_rows), F32),
                   jax.ShapeDtypeStruct((2, t_rows), jnp.int32),
                   jax.ShapeDtypeStruct((N_EXPERTS, LANES), jnp.int32)],
        scratch_shapes=[pltpu.VMEM((N_EXPERTS, LANES), F32)],
        compiler_params=_cparams(1),
        name="moe_router",
    )(x, counts0, g.reshape(1, d), router_w.T, router_b.reshape(N_EXPERTS, 1))


def _row_copy(src_ref, src_row, dst_ref, dst_row, sem):
    return pltpu.make_async_copy(src_ref.at[pl.ds(src_row, 1), :], dst_ref.at[pl.ds(dst_row, 1), :], sem)


def _dispatch_kernel(dest_ref, x_ref, slots_in_ref, slots_ref, sem, *, t_rows, row0):
    del slots_in_ref
    tm = x_ref.shape[0]
    base = row0 + pl.program_id(0) * tm

    def copies(r):
        return [_row_copy(x_ref, r, slots_ref, dest_ref[k * t_rows + base + r], sem) for k in range(2)]

    def start(r, carry):
        for k, cp in enumerate(copies(r)):
            cp.start(priority=k)
        return carry

    def wait(r, carry):
        for cp in copies(r):
            cp.wait()
        return carry

    lax.fori_loop(0, tm, start, 0, unroll=DMA_UNROLL)
    lax.fori_loop(0, tm, wait, 0, unroll=DMA_UNROLL)


def moe_dispatch(xn, dest_flat, slots, row0, tm):
    n_rows, d = xn.shape
    return pl.pallas_call(
        functools.partial(_dispatch_kernel, t_rows=dest_flat.shape[0] // 2, row0=row0),
        grid_spec=pltpu.PrefetchScalarGridSpec(
            num_scalar_prefetch=1,
            grid=(n_rows // tm,),
            in_specs=[pl.BlockSpec((tm, d), lambda i, dest: (i, 0)),
                      pl.BlockSpec(memory_space=pl.ANY)],
            out_specs=pl.BlockSpec(memory_space=pl.ANY),
            scratch_shapes=[pltpu.SemaphoreType.DMA(())]),
        out_shape=jax.ShapeDtypeStruct(slots.shape, slots.dtype),
        input_output_aliases={2: 0},
        compiler_params=_cparams(1),
        name="moe_dispatch",
    )(dest_flat, xn, slots)


def _moe_kernel(te_ref, tv_ref, x_ref, wg_ref, wu_ref, wd_ref, o_ref, xb_sc, acc_sc):
    i, j = pl.program_id(0), pl.program_id(1)
    valid = tv_ref[i] == 1

    @pl.when(j == 0)
    def _():
        xb_sc[...] = x_ref[...].astype(BF16)
        acc_sc[...] = jnp.zeros_like(acc_sc)

    @pl.when(valid)
    def _():
        xb = xb_sc[...]
        hg = _dot(xb, wg_ref[0].astype(BF16))
        hu = _dot(xb, wu_ref[0].astype(BF16))
        h = (_silu(hg) * hu).astype(BF16)
        acc_sc[...] += _dot(h, wd_ref[0].astype(BF16))

    @pl.when(j == pl.num_programs(1) - 1)
    def _():
        o_ref[...] = acc_sc[...]


def moe_experts(slots, tile_expert, tile_valid, wg, wu, wd, tm, tf):
    n_slots, d = slots.shape
    d_ff = wg.shape[2]
    n_f = d_ff // tf

    def f_idx(i, j, te, tv):
        return jnp.where(tv[i] == 1, j, n_f - 1)

    return pl.pallas_call(
        _moe_kernel,
        grid_spec=pltpu.PrefetchScalarGridSpec(
            num_scalar_prefetch=2,
            grid=(n_slots // tm, n_f),
            in_specs=[pl.BlockSpec((tm, d), lambda i, j, te, tv: (i, 0)),
                      pl.BlockSpec((1, d, tf), lambda i, j, te, tv: (te[i], 0, f_idx(i, j, te, tv))),
                      pl.BlockSpec((1, d, tf), lambda i, j, te, tv: (te[i], 0, f_idx(i, j, te, tv))),
                      pl.BlockSpec((1, tf, d), lambda i, j, te, tv: (te[i], f_idx(i, j, te, tv), 0))],
            out_specs=pl.BlockSpec((tm, d), lambda i, j, te, tv: (i, 0)),
            scratch_shapes=[pltpu.VMEM((tm, d), BF16), pltpu.VMEM((tm, d), F32)]),
        out_shape=jax.ShapeDtypeStruct((n_slots, d), F32),
        compiler_params=_cparams(2),
        name="moe_experts",
    )(tile_expert, tile_valid, slots, wg, wu, wd)


def _combine_kernel(dest_ref, x_ref, gates_ref, g_ref, y_hbm_ref, o_ref, buf_sc, sem, *, t_rows, row0):
    tm = x_ref.shape[0]
    base = row0 + pl.program_id(0) * tm

    def copies(r):
        return [_row_copy(y_hbm_ref, dest_ref[k * t_rows + base + r], buf_sc.at[k], r, sem)
                for k in range(2)]

    def start(r, carry):
        for k, cp in enumerate(copies(r)):
            cp.start(priority=k)
        return carry

    def wait(r, carry):
        for cp in copies(r):
            cp.wait()
        return carry

    lax.fori_loop(0, tm, start, 0, unroll=DMA_UNROLL)
    lax.fori_loop(0, tm, wait, 0, unroll=DMA_UNROLL)
    gates = gates_ref[...]
    out = x_ref[...] + (gates[:, 0:1] * buf_sc[0] + gates[:, 1:2] * buf_sc[1])
    o_ref[...] = _rms(out, g_ref[...], NORM_EPS)


def moe_combine(x, gates_col, dest_flat, y_slots, g_final, tm, row0):
    n_rows, d = x.shape
    t_rows = gates_col.shape[0]
    assert row0 % tm == 0 and n_rows % tm == 0
    blk0 = row0 // tm
    return pl.pallas_call(
        functools.partial(_combine_kernel, t_rows=t_rows, row0=row0),
        grid_spec=pltpu.PrefetchScalarGridSpec(
            num_scalar_prefetch=1,
            grid=(n_rows // tm,),
            in_specs=[pl.BlockSpec((tm, d), lambda i, dest: (i, 0)),
                      pl.BlockSpec((tm, 2), lambda i, dest: (i + blk0, 0)),
                      pl.BlockSpec((1, d), lambda i, dest: (0, 0)),
                      pl.BlockSpec(memory_space=pl.ANY)],
            out_specs=pl.BlockSpec((tm, d), lambda i, dest: (i, 0)),
            scratch_shapes=[pltpu.VMEM((2, tm, d), F32), pltpu.SemaphoreType.DMA(())]),
        out_shape=jax.ShapeDtypeStruct((n_rows, d), F32),
        compiler_params=_cparams(1),
        name="moe_combine",
    )(dest_flat, x, gates_col, g_final.reshape(1, d), y_slots)


def moe_final(x_prompt, x_sample, g_ffn, g_final, router_w, router_b, wg, wu, wd):
    n_prompt, n_sample = x_prompt.shape[0], x_sample.shape[0]
    t_rows = n_prompt + n_sample
    route = lambda x, counts0: moe_router(x, counts0, g_ffn, router_w, router_b,
                                          _row_tile(x.shape[0], ROUTER_TILE, LANES))
    xn_p, idx_p, gates_p, rank_p, counts_p = route(x_prompt, jnp.zeros((N_EXPERTS, LANES), jnp.int32))
    xn_s, idx_s, gates_s, rank_s, counts = route(x_sample, counts_p)
    idx, gates, rank = (jnp.concatenate(pair, axis=1) for pair in
                        ((idx_p, idx_s), (gates_p, gates_s), (rank_p, rank_s)))
    counts = counts[:, 0]
    tm = MOE_TILE
    n_tiles = -(-2 * t_rows // tm) + N_EXPERTS
    padded = (counts + tm - 1) // tm * tm
    pend = jnp.cumsum(padded)
    pstart = pend - padded
    dest = jnp.sum(jnp.where(idx[:, :, None] == jnp.arange(N_EXPERTS)[None, None, :], pstart[None, None, :], 0),
                   axis=-1) + rank
    dest_flat = dest.reshape(-1).astype(jnp.int32)
    tile_start = jnp.arange(n_tiles, dtype=jnp.int32) * tm
    tile_valid = (tile_start < pend[-1]).astype(jnp.int32)
    last_start = jnp.maximum(pend[-1] - tm, 0)
    probe = jnp.minimum(tile_start, last_start)
    tile_expert = jnp.minimum(jnp.sum((pend[None, :] <= probe[:, None]).astype(jnp.int32), axis=1),
                              N_EXPERTS - 1)
    slots = jnp.zeros((n_tiles * tm, x_prompt.shape[1]), F32)
    slots = moe_dispatch(xn_p, dest_flat, slots, 0, _row_tile(n_prompt, GATHER_TILE))
    slots = moe_dispatch(xn_s, dest_flat, slots, n_prompt, _row_tile(n_sample, GATHER_TILE))
    y_slots = moe_experts(slots, tile_expert, tile_valid, wg, wu, wd, tm, MOE_FF_TILE)
    gates_col = gates.T
    y_prompt = moe_combine(x_prompt, gates_col, dest_flat, y_slots, g_final,
                           _row_tile(n_prompt, COMBINE_TILE), 0)
    y_sample = moe_combine(x_sample, gates_col, dest_flat, y_slots, g_final, n_sample, n_prompt)
    return y_prompt, y_sample


def _block_diag(w):
    h, i, j = w.shape
    eye = jnp.eye(h, dtype=w.dtype)
    return jnp.einsum("hij,hg->higj", w, eye).reshape(h * i, h * j)


def kernel(x_prompt, x_sample, state_lru_conv, state_lru_h, state_rwkv_shift, state_rwkv_wkv, state_ssm_conv, state_ssm, norm_mix, norm_ffn, norm_final, w_in0, lru_conv_w, lru_conv_b, lru_wa, lru_ba, lru_wx, lru_bx, lru_lambda, rwkv_mu, rwkv_w0, rwkv_w_decay_up, rwkv_a0, rwkv_w_iclr_up, rwkv_w_gate_up, rwkv_k_k, rwkv_k_a, rwkv_r_k, rwkv_ln_w, rwkv_ln_b, w_out0, ffn_wg, ffn_wu, ffn_wd, w_in1, ssm_conv_w, ssm_conv_b, ssm_dt_bias, ssm_a_log, ssm_d, ssm_norm_g, w_out1, router_w, router_b, moe_wg, moe_wu, moe_wd):
    nb, seq, d = x_prompt.shape
    ns = x_sample.shape[0]
    tp = nb * seq
    lru_w = lru_conv_w.shape[-1]
    rw_w = rwkv_w0.shape[-1]
    shift_cols = rwkv_mu.shape[-1]
    n_rheads = rw_w // RWKV_HEAD
    inner = ssm_norm_g.shape[-1]
    n_sheads = ssm_a_log.shape[-1]
    conv_ch = ssm_conv_w.shape[-1]

    xp, xs = x_prompt.reshape(tp, d), x_sample.reshape(ns, d)
    tmp, tms = (_row_tile(n, TOKEN_TILE, 2 * SUBLANES) for n in (tp, ns))

    row = lambda v: v.reshape(1, -1)
    lp = dict(cw=lru_conv_w[0], cb=row(lru_conv_b[0]), wa=_block_diag(lru_wa[0]).astype(BF16),
              ba=row(lru_ba[0]), wx=_block_diag(lru_wx[0]).astype(BF16), bx=row(lru_bx[0]),
              lam=row(lru_lambda[0]))
    hsum = _block_diag(jnp.ones((n_rheads, RWKV_HEAD, RWKV_HEAD), BF16))
    rp = dict(mu=row(rwkv_mu[0]), w0=row(rwkv_w0[0]), wdec=rwkv_w_decay_up[0], a0=row(rwkv_a0[0]),
              wiclr=rwkv_w_iclr_up[0], wgate=rwkv_w_gate_up[0], kk=row(rwkv_k_k[0]), ka=row(rwkv_k_a[0]),
              rk=row(rwkv_r_k[0]), lnw=row(rwkv_ln_w[0]), lnb=row(rwkv_ln_b[0]), hsum=hsum)
    mp = dict(cw=ssm_conv_w[0], cb=row(ssm_conv_b[0]), dtb=row(ssm_dt_bias[0]), alog=row(ssm_a_log[0]),
              dexp=row(jnp.repeat(ssm_d[0], SSM_HEAD)), ng=row(ssm_norm_g[0]))

    w_in0_rwkv = w_in0[0][:, 2 * lru_w:]

    def in0(x, tm):
        return (norm_matmul(x, norm_mix[0], w_in0[0], 0, 1, 2 * lru_w, tm, "in0_lru"),
                norm_matmul(x, norm_mix[0], w_in0_rwkv, 0, 2, shift_cols // 2, tm, "in0_rwkv"))

    proj_lru_p, proj_rwkv_p = in0(xp, tmp)
    proj_lru_s, s_p_rwkv = in0(xs, tms)
    zeros = lambda *shape: jnp.zeros(shape, F32)
    out_a_p, p_lru_conv, p_lru_h = lru_prompt(proj_lru_p, nb, seq, zeros(nb, 3, lru_w), zeros(nb, lru_w), lp)
    out_b_p, p_shift, p_wkv = rwkv_prompt(proj_rwkv_p, nb, seq, zeros(nb, shift_cols),
                                          zeros(nb, n_rheads, RWKV_HEAD, RWKV_HEAD), rp)
    out_a_s, s_lru_conv, s_lru_h = lru_step(proj_lru_s[:, :lru_w], proj_lru_s[:, lru_w:], state_lru_conv[0],
                                            state_lru_h[0], lp)
    out_b_s, s_wkv = rwkv_step(s_p_rwkv, state_rwkv_shift[0], state_rwkv_wkv[0], rp)

    def mix0_ffn(out_a, out_b, x, tm):
        x = matmul_residual([out_a, out_b], w_out0[0], x, tm, d, "out0")
        return ffn_residual(x, norm_ffn[0], ffn_wg[0], ffn_wu[0], ffn_wd[0], tm, FFN_FF_TILE)

    xp = mix0_ffn(out_a_p, out_b_p, xp, tmp)
    xs = mix0_ffn(out_a_s, out_b_s, xs, tms)

    tn1 = 1536
    w_in1_dt = w_in1[0][:, inner + conv_ch:]

    def in1(x, tm):
        return (norm_matmul(x, norm_mix[1], w_in1[0], 0, (inner + conv_ch) // tn1, tn1, tm, "in1_main", BF16),
                norm_matmul(x, norm_mix[1], w_in1_dt, 0, 1, n_sheads, tm, "in1_dt"))

    proj1_p, dt_p = in1(xp, tmp)
    proj1_s, dt_s = in1(xs, tms)
    y_mix_p, p_ssm_conv, p_ssm = ssd_prompt(proj1_p, dt_p, nb, seq, zeros(nb, 3, conv_ch),
                                            zeros(nb, n_sheads, SSM_HEAD, SSM_STATE), mp)
    y_mix_s, s_ssm_conv, s_ssm = ssd_step(proj1_s[:, :inner], proj1_s[:, inner:], dt_s,
                                          state_ssm_conv[0], state_ssm[0], mp)
    xp = matmul_residual([y_mix_p], w_out1[0], xp, tmp, 512, "out1")
    xs = matmul_residual([y_mix_s], w_out1[0], xs, tms, 512, "out1")
    y_p, y_s = moe_final(xp, xs, norm_ffn[1], norm_final, router_w[0], router_b[0], moe_wg[0], moe_wu[0],
                         moe_wd[0])

    return (y_p.reshape(nb, seq, d), y_s.reshape(ns, 1, d),
            p_lru_conv[None], p_lru_h.reshape(1, nb, lru_w), p_shift.reshape(1, nb, shift_cols), p_wkv[None],
            p_ssm_conv[None], p_ssm[None],
            s_lru_conv.reshape(1, ns, 3, lru_w), s_lru_h[None], s_p_rwkv[None], s_wkv[None],
            s_ssm_conv.reshape(1, ns, 3, conv_ch), s_ssm[None])
```

```python
import functools

import jax
import jax.numpy as jnp
from jax import lax
from jax.experimental import pallas as pl
from jax.experimental.pallas import tpu as pltpu

F32 = jnp.float32
BF16 = jnp.bfloat16
HIGHEST = lax.Precision.HIGHEST

NORM_EPS = 1e-6
CONV_W = 4
LRU_HEADS = 8
LRU_C = 8.0
RWKV_HEAD = 64
DECAY_RANK = 64
ICLR_RANK = 64
GATE_RANK = 128
RWKV_GN_EPS = 64e-5
SSM_HEAD = 64
SSM_GROUPS = 8
SSM_STATE = 128
SSM_CHUNK = 128
SSM_NORM_EPS = 1e-5
N_EXPERTS = 8

V7X_VMEM_BYTES = 64 * 1024 * 1024
VMEM_LIMIT = V7X_VMEM_BYTES - 8 * 1024 * 1024
SUBLANES = 8
LANES = 128

RWKV_CHUNK = 64
LRU_CHUNK = 256
MOE_TILE = 1024
MOE_FF_TILE = 512
FFN_FF_TILE = 512
TOKEN_TILE = 1024
ROUTER_TILE = 512
GATHER_TILE = 512
COMBINE_TILE = 512
DMA_UNROLL = 8


def _cparams(n_axes):
    return pltpu.CompilerParams(dimension_semantics=("arbitrary",) * n_axes,
                                vmem_limit_bytes=VMEM_LIMIT)


def _row_tile(n_rows, cap, mult=SUBLANES):
    best = None
    for t in range(mult, min(cap, n_rows) + 1, mult):
        if n_rows % t == 0:
            best = t
    assert best is not None, (n_rows, cap)
    return best


def _dot(a, b, precision=None):
    return jnp.dot(a, b, preferred_element_type=F32, precision=precision)


def _dot_nt(a, b, precision=None):
    return lax.dot_general(a, b, (((1,), (1,)), ((), ())), preferred_element_type=F32,
                           precision=precision)


def _dot_tn(a, b, precision=None):
    return lax.dot_general(a, b, (((0,), (0,)), ((), ())), preferred_element_type=F32,
                           precision=precision)


_NN = (((1,), (0,)), ((), ()))
_NT = (((1,), (1,)), ((), ()))
_TN = (((0,), (0,)), ((), ()))


def _split2(x):
    hi = x.astype(BF16)
    lo = (x - hi.astype(F32)).astype(BF16)
    return hi, lo


def _split3(x):
    hi = x.astype(BF16)
    r1 = x - hi.astype(F32)
    mid = r1.astype(BF16)
    lo = (r1 - mid.astype(F32)).astype(BF16)
    return hi, mid, lo


def _dg(a, b, dims):
    return lax.dot_general(a, b, dims, preferred_element_type=F32)


def _dot3(a2, b2, dims=_NN):
    (ah, al), (bh, bl) = a2, b2
    return _dg(ah, bh, dims) + _dg(al, bh, dims) + _dg(ah, bl, dims)


def _dot_exact_rhs(x, m_bf16, terms=3):
    return sum(_dg(part, m_bf16, _NN) for part in (_split3(x) if terms == 3 else _split2(x)))


def _dot_exact_lhs(m_bf16, x):
    return sum(_dg(m_bf16, part, _NN) for part in _split3(x))


def _softplus(x):
    return jnp.maximum(x, 0.0) + jnp.log1p(jnp.exp(-jnp.abs(x)))


def _silu(x):
    return x * jax.nn.sigmoid(x)


def _gelu_tanh(x):
    return 0.5 * x * (1.0 + jnp.tanh(0.7978845608028654 * (x + 0.044715 * (x * x * x))))


def _rms(x, g, eps):
    return x * lax.rsqrt(jnp.mean(x * x, axis=-1, keepdims=True) + eps) * g


def _norm_mm_kernel(x_ref, g_ref, w_ref, o_ref, xn_sc):
    @pl.when(pl.program_id(1) == 0)
    def _():
        xn_sc[...] = _rms(x_ref[...], g_ref[...], NORM_EPS).astype(BF16)

    o_ref[...] = _dot(xn_sc[...], w_ref[...].astype(BF16)).astype(o_ref.dtype)


def norm_matmul(x, g, w, col0_blk, n_blk, tn, tm, name, out_dtype=F32):
    t_rows, d = x.shape
    return pl.pallas_call(
        _norm_mm_kernel,
        grid=(t_rows // tm, n_blk),
        in_specs=[pl.BlockSpec((tm, d), lambda i, j: (i, 0)),
                  pl.BlockSpec((1, d), lambda i, j: (0, 0)),
                  pl.BlockSpec((d, tn), lambda i, j: (0, j + col0_blk))],
        out_specs=pl.BlockSpec((tm, tn), lambda i, j: (i, j)),
        out_shape=jax.ShapeDtypeStruct((t_rows, n_blk * tn), out_dtype),
        scratch_shapes=[pltpu.VMEM((tm, d), BF16)],
        compiler_params=_cparams(2),
        name=name,
    )(x, g.reshape(1, d), w)


def _mm_res_kernel(*refs, n_in):
    x_refs, w_refs = refs[:n_in], refs[n_in:2 * n_in]
    res_ref, o_ref = refs[2 * n_in], refs[2 * n_in + 1]
    acc = res_ref[...]
    for x_ref, w_ref in zip(x_refs, w_refs):
        acc = acc + _dot(x_ref[...].astype(BF16), w_ref[...].astype(BF16))
    o_ref[...] = acc


def matmul_residual(xs, w, res, tm, tn, name):
    n_in = len(xs)
    t_rows, kp = xs[0].shape
    n_cols = w.shape[1]
    in_specs = [pl.BlockSpec((tm, kp), lambda i, j: (i, 0)) for _ in xs]
    in_specs += [pl.BlockSpec((kp, tn), functools.partial(lambda i, j, p: (p, j), p=p))
                 for p in range(n_in)]
    in_specs += [pl.BlockSpec((tm, tn), lambda i, j: (i, j))]
    return pl.pallas_call(
        functools.partial(_mm_res_kernel, n_in=n_in),
        grid=(t_rows // tm, n_cols // tn),
        in_specs=in_specs,
        out_specs=pl.BlockSpec((tm, tn), lambda i, j: (i, j)),
        out_shape=jax.ShapeDtypeStruct((t_rows, n_cols), F32),
        compiler_params=_cparams(2),
        name=name,
    )(*xs, *([w] * n_in), res)


def _ffn_kernel(x_ref, g_ref, wg_ref, wu_ref, wd_ref, o_ref, xn_sc, acc_sc):
    j = pl.program_id(1)

    @pl.when(j == 0)
    def _():
        xn_sc[...] = _rms(x_ref[...], g_ref[...], NORM_EPS).astype(BF16)
        acc_sc[...] = jnp.zeros_like(acc_sc)

    xn = xn_sc[...]
    hg = _dot(xn, wg_ref[...].astype(BF16))
    hu = _dot(xn, wu_ref[...].astype(BF16))
    h = (_silu(hg) * hu).astype(BF16)
    acc_sc[...] += _dot(h, wd_ref[...].astype(BF16))

    @pl.when(j == pl.num_programs(1) - 1)
    def _():
        o_ref[...] = x_ref[...] + acc_sc[...]


def ffn_residual(x, g, wg, wu, wd, tm, tf):
    t_rows, d = x.shape
    d_ff = wg.shape[1]
    return pl.pallas_call(
        _ffn_kernel,
        grid=(t_rows // tm, d_ff // tf),
        in_specs=[pl.BlockSpec((tm, d), lambda i, j: (i, 0)),
                  pl.BlockSpec((1, d), lambda i, j: (0, 0)),
                  pl.BlockSpec((d, tf), lambda i, j: (0, j)),
                  pl.BlockSpec((d, tf), lambda i, j: (0, j)),
                  pl.BlockSpec((tf, d), lambda i, j: (j, 0))],
        out_specs=pl.BlockSpec((tm, d), lambda i, j: (i, 0)),
        out_shape=jax.ShapeDtypeStruct((t_rows, d), F32),
        scratch_shapes=[pltpu.VMEM((tm, d), BF16), pltpu.VMEM((tm, d), F32)],
        compiler_params=_cparams(2),
        name="ffn_swiglu",
    )(x, g.reshape(1, d), wg, wu, wd)


def _conv4(u, u1, u2, u3, cw, cb):
    return cb + cw[3:4] * u + cw[2:3] * u1 + cw[1:2] * u2 + cw[0:1] * u3


def _lru_gates(xc, wa, ba, wx, bx, lam):
    xb = xc.astype(BF16)
    r = jax.nn.sigmoid(_dot(xb, wa) + ba)
    i = jax.nn.sigmoid(_dot(xb, wx) + bx)
    log_a = -LRU_C * r * _softplus(-lam)
    a = jnp.exp(log_a)
    u = jnp.sqrt(1.0 - jnp.exp(2.0 * log_a)) * (i * xc)
    return a, u


def _lru_prompt_kernel(x_ref, g_ref, conv0_ref, h0_ref, cw_ref, cb_ref, wa_ref, ba_ref, wx_ref,
                       bx_ref, lam_ref, o_ref, nconv_ref, nh_ref, ext_sc, h_sc, *, lc):
    width = x_ref.shape[1]

    @pl.when(pl.program_id(1) == 0)
    def _():
        ext_sc[0:SUBLANES, :] = jnp.zeros((SUBLANES, width), F32)
        ext_sc[SUBLANES - 3:SUBLANES, :] = conv0_ref[0]
        h_sc[...] = h0_ref[0]

    u = x_ref[...]
    ext_sc[SUBLANES:SUBLANES + lc, :] = u
    xc = _conv4(u, ext_sc[SUBLANES - 1:SUBLANES - 1 + lc, :], ext_sc[SUBLANES - 2:SUBLANES - 2 + lc, :],
                ext_sc[SUBLANES - 3:SUBLANES - 3 + lc, :], cw_ref[...], cb_ref[...])
    tail = ext_sc[lc + SUBLANES - 3:lc + SUBLANES, :]
    ext_sc[SUBLANES - 3:SUBLANES, :] = tail
    nconv_ref[0] = tail

    a, h = _lru_gates(xc, wa_ref[...], ba_ref[...], wx_ref[...], bx_ref[...], lam_ref[...])
    row = lax.broadcasted_iota(jnp.int32, (lc, width), 0)
    s = 1
    while s < lc:
        keep = row >= s
        a_sh = jnp.where(keep, pltpu.roll(a, s, 0), 1.0)
        h_sh = jnp.where(keep, pltpu.roll(h, s, 0), 0.0)
        h = a * h_sh + h
        a = a * a_sh
        s *= 2
    hs = h + a * h_sc[...]
    h_last = hs[lc - 1:lc, :]
    h_sc[...] = h_last
    nh_ref[0] = h_last
    o_ref[...] = (hs * _gelu_tanh(g_ref[...])).astype(o_ref.dtype)


def lru_prompt(proj_lru, n_batch, seq, conv0, h0, lp):
    width = conv0.shape[-1]
    lc = min(LRU_CHUNK, seq)
    n_chunks = seq // lc
    full = lambda shape: pl.BlockSpec(shape, lambda b, c: (0,) * len(shape))
    return pl.pallas_call(
        functools.partial(_lru_prompt_kernel, lc=lc),
        grid=(n_batch, n_chunks),
        in_specs=[pl.BlockSpec((lc, width), lambda b, c: (b * n_chunks + c, 0)),
                  pl.BlockSpec((lc, width), lambda b, c: (b * n_chunks + c, 1)),
                  pl.BlockSpec((1, 3, width), lambda b, c: (b, 0, 0)),
                  pl.BlockSpec((1, 1, width), lambda b, c: (b, 0, 0)),
                  full((CONV_W, width)), full((1, width)), full((width, width)), full((1, width)),
                  full((width, width)), full((1, width)), full((1, width))],
        out_specs=[pl.BlockSpec((lc, width), lambda b, c: (b * n_chunks + c, 0)),
                   pl.BlockSpec((1, 3, width), lambda b, c: (b, 0, 0)),
                   pl.BlockSpec((1, 1, width), lambda b, c: (b, 0, 0))],
        out_shape=[jax.ShapeDtypeStruct((n_batch * seq, width), BF16),
                   jax.ShapeDtypeStruct((n_batch, 3, width), F32),
                   jax.ShapeDtypeStruct((n_batch, 1, width), F32)],
        scratch_shapes=[pltpu.VMEM((lc + SUBLANES, width), F32), pltpu.VMEM((1, width), F32)],
        compiler_params=_cparams(2),
        name="lru_prompt",
    )(proj_lru, proj_lru, conv0, h0.reshape(n_batch, 1, width), lp["cw"], lp["cb"], lp["wa"], lp["ba"],
      lp["wx"], lp["bx"], lp["lam"])


def _rows_call(body, inputs, out_shapes, name):
    return pl.pallas_call(body, out_shape=out_shapes, name=name,
                          compiler_params=pltpu.CompilerParams(vmem_limit_bytes=VMEM_LIMIT))(*inputs)


def _lru_step_kernel(x_ref, g_ref, buf_ref, h0_ref, cw_ref, cb_ref, wa_ref, ba_ref, wx_ref, bx_ref,
                     lam_ref, o_ref, nbuf_ref, nh_ref):
    width = x_ref.shape[1]
    u = x_ref[...]
    b0, b1, b2 = (buf_ref[:, k * width:(k + 1) * width] for k in range(3))
    xc = _conv4(u, b2, b1, b0, cw_ref[...], cb_ref[...])
    a, uu = _lru_gates(xc, wa_ref[...], ba_ref[...], wx_ref[...], bx_ref[...], lam_ref[...])
    h = a * h0_ref[...] + uu
    nh_ref[...] = h
    o_ref[...] = (h * _gelu_tanh(g_ref[...])).astype(o_ref.dtype)
    nbuf_ref[:, 0:width] = b1
    nbuf_ref[:, width:2 * width] = b2
    nbuf_ref[:, 2 * width:3 * width] = u


def lru_step(x_lru, g_lru, conv_buf, h0, lp):
    n, width = x_lru.shape
    return _rows_call(
        _lru_step_kernel,
        [x_lru, g_lru, conv_buf.reshape(n, 3 * width), h0, lp["cw"], lp["cb"], lp["wa"], lp["ba"],
         lp["wx"], lp["bx"], lp["lam"]],
        [jax.ShapeDtypeStruct((n, width), BF16), jax.ShapeDtypeStruct((n, 3 * width), F32),
         jax.ShapeDtypeStruct((n, width), F32)],
        "lru_step")


def _rwkv_rows(mixed, rp):
    w = rp["w0"].shape[1]
    r, k, v = mixed[:, 0:w], mixed[:, w:2 * w], mixed[:, 2 * w:3 * w]
    o = 3 * w
    wd = mixed[:, o:o + DECAY_RANK]
    ad = mixed[:, o + DECAY_RANK:o + DECAY_RANK + ICLR_RANK]
    gd = mixed[:, o + DECAY_RANK + ICLR_RANK:o + DECAY_RANK + ICLR_RANK + GATE_RANK]
    dec_in = rp["w0"] + _dot(jnp.tanh(wd).astype(BF16), rp["wdec"].astype(BF16))
    w_log = -_softplus(-dec_in) - 0.5
    lw = -jnp.exp(w_log)
    iclr = jax.nn.sigmoid(rp["a0"] + _dot(ad.astype(BF16), rp["wiclr"].astype(BF16)))
    gate = _dot(jax.nn.sigmoid(gd).astype(BF16), rp["wgate"].astype(BF16))
    kk = k * rp["kk"]
    ss = _dot_exact_rhs(kk * kk, rp["hsum"])
    kkn = kk / jnp.maximum(jnp.sqrt(ss), 1e-12)
    k2 = k * (1.0 + (iclr - 1.0) * rp["ka"])
    return r, k2, v, lw, -kkn, kkn * iclr, gate


def _rwkv_post(o, r, k2, v, gate, rp):
    inv = 1.0 / RWKV_HEAD
    mu = _dot_exact_rhs(o, rp["hsum"], 2) * inv
    d = o - mu
    var = _dot_exact_rhs(d * d, rp["hsum"], 2) * inv
    on = d * lax.rsqrt(var + RWKV_GN_EPS) * rp["lnw"] + rp["lnb"]
    bonus = _dot_exact_rhs(r * k2 * rp["rk"], rp["hsum"], 2) * v
    return (on + bonus) * gate


_RWKV_PARAM_NAMES = ("mu", "w0", "wdec", "a0", "wiclr", "wgate", "kk", "ka", "rk", "lnw", "lnb", "hsum")


def _rwkv_prompt_kernel(*refs, chunk, nbs):
    p_refs, shift0_ref, s0_ref, rest = refs[:nbs], refs[nbs], refs[nbs + 1], refs[nbs + 2:]
    n_prm = len(_RWKV_PARAM_NAMES)
    rp = {name: ref[...] for name, ref in zip(_RWKV_PARAM_NAMES, rest[:n_prm])}
    o_ref, nshift_ref, nwkv_ref, prev_sc, s_sc, o_sc = rest[n_prm:]
    n_pairs = s_sc.shape[0] // nbs
    hd = RWKV_HEAD
    assert chunk == hd, "the pair-packed layout below uses chunk == head size"

    @pl.when(pl.program_id(1) == 0)
    def _():
        for i in range(nbs):
            prev_sc[i] = shift0_ref[i]
            for p in range(n_pairs):
                s_sc[i * n_pairs + p] = jnp.concatenate([s0_ref[i, 2 * p], s0_ref[i, 2 * p + 1]], axis=0).T

    ti = lax.broadcasted_iota(jnp.int32, (chunk, chunk), 0)
    si = lax.broadcasted_iota(jnp.int32, (chunk, chunk), 1)
    lower = (ti >= si).astype(BF16)

    def prep(i):
        p = p_refs[i][...]
        row = lax.broadcasted_iota(jnp.int32, p.shape, 0)
        p_prev = jnp.where(row >= 1, pltpu.roll(p, 1, 0), prev_sc[i])
        last = p[chunk - 1:chunk, :]
        prev_sc[i] = last
        nshift_ref[i] = last
        mixed = p + (p_prev - p) * rp["mu"]
        r, k2, v, lw, a, b, gate = _rwkv_rows(mixed, rp)
        cs = _dot_exact_lhs(lower, lw)
        g_in, g_ex, g_inv = jnp.exp(cs), jnp.exp(cs - lw), jnp.exp(-cs)
        bt, kt = b * g_inv, k2 * g_inv
        g_end = g_in[chunk - 1:chunk, :]
        return dict(r=r, k2=k2, v=v, gate=gate, at=a * g_ex, bt=bt, kt=kt, rt=r * g_in, g_end=g_end,
                    bc=bt * g_end, kc=kt * g_end)

    preps = [prep(i) for i in range(nbs)]
    n_fac = max(1, (chunk - 1).bit_length())
    pw_ = 2 * hd
    lane1 = lax.broadcasted_iota(jnp.int32, (1, pw_), 1)
    lo1 = lane1 < hd
    lo2 = jnp.concatenate([lo1, lo1], axis=1)
    ti2 = lax.broadcasted_iota(jnp.int32, (chunk, pw_), 0)
    si2 = lax.broadcasted_iota(jnp.int32, (chunk, pw_), 1) % hd
    strict2, incl2 = ti2 > si2, ti2 >= si2

    def bdiag(x, lo):
        return jnp.concatenate([jnp.where(lo, x, 0), jnp.where(lo, 0, x)], axis=0)

    def bdiag2(x2, lo):
        return bdiag(x2[0], lo), bdiag(x2[1], lo)

    units = [(i, q) for i in range(nbs) for q in range(n_pairs)]
    pairs = range(len(units))
    pls = [slice(q * pw_, (q + 1) * pw_) for _, q in units]
    a_p, r_p, v_p, bt_p, kt_p, bc_p, kc_p, ge_p = (
        [preps[i][name][:, pls[u]] for u, (i, _) in enumerate(units)]
        for name in ("at", "rt", "v", "bt", "kt", "bc", "kc", "g_end"))
    mas, mrs, v2s = [], [], []
    for p in pairs:
        b2, k2_ = _split2(bt_p[p]), _split2(kt_p[p])
        rhs_rows = tuple(jnp.concatenate([bdiag(b2[i], lo1), bdiag(k2_[i], lo1)], axis=0) for i in range(2))
        mas.append(_dot3(_split2(a_p[p]), rhs_rows, _NT))
        mrs.append(_dg(r_p[p].astype(BF16), rhs_rows[0], _NT))
        v2s.append(bdiag2(_split2(v_p[p]), lo1))
    labs = [jnp.where(strict2, m[:, :pw_], 0.0) for m in mas]
    lrbs = [jnp.where(incl2, m[:, :pw_], 0.0).astype(BF16) for m in mrs]
    lakvs = [_dot3(_split2(jnp.where(strict2, mas[p][:, pw_:], 0.0)), v2s[p]) for p in pairs]
    lrkvs = [_dg(jnp.where(incl2, mrs[p][:, pw_:], 0.0).astype(BF16), v2s[p][0], _NN) for p in pairs]
    ys = [jnp.concatenate([a_p[p], lakvs[p]], axis=1) for p in pairs]
    pws = [_split2(lab) for lab in labs]
    for f in range(n_fac):
        ys = [ys[p] + _dot3(pws[p], bdiag2(_split2(ys[p]), lo2)) for p in pairs]
        if f + 1 < n_fac:
            pws = [_split2(_dot3(pw, bdiag2(pw, lo1))) for pw in pws]
    y2s = [_split2(y) for y in ys]
    qos = [jnp.concatenate([r_p[p], lrkvs[p]], axis=1) + _dg(lrbs[p], bdiag(y2s[p][0], lo2), _NN)
           for p in pairs]
    xs_ = [_dot3(_split2(bc_p[p]), y2s[p], _TN) for p in pairs]
    kvs = [_dot3(_split2(kc_p[p]), _split2(v_p[p]), _TN) for p in pairs]
    mts = [_split2(jnp.where(lo1, xs_[p][:hd, :pw_], xs_[p][hd:, :pw_]) + jnp.where(ti2 == si2, ge_p[p], 0.0))
           for p in pairs]
    n0s = [jnp.where(lo1, xs_[p][:hd, pw_:] + kvs[p][:hd], xs_[p][hd:, pw_:] + kvs[p][hd:]) for p in pairs]
    s0s = [bdiag2(_split2(s_sc[p]), lo1) for p in pairs]
    for p, (i, _) in enumerate(units):
        o_sc[i, :, pls[p]] = _dg(qos[p][:, :pw_].astype(BF16), s0s[p][0], _NN) + qos[p][:, pw_:]
    for p in pairs:
        s_sc[p] = _dot3(mts[p], s0s[p]) + n0s[p]

    @pl.when(pl.program_id(1) == pl.num_programs(1) - 1)
    def _():
        for p, (i, q) in enumerate(units):
            s_pair = s_sc[p].T
            nwkv_ref[i, 2 * q] = s_pair[:hd]
            nwkv_ref[i, 2 * q + 1] = s_pair[hd:]

    for i, pre in enumerate(preps):
        o_ref[i] = _rwkv_post(o_sc[i], pre["r"], pre["k2"], pre["v"], pre["gate"], rp).astype(o_ref.dtype)


def _rwkv_param_list(rp):
    return [rp[name] for name in _RWKV_PARAM_NAMES]


def rwkv_prompt(p_rwkv, n_batch, seq, shift0, wkv0, rp):
    cols = shift0.shape[-1]
    n_heads, hd = wkv0.shape[1], wkv0.shape[2]
    width = n_heads * hd
    chunk = min(RWKV_CHUNK, seq)
    n_chunks = seq // chunk
    prm = _rwkv_param_list(rp)
    prm_specs = [pl.BlockSpec(x.shape, lambda b, c: (0, 0)) for x in prm]
    nbs = 2 if n_batch % 2 == 0 else 1
    p_specs = [pl.BlockSpec((chunk, cols), functools.partial(
        lambda b, c, i: ((b * nbs + i) * n_chunks + c, 0), i=i)) for i in range(nbs)]
    out, new_shift, new_wkv = pl.pallas_call(
        functools.partial(_rwkv_prompt_kernel, chunk=chunk, nbs=nbs),
        grid=(n_batch // nbs, n_chunks),
        in_specs=p_specs + [pl.BlockSpec((nbs, 1, cols), lambda b, c: (b, 0, 0)),
                            pl.BlockSpec((nbs, n_heads, hd, hd), lambda b, c: (b, 0, 0, 0))] + prm_specs,
        out_specs=[pl.BlockSpec((nbs, chunk, width), lambda b, c: (b, c, 0)),
                   pl.BlockSpec((nbs, 1, cols), lambda b, c: (b, 0, 0)),
                   pl.BlockSpec((nbs, n_heads, hd, hd), lambda b, c: (b, 0, 0, 0))],
        out_shape=[jax.ShapeDtypeStruct((n_batch, seq, width), BF16),
                   jax.ShapeDtypeStruct((n_batch, 1, cols), F32),
                   jax.ShapeDtypeStruct((n_batch, n_heads, hd, hd), F32)],
        scratch_shapes=[pltpu.VMEM((nbs, 1, cols), F32), pltpu.VMEM((nbs * n_heads // 2, hd, 2 * hd), F32),
                        pltpu.VMEM((nbs, chunk, width), F32)],
        compiler_params=_cparams(2),
        name="rwkv_prompt",
    )(*([p_rwkv] * nbs), shift0.reshape(n_batch, 1, cols), wkv0, *prm)
    return out.reshape(n_batch * seq, width), new_shift, new_wkv


def _rwkv_step_pre_kernel(p_ref, prev_ref, *rest):
    n_prm = len(_RWKV_PARAM_NAMES)
    rp = {name: ref[...] for name, ref in zip(_RWKV_PARAM_NAMES, rest[:n_prm])}
    r_ref, k_ref, v_ref, w_ref, a_ref, b_ref, gate_ref = rest[n_prm:]
    p = p_ref[...]
    mixed = p + (prev_ref[...] - p) * rp["mu"]
    r, k2, v, lw, a, b, gate = _rwkv_rows(mixed, rp)
    r_ref[...] = r
    k_ref[...] = k2
    v_ref[...] = v
    w_ref[...] = jnp.exp(lw)
    a_ref[...] = a
    b_ref[...] = b
    gate_ref[...] = gate


def _rwkv_step_kernel(s_ref, w_ref, a_ref, b_ref, k_ref, r_ref, vt_ref, ns_ref, ot_ref):
    bt, n_heads = s_ref.shape[0], s_ref.shape[1]
    lane = lax.broadcasted_iota(jnp.int32, ot_ref.shape[1:], 1)
    for i in range(bt):
        vt = vt_ref[i]
        ss = [s_ref[i, h] for h in range(n_heads)]
        sas = [jnp.sum(ss[h] * a_ref[i, h], axis=-1, keepdims=True) for h in range(n_heads)]
        s_news = [ss[h] * w_ref[i, h] + sas[h] * b_ref[i, h] + vt[:, h:h + 1] * k_ref[i, h]
                  for h in range(n_heads)]
        for h in range(n_heads):
            ns_ref[i, h] = s_news[h]
        os_ = [jnp.sum(s_news[h] * r_ref[i, h], axis=-1, keepdims=True) for h in range(n_heads)]
        ot = jnp.zeros(ot_ref.shape[1:], F32)
        for h in range(n_heads):
            ot = jnp.where(lane == h, os_[h], ot)
        ot_ref[i] = ot


def _rwkv_step_post_kernel(o_ref, r_ref, k_ref, v_ref, gate_ref, *rest):
    n_prm = len(_RWKV_PARAM_NAMES)
    rp = {name: ref[...] for name, ref in zip(_RWKV_PARAM_NAMES, rest[:n_prm])}
    out_ref = rest[n_prm]
    out_ref[...] = _rwkv_post(o_ref[...], r_ref[...], k_ref[...], v_ref[...], gate_ref[...],
                              rp).astype(out_ref.dtype)


def rwkv_step(p_rwkv, shift_prev, wkv0, rp):
    n = p_rwkv.shape[0]
    n_heads, hd = wkv0.shape[1], wkv0.shape[2]
    width = n_heads * hd
    prm = _rwkv_param_list(rp)
    row = jax.ShapeDtypeStruct((n, width), F32)
    r, k2, v, w, a, b, gate = _rows_call(_rwkv_step_pre_kernel, [p_rwkv, shift_prev, *prm], [row] * 7,
                                         "rwkv_step_pre")
    hrow = lambda z: z.reshape(n, n_heads, 1, hd)
    vt = jnp.transpose(v.reshape(n, n_heads, hd), (0, 2, 1))
    bt = SUBLANES
    vec_spec = pl.BlockSpec((bt, n_heads, 1, hd), lambda i: (i, 0, 0, 0))
    st_spec = pl.BlockSpec((bt, n_heads, hd, hd), lambda i: (i, 0, 0, 0))
    t_spec = pl.BlockSpec((bt, hd, n_heads), lambda i: (i, 0, 0))
    new_wkv, ot = pl.pallas_call(
        _rwkv_step_kernel,
        grid=(n // bt,),
        in_specs=[st_spec] + [vec_spec] * 5 + [t_spec],
        out_specs=[st_spec, t_spec],
        out_shape=[jax.ShapeDtypeStruct(wkv0.shape, F32), jax.ShapeDtypeStruct((n, hd, n_heads), F32)],
        compiler_params=_cparams(1),
        name="rwkv_step",
    )(wkv0, hrow(w), hrow(a), hrow(b), hrow(k2), hrow(r), vt)
    o = jnp.transpose(ot, (0, 2, 1)).reshape(n, width)
    (out_b,) = _rows_call(_rwkv_step_post_kernel, [o, r, k2, v, gate, *prm],
                          [jax.ShapeDtypeStruct((n, width), BF16)], "rwkv_step_post")
    return out_b, new_wkv


def _mamba_post(y, xs, z, dexp, ng):
    y = (y + dexp * xs) * _silu(z)
    gw = y.shape[1] // SSM_GROUPS
    parts = []
    for g in range(SSM_GROUPS):
        yg = y[:, g * gw:(g + 1) * gw]
        parts.append(yg * lax.rsqrt(jnp.mean(yg * yg, axis=-1, keepdims=True) + SSM_NORM_EPS))
    return jnp.concatenate(parts, axis=1) * ng


def _ssd_prompt_kernel(z_ref, xlo_ref, xhi_ref, dt_ref, conv0_ref, s0_ref, cw_ref, cb_ref, dtb_ref,
                       alog_ref, dexp_ref, ng_ref, hexp_ref, qexp_ref, y_ref, nconv_ref, nssm_ref,
                       ext_sc, s_sc, y_sc, yo_sc, *, q):
    inner = xlo_ref.shape[1]
    n_heads = s_sc.shape[0]
    hpg = n_heads // SSM_GROUPS

    @pl.when(pl.program_id(1) == 0)
    def _():
        ext_sc[0:SUBLANES, :] = jnp.zeros((SUBLANES, ext_sc.shape[1]), F32)
        ext_sc[SUBLANES - 3:SUBLANES, :] = conv0_ref[0]
        s_sc[...] = s0_ref[0]

    ext_sc[SUBLANES:SUBLANES + q, 0:inner] = xlo_ref[...].astype(F32)
    ext_sc[SUBLANES:SUBLANES + q, inner:] = xhi_ref[...].astype(F32)
    xbc = _silu(_conv4(ext_sc[SUBLANES:SUBLANES + q, :], ext_sc[SUBLANES - 1:SUBLANES - 1 + q, :],
                       ext_sc[SUBLANES - 2:SUBLANES - 2 + q, :], ext_sc[SUBLANES - 3:SUBLANES - 3 + q, :],
                       cw_ref[...], cb_ref[...]))
    tail = ext_sc[q + SUBLANES - 3:q + SUBLANES, :]
    ext_sc[SUBLANES - 3:SUBLANES, :] = tail
    nconv_ref[0] = tail

    xs = xbc[:, 0:inner]
    gn = SSM_GROUPS * SSM_STATE
    bm = xbc[:, inner:inner + gn].astype(BF16)
    cm = xbc[:, inner + gn:].astype(BF16)
    dt = _softplus(dt_ref[...] + dtb_ref[...])
    dta = dt * (-jnp.exp(alog_ref[...]))
    ti = lax.broadcasted_iota(jnp.int32, (q, q), 0)
    si = lax.broadcasted_iota(jnp.int32, (q, q), 1)
    causal = ti >= si
    da = _dot_exact_lhs(causal.astype(BF16), dta)
    upper = (ti <= si).astype(BF16)
    da_t = sum(_dg(part, upper, _TN) for part in _split3(dta))
    da_end = da[q - 1:q, :]
    end_decay = jnp.exp(da_end)
    hexp, qexp = hexp_ref[...], qexp_ref[...]
    xdt = xs * _dot_exact_rhs(dt, hexp, 2)
    xdt_b = xdt.astype(BF16)
    xdt_end = (xdt * _dot_exact_rhs(jnp.exp(da_end - da), hexp, 2)).astype(BF16)
    from_start = _dot_exact_rhs(jnp.exp(da), hexp, 2)
    da_col = _dot_exact_rhs(da, qexp)

    heads = range(n_heads)
    bgs = [bm[:, g * SSM_STATE:(g + 1) * SSM_STATE] for g in range(SSM_GROUPS)]
    cgs = [cm[:, g * SSM_STATE:(g + 1) * SSM_STATE] for g in range(SSM_GROUPS)]
    scores = [_dot_nt(cgs[g], bgs[g]) for g in range(SSM_GROUPS)]
    hss = [slice(h * SSM_HEAD, (h + 1) * SSM_HEAD) for h in heads]
    s_hs = [s_sc[h] for h in heads]
    for h in heads:
        yo_sc[:, hss[h]] = _dot_nt(cgs[h // hpg], s_hs[h].astype(BF16))
    wts = [(scores[h // hpg]
            * jnp.exp(jnp.where(causal, da_col[:, h * q:(h + 1) * q] - da_t[h:h + 1, :], -jnp.inf))
            ).astype(BF16) for h in heads]
    for h in heads:
        y_sc[:, hss[h]] = _dot(wts[h], xdt_b[:, hss[h]])
    upd = [_dot_tn(xdt_end[:, hss[h]], bgs[h // hpg]) for h in heads]
    for h in heads:
        s_sc[h] = s_hs[h] * end_decay[:, h:h + 1] + upd[h]

    @pl.when(pl.program_id(1) == pl.num_programs(1) - 1)
    def _():
        nssm_ref[0] = s_sc[...]

    y = y_sc[...] + yo_sc[...] * from_start
    y_ref[...] = _mamba_post(y, xs, z_ref[...].astype(F32), dexp_ref[...], ng_ref[...]).astype(y_ref.dtype)


def ssd_prompt(proj1, dt_raw, n_batch, seq, conv0, ssm0, mp):
    n_heads, hd, n_state = ssm0.shape[1:]
    inner = n_heads * hd
    conv_ch = conv0.shape[-1]
    q = SSM_CHUNK if seq % SSM_CHUNK == 0 else seq
    n_chunks = seq // q
    full = lambda shape: pl.BlockSpec(shape, lambda b, c: (0,) * len(shape))
    return pl.pallas_call(
        functools.partial(_ssd_prompt_kernel, q=q),
        grid=(n_batch, n_chunks),
        in_specs=[pl.BlockSpec((q, inner), lambda b, c: (b * n_chunks + c, 0)),
                  pl.BlockSpec((q, inner), lambda b, c: (b * n_chunks + c, 1)),
                  pl.BlockSpec((q, inner), lambda b, c: (b * n_chunks + c, 2)),
                  pl.BlockSpec((q, n_heads), lambda b, c: (b * n_chunks + c, 0)),
                  pl.BlockSpec((1, 3, conv_ch), lambda b, c: (b, 0, 0)),
                  pl.BlockSpec((1, n_heads, hd, n_state), lambda b, c: (b, 0, 0, 0)),
                  full((CONV_W, conv_ch)), full((1, conv_ch)), full((1, n_heads)), full((1, n_heads)),
                  full((1, inner)), full((1, inner)), full((n_heads, inner)), full((n_heads, n_heads * q))],
        out_specs=[pl.BlockSpec((q, inner), lambda b, c: (b * n_chunks + c, 0)),
                   pl.BlockSpec((1, 3, conv_ch), lambda b, c: (b, 0, 0)),
                   pl.BlockSpec((1, n_heads, hd, n_state), lambda b, c: (b, 0, 0, 0))],
        out_shape=[jax.ShapeDtypeStruct((n_batch * seq, inner), BF16),
                   jax.ShapeDtypeStruct((n_batch, 3, conv_ch), F32),
                   jax.ShapeDtypeStruct(ssm0.shape, F32)],
        scratch_shapes=[pltpu.VMEM((q + SUBLANES, conv_ch), F32), pltpu.VMEM((n_heads, hd, n_state), F32),
                        pltpu.VMEM((q, inner), F32), pltpu.VMEM((q, inner), F32)],
        compiler_params=_cparams(2),
        name="ssd_prompt",
    )(proj1, proj1, proj1, dt_raw, conv0, ssm0, mp["cw"], mp["cb"], mp["dtb"], mp["alog"], mp["dexp"],
      mp["ng"], jnp.repeat(jnp.eye(n_heads, dtype=BF16), hd, axis=1),
      jnp.repeat(jnp.eye(n_heads, dtype=BF16), q, axis=1))


def _ssd_step_pre_kernel(x_ref, dt_ref, buf_ref, cw_ref, cb_ref, dtb_ref, alog_ref,
                         xs_ref, b_ref, c_ref, dt_out_ref, dec_ref, nbuf_ref):
    ch = x_ref.shape[1]
    inner = xs_ref.shape[1]
    gn = b_ref.shape[1]
    u = x_ref[...].astype(F32)
    b0, b1, b2 = (buf_ref[:, k * ch:(k + 1) * ch] for k in range(3))
    xbc = _silu(_conv4(u, b2, b1, b0, cw_ref[...], cb_ref[...]))
    xs_ref[...] = xbc[:, 0:inner]
    b_ref[...] = xbc[:, inner:inner + gn]
    c_ref[...] = xbc[:, inner + gn:]
    dt = _softplus(dt_ref[...] + dtb_ref[...])
    dt_out_ref[...] = dt
    dec_ref[...] = jnp.exp(dt * (-jnp.exp(alog_ref[...])))
    nbuf_ref[:, 0:ch] = b1
    nbuf_ref[:, ch:2 * ch] = b2
    nbuf_ref[:, 2 * ch:3 * ch] = u


def _ssd_step_kernel(s_ref, xt_ref, b_ref, c_ref, dt_ref, dec_ref, ns_ref, yt_ref):
    bt, n_heads = s_ref.shape[0], s_ref.shape[1]
    hpg = n_heads // SSM_GROUPS
    lane = lax.broadcasted_iota(jnp.int32, yt_ref.shape[1:], 1)
    for i in range(bt):
        xt = xt_ref[i]
        dt = dt_ref[i]
        dec = dec_ref[i]
        xdt = xt * dt
        s_news = [s_ref[i, h] * dec[:, h:h + 1] + xdt[:, h:h + 1] * b_ref[i, h // hpg]
                  for h in range(n_heads)]
        for h in range(n_heads):
            ns_ref[i, h] = s_news[h]
        ys = [jnp.sum(s_news[h] * c_ref[i, h // hpg], axis=-1, keepdims=True) for h in range(n_heads)]
        yt = jnp.zeros(yt_ref.shape[1:], F32)
        for h in range(n_heads):
            yt = jnp.where(lane == h, ys[h], yt)
        yt_ref[i] = yt


def _ssd_step_post_kernel(y_ref, xs_ref, z_ref, dexp_ref, ng_ref, o_ref):
    o_ref[...] = _mamba_post(y_ref[...], xs_ref[...], z_ref[...].astype(F32), dexp_ref[...],
                             ng_ref[...]).astype(o_ref.dtype)


def ssd_step(z, xbc_raw, dt_raw, conv_buf, ssm0, mp):
    n, conv_ch = xbc_raw.shape
    n_heads, hd, n_state = ssm0.shape[1:]
    inner = n_heads * hd
    gn = SSM_GROUPS * n_state
    xs, bm, cm, dt, dec, nbuf = _rows_call(
        _ssd_step_pre_kernel,
        [xbc_raw, dt_raw, conv_buf.reshape(n, 3 * conv_ch), mp["cw"], mp["cb"], mp["dtb"], mp["alog"]],
        [jax.ShapeDtypeStruct((n, inner), F32), jax.ShapeDtypeStruct((n, gn), F32),
         jax.ShapeDtypeStruct((n, gn), F32), jax.ShapeDtypeStruct((n, n_heads), F32),
         jax.ShapeDtypeStruct((n, n_heads), F32), jax.ShapeDtypeStruct((n, 3 * conv_ch), F32)],
        "ssd_step_pre")
    xt = jnp.transpose(xs.reshape(n, n_heads, hd), (0, 2, 1))
    bt = 4
    st_spec = pl.BlockSpec((bt, n_heads, hd, n_state), lambda i: (i, 0, 0, 0))
    t_spec = pl.BlockSpec((bt, hd, n_heads), lambda i: (i, 0, 0))
    g_spec = pl.BlockSpec((bt, SSM_GROUPS, 1, n_state), lambda i: (i, 0, 0, 0))
    h_spec = pl.BlockSpec((bt, 1, n_heads), lambda i: (i, 0, 0))
    new_ssm, yt = pl.pallas_call(
        _ssd_step_kernel,
        grid=(n // bt,),
        in_specs=[st_spec, t_spec, g_spec, g_spec, h_spec, h_spec],
        out_specs=[st_spec, t_spec],
        out_shape=[jax.ShapeDtypeStruct(ssm0.shape, F32), jax.ShapeDtypeStruct((n, hd, n_heads), F32)],
        compiler_params=_cparams(1),
        name="ssd_step",
    )(ssm0, xt, bm.reshape(n, SSM_GROUPS, 1, n_state), cm.reshape(n, SSM_GROUPS, 1, n_state),
      dt.reshape(n, 1, n_heads), dec.reshape(n, 1, n_heads))
    y = jnp.transpose(yt, (0, 2, 1)).reshape(n, inner)
    (y,) = _rows_call(_ssd_step_post_kernel, [y, xs, z, mp["dexp"], mp["ng"]],
                      [jax.ShapeDtypeStruct((n, inner), BF16)], "ssd_step_post")
    return y, nbuf, new_ssm


def _router_kernel(x_ref, cnt0_ref, g_ref, rwt_ref, rb_ref, xn_ref, idx_ref, gate_ref, rank_ref, cnt_ref,
                   cnt_sc):
    tm = x_ref.shape[0]

    @pl.when(pl.program_id(0) == 0)
    def _():
        cnt_sc[...] = cnt0_ref[...].astype(F32)

    xn = _rms(x_ref[...], g_ref[...], NORM_EPS)
    xn_ref[...] = xn
    logits = _dot_nt(rwt_ref[...], xn, HIGHEST) + rb_ref[...]
    e_iota = lax.broadcasted_iota(jnp.int32, logits.shape, 0)
    m1 = jnp.max(logits, axis=0, keepdims=True)
    i1 = jnp.min(jnp.where(logits == m1, e_iota, N_EXPERTS), axis=0, keepdims=True)
    rest = jnp.where(e_iota == i1, -jnp.inf, logits)
    m2 = jnp.max(rest, axis=0, keepdims=True)
    i2 = jnp.min(jnp.where(rest == m2, e_iota, N_EXPERTS), axis=0, keepdims=True)
    e2 = jnp.exp(m2 - m1)
    denom = 1.0 + e2
    idx_ref[...] = jnp.concatenate([i1, i2], axis=0)
    gate_ref[...] = jnp.concatenate([1.0 / denom, e2 / denom], axis=0)
    oh1 = (e_iota == i1).astype(F32)
    oh2 = (e_iota == i2).astype(F32)
    oh = oh1 + oh2
    ti = lax.broadcasted_iota(jnp.int32, (tm, tm), 0)
    si = lax.broadcasted_iota(jnp.int32, (tm, tm), 1)
    before = _dot(oh.astype(BF16), (ti < si).astype(BF16)) + cnt_sc[:, 0:1]
    rank_ref[...] = jnp.concatenate(
        [jnp.sum(oh1 * before, axis=0, keepdims=True), jnp.sum(oh2 * before, axis=0, keepdims=True)],
        axis=0).astype(jnp.int32)
    cnt_sc[...] = cnt_sc[...] + jnp.sum(oh, axis=1, keepdims=True)
    cnt_ref[...] = cnt_sc[...].astype(jnp.int32)


def moe_router(x, counts0, g, router_w, router_b, tm):
    t_rows, d = x.shape
    return pl.pallas_call(
        _router_kernel,
        grid=(t_rows // tm,),
        in_specs=[pl.BlockSpec((tm, d), lambda i: (i, 0)),
                  pl.BlockSpec((N_EXPERTS, LANES), lambda i: (0, 0)),
                  pl.BlockSpec((1, d), lambda i: (0, 0)),
                  pl.BlockSpec((N_EXPERTS, d), lambda i: (0, 0)),
                  pl.BlockSpec((N_EXPERTS, 1), lambda i: (0, 0))],
        out_specs=[pl.BlockSpec((tm, d), lambda i: (i, 0)),
                   pl.BlockSpec((2, tm), lambda i: (0, i)),
                   pl.BlockSpec((2, tm), lambda i: (0, i)),
                   pl.BlockSpec((2, tm), lambda i: (0, i)),
                   pl.BlockSpec((N_EXPERTS, LANES), lambda i: (0, 0))],
        out_shape=[jax.ShapeDtypeStruct((t_rows, d), F32),
                   jax.ShapeDtypeStruct((2, t_rows), jnp.int32),
                   jax.ShapeDtypeStruct((2, t_rows), F32),
                   jax.ShapeDtypeStruct((2, t_rows), jnp.int32),
                   jax.ShapeDtypeStruct((N_EXPERTS, LANES), jnp.int32)],
        scratch_shapes=[pltpu.VMEM((N_EXPERTS, LANES), F32)],
        compiler_params=_cparams(1),
        name="moe_router",
    )(x, counts0, g.reshape(1, d), router_w.T, router_b.reshape(N_EXPERTS, 1))


def _row_copy(src_ref, src_row, dst_ref, dst_row, sem):
    return pltpu.make_async_copy(src_ref.at[pl.ds(src_row, 1), :], dst_ref.at[pl.ds(dst_row, 1), :], sem)


def _dispatch_kernel(dest_ref, x_ref, slots_in_ref, slots_ref, sem, *, t_rows, row0):
    del slots_in_ref
    tm = x_ref.shape[0]
    base = row0 + pl.program_id(0) * tm

    def copies(r):
        return [_row_copy(x_ref, r, slots_ref, dest_ref[k * t_rows + base + r], sem) for k in range(2)]

    def start(r, carry):
        for k, cp in enumerate(copies(r)):
            cp.start(priority=k)
        return carry

    def wait(r, carry):
        for cp in copies(r):
            cp.wait()
        return carry

    lax.fori_loop(0, tm, start, 0, unroll=DMA_UNROLL)
    lax.fori_loop(0, tm, wait, 0, unroll=DMA_UNROLL)


def moe_dispatch(xn, dest_flat, slots, row0, tm):
    n_rows, d = xn.shape
    return pl.pallas_call(
        functools.partial(_dispatch_kernel, t_rows=dest_flat.shape[0] // 2, row0=row0),
        grid_spec=pltpu.PrefetchScalarGridSpec(
            num_scalar_prefetch=1,
            grid=(n_rows // tm,),
            in_specs=[pl.BlockSpec((tm, d), lambda i, dest: (i, 0)),
                      pl.BlockSpec(memory_space=pl.ANY)],
            out_specs=pl.BlockSpec(memory_space=pl.ANY),
            scratch_shapes=[pltpu.SemaphoreType.DMA(())]),
        out_shape=jax.ShapeDtypeStruct(slots.shape, slots.dtype),
        input_output_aliases={2: 0},
        compiler_params=_cparams(1),
        name="moe_dispatch",
    )(dest_flat, xn, slots)


def _moe_kernel(te_ref, tv_ref, x_ref, wg_ref, wu_ref, wd_ref, o_ref, xb_sc, acc_sc):
    i, j = pl.program_id(0), pl.program_id(1)
    tm = x_ref.shape[0]

    @pl.when(j == 0)
    def _():
        xb_sc[...] = x_ref[...].astype(BF16)
        acc_sc[...] = jnp.zeros_like(acc_sc)

    def swiglu_rows(rows):
        xb = xb_sc[0:rows, :]
        hg = _dot(xb, wg_ref[0].astype(BF16))
        hu = _dot(xb, wu_ref[0].astype(BF16))
        h = (_silu(hg) * hu).astype(BF16)
        acc_sc[0:rows, :] += _dot(h, wd_ref[0].astype(BF16))

    pl.when(tv_ref[i] == 2)(functools.partial(swiglu_rows, tm))
    pl.when(tv_ref[i] == 1)(functools.partial(swiglu_rows, tm // 2))

    @pl.when(j == pl.num_programs(1) - 1)
    def _():
        o_ref[...] = acc_sc[...]


def moe_experts(slots, tile_expert, tile_valid, wg, wu, wd, tm, tf):
    n_slots, d = slots.shape
    d_ff = wg.shape[2]
    n_f = d_ff // tf

    def f_idx(i, j, te, tv):
        return jnp.where(tv[i] > 0, j, n_f - 1)

    return pl.pallas_call(
        _moe_kernel,
        grid_spec=pltpu.PrefetchScalarGridSpec(
            num_scalar_prefetch=2,
            grid=(n_slots // tm, n_f),
            in_specs=[pl.BlockSpec((tm, d), lambda i, j, te, tv: (i, 0)),
                      pl.BlockSpec((1, d, tf), lambda i, j, te, tv: (te[i], 0, f_idx(i, j, te, tv))),
                      pl.BlockSpec((1, d, tf), lambda i, j, te, tv: (te[i], 0, f_idx(i, j, te, tv))),
                      pl.BlockSpec((1, tf, d), lambda i, j, te, tv: (te[i], f_idx(i, j, te, tv), 0))],
            out_specs=pl.BlockSpec((tm, d), lambda i, j, te, tv: (i, 0)),
            scratch_shapes=[pltpu.VMEM((tm, d), BF16), pltpu.VMEM((tm, d), F32)]),
        out_shape=jax.ShapeDtypeStruct((n_slots, d), F32),
        compiler_params=_cparams(2),
        name="moe_experts",
    )(tile_expert, tile_valid, slots, wg, wu, wd)


def _combine_kernel(dest_ref, x_ref, gates_ref, g_ref, y_hbm_ref, o_ref, buf_sc, sem, *, t_rows, row0):
    tm = x_ref.shape[0]
    base = row0 + pl.program_id(0) * tm

    def copies(r):
        return [_row_copy(y_hbm_ref, dest_ref[k * t_rows + base + r], buf_sc.at[k], r, sem)
                for k in range(2)]

    def start(r, carry):
        for k, cp in enumerate(copies(r)):
            cp.start(priority=k)
        return carry

    def wait(r, carry):
        for cp in copies(r):
            cp.wait()
        return carry

    lax.fori_loop(0, tm, start, 0, unroll=DMA_UNROLL)
    lax.fori_loop(0, tm, wait, 0, unroll=DMA_UNROLL)
    gates = gates_ref[...]
    out = x_ref[...] + (gates[:, 0:1] * buf_sc[0] + gates[:, 1:2] * buf_sc[1])
    o_ref[...] = _rms(out, g_ref[...], NORM_EPS)


def moe_combine(x, gates_col, dest_flat, y_slots, g_final, tm, row0):
    n_rows, d = x.shape
    t_rows = gates_col.shape[0]
    assert row0 % tm == 0 and n_rows % tm == 0
    blk0 = row0 // tm
    return pl.pallas_call(
        functools.partial(_combine_kernel, t_rows=t_rows, row0=row0),
        grid_spec=pltpu.PrefetchScalarGridSpec(
            num_scalar_prefetch=1,
            grid=(n_rows // tm,),
            in_specs=[pl.BlockSpec((tm, d), lambda i, dest: (i, 0)),
                      pl.BlockSpec((tm, 2), lambda i, dest: (i + blk0, 0)),
                      pl.BlockSpec((1, d), lambda i, dest: (0, 0)),
                      pl.BlockSpec(memory_space=pl.ANY)],
            out_specs=pl.BlockSpec((tm, d), lambda i, dest: (i, 0)),
            scratch_shapes=[pltpu.VMEM((2, tm, d), F32), pltpu.SemaphoreType.DMA(())]),
        out_shape=jax.ShapeDtypeStruct((n_rows, d), F32),
        compiler_params=_cparams(1),
        name="moe_combine",
    )(dest_flat, x, gates_col, g_final.reshape(1, d), y_slots)


def moe_final(x_prompt, x_sample, g_ffn, g_final, router_w, router_b, wg, wu, wd):
    n_prompt, n_sample = x_prompt.shape[0], x_sample.shape[0]
    t_rows = n_prompt + n_sample
    route = lambda x, counts0: moe_router(x, counts0, g_ffn, router_w, router_b,
                                          _row_tile(x.shape[0], ROUTER_TILE, LANES))
    xn_p, idx_p, gates_p, rank_p, counts_p = route(x_prompt, jnp.zeros((N_EXPERTS, LANES), jnp.int32))
    xn_s, idx_s, gates_s, rank_s, counts = route(x_sample, counts_p)
    idx, gates, rank = (jnp.concatenate(pair, axis=1) for pair in
                        ((idx_p, idx_s), (gates_p, gates_s), (rank_p, rank_s)))
    counts = counts[:, 0]
    tm = MOE_TILE
    n_tiles = -(-2 * t_rows // tm) + N_EXPERTS
    padded = (counts + tm - 1) // tm * tm
    pend = jnp.cumsum(padded)
    pstart = pend - padded
    dest = jnp.sum(jnp.where(idx[:, :, None] == jnp.arange(N_EXPERTS)[None, None, :], pstart[None, None, :], 0),
                   axis=-1) + rank
    dest_flat = dest.reshape(-1).astype(jnp.int32)
    tile_start = jnp.arange(n_tiles, dtype=jnp.int32) * tm
    last_start = jnp.maximum(pend[-1] - tm, 0)
    probe = jnp.minimum(tile_start, last_start)
    tile_expert = jnp.minimum(jnp.sum((pend[None, :] <= probe[:, None]).astype(jnp.int32), axis=1),
                              N_EXPERTS - 1)
    onehot_e = tile_expert[:, None] == jnp.arange(N_EXPERTS)[None, :]
    group_end = jnp.sum(jnp.where(onehot_e, (pstart + counts)[None, :], 0), axis=1)
    live_rows = jnp.clip(group_end - tile_start, 0, tm)
    half = tm // 2
    tile_valid = jnp.where(tile_start < pend[-1], (live_rows + half - 1) // half, 0).astype(jnp.int32)
    slots = jnp.zeros((n_tiles * tm, x_prompt.shape[1]), F32)
    slots = moe_dispatch(xn_p, dest_flat, slots, 0, _row_tile(n_prompt, GATHER_TILE))
    slots = moe_dispatch(xn_s, dest_flat, slots, n_prompt, _row_tile(n_sample, GATHER_TILE))
    y_slots = moe_experts(slots, tile_expert, tile_valid, wg, wu, wd, tm, MOE_FF_TILE)
    gates_col = gates.T
    y_prompt = moe_combine(x_prompt, gates_col, dest_flat, y_slots, g_final,
                           _row_tile(n_prompt, COMBINE_TILE), 0)
    y_sample = moe_combine(x_sample, gates_col, dest_flat, y_slots, g_final, n_sample, n_prompt)
    return y_prompt, y_sample


def _block_diag(w):
    h, i, j = w.shape
    eye = jnp.eye(h, dtype=w.dtype)
    return jnp.einsum("hij,hg->higj", w, eye).reshape(h * i, h * j)


def kernel(x_prompt, x_sample, state_lru_conv, state_lru_h, state_rwkv_shift, state_rwkv_wkv, state_ssm_conv, state_ssm, norm_mix, norm_ffn, norm_final, w_in0, lru_conv_w, lru_conv_b, lru_wa, lru_ba, lru_wx, lru_bx, lru_lambda, rwkv_mu, rwkv_w0, rwkv_w_decay_up, rwkv_a0, rwkv_w_iclr_up, rwkv_w_gate_up, rwkv_k_k, rwkv_k_a, rwkv_r_k, rwkv_ln_w, rwkv_ln_b, w_out0, ffn_wg, ffn_wu, ffn_wd, w_in1, ssm_conv_w, ssm_conv_b, ssm_dt_bias, ssm_a_log, ssm_d, ssm_norm_g, w_out1, router_w, router_b, moe_wg, moe_wu, moe_wd):
    nb, seq, d = x_prompt.shape
    ns = x_sample.shape[0]
    tp = nb * seq
    lru_w = lru_conv_w.shape[-1]
    rw_w = rwkv_w0.shape[-1]
    shift_cols = rwkv_mu.shape[-1]
    n_rheads = rw_w // RWKV_HEAD
    inner = ssm_norm_g.shape[-1]
    n_sheads = ssm_a_log.shape[-1]
    conv_ch = ssm_conv_w.shape[-1]

    xp, xs = x_prompt.reshape(tp, d), x_sample.reshape(ns, d)
    tmp, tms = (_row_tile(n, TOKEN_TILE, 2 * SUBLANES) for n in (tp, ns))

    row = lambda v: v.reshape(1, -1)
    lp = dict(cw=lru_conv_w[0], cb=row(lru_conv_b[0]), wa=_block_diag(lru_wa[0]).astype(BF16),
              ba=row(lru_ba[0]), wx=_block_diag(lru_wx[0]).astype(BF16), bx=row(lru_bx[0]),
              lam=row(lru_lambda[0]))
    hsum = _block_diag(jnp.ones((n_rheads, RWKV_HEAD, RWKV_HEAD), BF16))
    rp = dict(mu=row(rwkv_mu[0]), w0=row(rwkv_w0[0]), wdec=rwkv_w_decay_up[0], a0=row(rwkv_a0[0]),
              wiclr=rwkv_w_iclr_up[0], wgate=rwkv_w_gate_up[0], kk=row(rwkv_k_k[0]), ka=row(rwkv_k_a[0]),
              rk=row(rwkv_r_k[0]), lnw=row(rwkv_ln_w[0]), lnb=row(rwkv_ln_b[0]), hsum=hsum)
    mp = dict(cw=ssm_conv_w[0], cb=row(ssm_conv_b[0]), dtb=row(ssm_dt_bias[0]), alog=row(ssm_a_log[0]),
              dexp=row(jnp.repeat(ssm_d[0], SSM_HEAD)), ng=row(ssm_norm_g[0]))

    w_in0_rwkv = w_in0[0][:, 2 * lru_w:]

    def in0(x, tm):
        return (norm_matmul(x, norm_mix[0], w_in0[0], 0, 1, 2 * lru_w, tm, "in0_lru"),
                norm_matmul(x, norm_mix[0], w_in0_rwkv, 0, 1, shift_cols, tm, "in0_rwkv"))

    proj_lru_p, proj_rwkv_p = in0(xp, tmp)
    proj_lru_s, s_p_rwkv = in0(xs, tms)
    zeros = lambda *shape: jnp.zeros(shape, F32)
    out_a_p, p_lru_conv, p_lru_h = lru_prompt(proj_lru_p, nb, seq, zeros(nb, 3, lru_w), zeros(nb, lru_w), lp)
    out_b_p, p_shift, p_wkv = rwkv_prompt(proj_rwkv_p, nb, seq, zeros(nb, shift_cols),
                                          zeros(nb, n_rheads, RWKV_HEAD, RWKV_HEAD), rp)
    out_a_s, s_lru_conv, s_lru_h = lru_step(proj_lru_s[:, :lru_w], proj_lru_s[:, lru_w:], state_lru_conv[0],
                                            state_lru_h[0], lp)
    out_b_s, s_wkv = rwkv_step(s_p_rwkv, state_rwkv_shift[0], state_rwkv_wkv[0], rp)

    def mix0_ffn(out_a, out_b, x, tm):
        x = matmul_residual([out_a, out_b], w_out0[0], x, tm, d, "out0")
        return ffn_residual(x, norm_ffn[0], ffn_wg[0], ffn_wu[0], ffn_wd[0], tm, FFN_FF_TILE)

    xp = mix0_ffn(out_a_p, out_b_p, xp, tmp)
    xs = mix0_ffn(out_a_s, out_b_s, xs, tms)

    tn1 = 1536
    w_in1_dt = w_in1[0][:, inner + conv_ch:]

    def in1(x, tm):
        return (norm_matmul(x, norm_mix[1], w_in1[0], 0, (inner + conv_ch) // tn1, tn1, tm, "in1_main", BF16),
                norm_matmul(x, norm_mix[1], w_in1_dt, 0, 1, n_sheads, tm, "in1_dt"))

    proj1_p, dt_p = in1(xp, tmp)
    proj1_s, dt_s = in1(xs, tms)
    y_mix_p, p_ssm_conv, p_ssm = ssd_prompt(proj1_p, dt_p, nb, seq, zeros(nb, 3, conv_ch),
                                            zeros(nb, n_sheads, SSM_HEAD, SSM_STATE), mp)
    y_mix_s, s_ssm_conv, s_ssm = ssd_step(proj1_s[:, :inner], proj1_s[:, inner:], dt_s,
                                          state_ssm_conv[0], state_ssm[0], mp)
    xp = matmul_residual([y_mix_p], w_out1[0], xp, tmp, d, "out1")
    xs = matmul_residual([y_mix_s], w_out1[0], xs, tms, d, "out1")
    y_p, y_s = moe_final(xp, xs, norm_ffn[1], norm_final, router_w[0], router_b[0], moe_wg[0], moe_wu[0],
                         moe_wd[0])

    return (y_p.reshape(nb, seq, d), y_s.reshape(ns, 1, d),
            p_lru_conv[None], p_lru_h.reshape(1, nb, lru_w), p_shift.reshape(1, nb, shift_cols), p_wkv[None],
            p_ssm_conv[None], p_ssm[None],
            s_lru_conv.reshape(1, ns, 3, lru_w), s_lru_h[None], s_p_rwkv[None], s_wkv[None],
            s_ssm_conv.reshape(1, ns, 3, conv_ch), s_ssm[None])
```

```python
import functools

import jax
import jax.numpy as jnp
from jax import lax
from jax.experimental import pallas as pl
from jax.experimental.pallas import tpu as pltpu

F32 = jnp.float32
BF16 = jnp.bfloat16
HIGHEST = lax.Precision.HIGHEST

NORM_EPS = 1e-6
CONV_W = 4
LRU_HEADS = 8
LRU_C = 8.0
RWKV_HEAD = 64
DECAY_RANK = 64
ICLR_RANK = 64
GATE_RANK = 128
RWKV_GN_EPS = 64e-5
SSM_HEAD = 64
SSM_GROUPS = 8
SSM_STATE = 128
SSM_CHUNK = 128
SSM_NORM_EPS = 1e-5
N_EXPERTS = 8

V7X_VMEM_BYTES = 64 * 1024 * 1024
VMEM_LIMIT = V7X_VMEM_BYTES - 8 * 1024 * 1024
SUBLANES = 8
LANES = 128

RWKV_CHUNK = 64
LRU_CHUNK = 256
MOE_TILE = 1024
MOE_FF_TILE = 512
FFN_FF_TILE = 512
TOKEN_TILE = 1024
ROUTER_TILE = 512
GATHER_TILE = 512
COMBINE_TILE = 512
DMA_UNROLL = 8


def _cparams(n_axes):
    return pltpu.CompilerParams(dimension_semantics=("arbitrary",) * n_axes,
                                vmem_limit_bytes=VMEM_LIMIT)


def _row_tile(n_rows, cap, mult=SUBLANES):
    best = None
    for t in range(mult, min(cap, n_rows) + 1, mult):
        if n_rows % t == 0:
            best = t
    assert best is not None, (n_rows, cap)
    return best


def _dot(a, b, precision=None):
    return jnp.dot(a, b, preferred_element_type=F32, precision=precision)


def _dot_nt(a, b, precision=None):
    return lax.dot_general(a, b, (((1,), (1,)), ((), ())), preferred_element_type=F32,
                           precision=precision)


def _dot_tn(a, b, precision=None):
    return lax.dot_general(a, b, (((0,), (0,)), ((), ())), preferred_element_type=F32,
                           precision=precision)


_NN = (((1,), (0,)), ((), ()))
_NT = (((1,), (1,)), ((), ()))
_TN = (((0,), (0,)), ((), ()))


def _split2(x):
    hi = x.astype(BF16)
    lo = (x - hi.astype(F32)).astype(BF16)
    return hi, lo


def _split3(x):
    hi = x.astype(BF16)
    r1 = x - hi.astype(F32)
    mid = r1.astype(BF16)
    lo = (r1 - mid.astype(F32)).astype(BF16)
    return hi, mid, lo


def _dg(a, b, dims):
    return lax.dot_general(a, b, dims, preferred_element_type=F32)


def _dot3(a2, b2, dims=_NN):
    (ah, al), (bh, bl) = a2, b2
    return _dg(ah, bh, dims) + _dg(al, bh, dims) + _dg(ah, bl, dims)


def _dot_exact_rhs(x, m_bf16, terms=3):
    return sum(_dg(part, m_bf16, _NN) for part in (_split3(x) if terms == 3 else _split2(x)))


def _dot_exact_lhs(m_bf16, x):
    return sum(_dg(m_bf16, part, _NN) for part in _split3(x))


def _softplus(x):
    return jnp.maximum(x, 0.0) + jnp.log1p(jnp.exp(-jnp.abs(x)))


def _silu(x):
    return x * jax.nn.sigmoid(x)


def _gelu_tanh(x):
    return 0.5 * x * (1.0 + jnp.tanh(0.7978845608028654 * (x + 0.044715 * (x * x * x))))


def _rms(x, g, eps):
    return x * lax.rsqrt(jnp.mean(x * x, axis=-1, keepdims=True) + eps) * g


def _norm_mm_kernel(x_ref, g_ref, w_ref, o_ref, xn_sc):
    @pl.when(pl.program_id(1) == 0)
    def _():
        xn_sc[...] = _rms(x_ref[...], g_ref[...], NORM_EPS).astype(BF16)

    o_ref[...] = _dot(xn_sc[...], w_ref[...].astype(BF16)).astype(o_ref.dtype)


def norm_matmul(x, g, w, col0_blk, n_blk, tn, tm, name, out_dtype=F32):
    t_rows, d = x.shape
    return pl.pallas_call(
        _norm_mm_kernel,
        grid=(t_rows // tm, n_blk),
        in_specs=[pl.BlockSpec((tm, d), lambda i, j: (i, 0)),
                  pl.BlockSpec((1, d), lambda i, j: (0, 0)),
                  pl.BlockSpec((d, tn), lambda i, j: (0, j + col0_blk))],
        out_specs=pl.BlockSpec((tm, tn), lambda i, j: (i, j)),
        out_shape=jax.ShapeDtypeStruct((t_rows, n_blk * tn), out_dtype),
        scratch_shapes=[pltpu.VMEM((tm, d), BF16)],
        compiler_params=_cparams(2),
        name=name,
    )(x, g.reshape(1, d), w)


def _mm_res_kernel(*refs, n_in):
    x_refs, w_refs = refs[:n_in], refs[n_in:2 * n_in]
    res_ref, o_ref = refs[2 * n_in], refs[2 * n_in + 1]
    acc = res_ref[...]
    for x_ref, w_ref in zip(x_refs, w_refs):
        acc = acc + _dot(x_ref[...].astype(BF16), w_ref[...].astype(BF16))
    o_ref[...] = acc


def matmul_residual(xs, w, res, tm, tn, name):
    n_in = len(xs)
    t_rows, kp = xs[0].shape
    n_cols = w.shape[1]
    in_specs = [pl.BlockSpec((tm, kp), lambda i, j: (i, 0)) for _ in xs]
    in_specs += [pl.BlockSpec((kp, tn), functools.partial(lambda i, j, p: (p, j), p=p))
                 for p in range(n_in)]
    in_specs += [pl.BlockSpec((tm, tn), lambda i, j: (i, j))]
    return pl.pallas_call(
        functools.partial(_mm_res_kernel, n_in=n_in),
        grid=(t_rows // tm, n_cols // tn),
        in_specs=in_specs,
        out_specs=pl.BlockSpec((tm, tn), lambda i, j: (i, j)),
        out_shape=jax.ShapeDtypeStruct((t_rows, n_cols), F32),
        compiler_params=_cparams(2),
        name=name,
    )(*xs, *([w] * n_in), res)


def _ffn_kernel(x_ref, g_ref, wg_ref, wu_ref, wd_ref, o_ref, xn_sc, acc_sc):
    j = pl.program_id(1)

    @pl.when(j == 0)
    def _():
        xn_sc[...] = _rms(x_ref[...], g_ref[...], NORM_EPS).astype(BF16)
        acc_sc[...] = jnp.zeros_like(acc_sc)

    xn = xn_sc[...]
    hg = _dot(xn, wg_ref[...].astype(BF16))
    hu = _dot(xn, wu_ref[...].astype(BF16))
    h = (_silu(hg) * hu).astype(BF16)
    acc_sc[...] += _dot(h, wd_ref[...].astype(BF16))

    @pl.when(j == pl.num_programs(1) - 1)
    def _():
        o_ref[...] = x_ref[...] + acc_sc[...]


def ffn_residual(x, g, wg, wu, wd, tm, tf):
    t_rows, d = x.shape
    d_ff = wg.shape[1]
    return pl.pallas_call(
        _ffn_kernel,
        grid=(t_rows // tm, d_ff // tf),
        in_specs=[pl.BlockSpec((tm, d), lambda i, j: (i, 0)),
                  pl.BlockSpec((1, d), lambda i, j: (0, 0)),
                  pl.BlockSpec((d, tf), lambda i, j: (0, j)),
                  pl.BlockSpec((d, tf), lambda i, j: (0, j)),
                  pl.BlockSpec((tf, d), lambda i, j: (j, 0))],
        out_specs=pl.BlockSpec((tm, d), lambda i, j: (i, 0)),
        out_shape=jax.ShapeDtypeStruct((t_rows, d), F32),
        scratch_shapes=[pltpu.VMEM((tm, d), BF16), pltpu.VMEM((tm, d), F32)],
        compiler_params=_cparams(2),
        name="ffn_swiglu",
    )(x, g.reshape(1, d), wg, wu, wd)


def _conv4(u, u1, u2, u3, cw, cb):
    return cb + cw[3:4] * u + cw[2:3] * u1 + cw[1:2] * u2 + cw[0:1] * u3


def _lru_gates(xc, wa, ba, wx, bx, lam):
    xb = xc.astype(BF16)
    r = jax.nn.sigmoid(_dot(xb, wa) + ba)
    i = jax.nn.sigmoid(_dot(xb, wx) + bx)
    log_a = -LRU_C * r * _softplus(-lam)
    a = jnp.exp(log_a)
    u = jnp.sqrt(1.0 - jnp.exp(2.0 * log_a)) * (i * xc)
    return a, u


def _lru_prompt_kernel(x_ref, g_ref, conv0_ref, h0_ref, cw_ref, cb_ref, wa_ref, ba_ref, wx_ref,
                       bx_ref, lam_ref, o_ref, nconv_ref, nh_ref, ext_sc, h_sc, *, lc):
    width = x_ref.shape[1]

    @pl.when(pl.program_id(1) == 0)
    def _():
        ext_sc[0:SUBLANES, :] = jnp.zeros((SUBLANES, width), F32)
        ext_sc[SUBLANES - 3:SUBLANES, :] = conv0_ref[0]
        h_sc[...] = h0_ref[0]

    u = x_ref[...]
    ext_sc[SUBLANES:SUBLANES + lc, :] = u
    xc = _conv4(u, ext_sc[SUBLANES - 1:SUBLANES - 1 + lc, :], ext_sc[SUBLANES - 2:SUBLANES - 2 + lc, :],
                ext_sc[SUBLANES - 3:SUBLANES - 3 + lc, :], cw_ref[...], cb_ref[...])
    tail = ext_sc[lc + SUBLANES - 3:lc + SUBLANES, :]
    ext_sc[SUBLANES - 3:SUBLANES, :] = tail
    nconv_ref[0] = tail

    a, h = _lru_gates(xc, wa_ref[...], ba_ref[...], wx_ref[...], bx_ref[...], lam_ref[...])
    row = lax.broadcasted_iota(jnp.int32, (lc, width), 0)
    s = 1
    while s < lc:
        keep = row >= s
        a_sh = jnp.where(keep, pltpu.roll(a, s, 0), 1.0)
        h_sh = jnp.where(keep, pltpu.roll(h, s, 0), 0.0)
        h = a * h_sh + h
        a = a * a_sh
        s *= 2
    hs = h + a * h_sc[...]
    h_last = hs[lc - 1:lc, :]
    h_sc[...] = h_last
    nh_ref[0] = h_last
    o_ref[...] = (hs * _gelu_tanh(g_ref[...])).astype(o_ref.dtype)


def lru_prompt(proj_lru, n_batch, seq, conv0, h0, lp):
    width = conv0.shape[-1]
    lc = min(LRU_CHUNK, seq)
    n_chunks = seq // lc
    full = lambda shape: pl.BlockSpec(shape, lambda b, c: (0,) * len(shape))
    return pl.pallas_call(
        functools.partial(_lru_prompt_kernel, lc=lc),
        grid=(n_batch, n_chunks),
        in_specs=[pl.BlockSpec((lc, width), lambda b, c: (b * n_chunks + c, 0)),
                  pl.BlockSpec((lc, width), lambda b, c: (b * n_chunks + c, 1)),
                  pl.BlockSpec((1, 3, width), lambda b, c: (b, 0, 0)),
                  pl.BlockSpec((1, 1, width), lambda b, c: (b, 0, 0)),
                  full((CONV_W, width)), full((1, width)), full((width, width)), full((1, width)),
                  full((width, width)), full((1, width)), full((1, width))],
        out_specs=[pl.BlockSpec((lc, width), lambda b, c: (b * n_chunks + c, 0)),
                   pl.BlockSpec((1, 3, width), lambda b, c: (b, 0, 0)),
                   pl.BlockSpec((1, 1, width), lambda b, c: (b, 0, 0))],
        out_shape=[jax.ShapeDtypeStruct((n_batch * seq, width), BF16),
                   jax.ShapeDtypeStruct((n_batch, 3, width), F32),
                   jax.ShapeDtypeStruct((n_batch, 1, width), F32)],
        scratch_shapes=[pltpu.VMEM((lc + SUBLANES, width), F32), pltpu.VMEM((1, width), F32)],
        compiler_params=_cparams(2),
        name="lru_prompt",
    )(proj_lru, proj_lru, conv0, h0.reshape(n_batch, 1, width), lp["cw"], lp["cb"], lp["wa"], lp["ba"],
      lp["wx"], lp["bx"], lp["lam"])


def _rows_call(body, inputs, out_shapes, name):
    return pl.pallas_call(body, out_shape=out_shapes, name=name,
                          compiler_params=pltpu.CompilerParams(vmem_limit_bytes=VMEM_LIMIT))(*inputs)


def _lru_step_kernel(x_ref, g_ref, buf_ref, h0_ref, cw_ref, cb_ref, wa_ref, ba_ref, wx_ref, bx_ref,
                     lam_ref, o_ref, nbuf_ref, nh_ref):
    width = x_ref.shape[1]
    u = x_ref[...]
    b0, b1, b2 = (buf_ref[:, k * width:(k + 1) * width] for k in range(3))
    xc = _conv4(u, b2, b1, b0, cw_ref[...], cb_ref[...])
    a, uu = _lru_gates(xc, wa_ref[...], ba_ref[...], wx_ref[...], bx_ref[...], lam_ref[...])
    h = a * h0_ref[...] + uu
    nh_ref[...] = h
    o_ref[...] = (h * _gelu_tanh(g_ref[...])).astype(o_ref.dtype)
    nbuf_ref[:, 0:width] = b1
    nbuf_ref[:, width:2 * width] = b2
    nbuf_ref[:, 2 * width:3 * width] = u


def lru_step(x_lru, g_lru, conv_buf, h0, lp):
    n, width = x_lru.shape
    return _rows_call(
        _lru_step_kernel,
        [x_lru, g_lru, conv_buf.reshape(n, 3 * width), h0, lp["cw"], lp["cb"], lp["wa"], lp["ba"],
         lp["wx"], lp["bx"], lp["lam"]],
        [jax.ShapeDtypeStruct((n, width), BF16), jax.ShapeDtypeStruct((n, 3 * width), F32),
         jax.ShapeDtypeStruct((n, width), F32)],
        "lru_step")


def _rwkv_rows(mixed, rp):
    w = rp["w0"].shape[1]
    r, k, v = mixed[:, 0:w], mixed[:, w:2 * w], mixed[:, 2 * w:3 * w]
    o = 3 * w
    wd = mixed[:, o:o + DECAY_RANK]
    ad = mixed[:, o + DECAY_RANK:o + DECAY_RANK + ICLR_RANK]
    gd = mixed[:, o + DECAY_RANK + ICLR_RANK:o + DECAY_RANK + ICLR_RANK + GATE_RANK]
    dec_in = rp["w0"] + _dot(jnp.tanh(wd).astype(BF16), rp["wdec"].astype(BF16))
    w_log = -_softplus(-dec_in) - 0.5
    lw = -jnp.exp(w_log)
    iclr = jax.nn.sigmoid(rp["a0"] + _dot(ad.astype(BF16), rp["wiclr"].astype(BF16)))
    gate = _dot(jax.nn.sigmoid(gd).astype(BF16), rp["wgate"].astype(BF16))
    kk = k * rp["kk"]
    ss = _dot_exact_rhs(kk * kk, rp["hsum"])
    kkn = kk / jnp.maximum(jnp.sqrt(ss), 1e-12)
    k2 = k * (1.0 + (iclr - 1.0) * rp["ka"])
    return r, k2, v, lw, -kkn, kkn * iclr, gate


def _rwkv_post(o, r, k2, v, gate, rp):
    inv = 1.0 / RWKV_HEAD
    mu = _dot_exact_rhs(o, rp["hsum"], 2) * inv
    d = o - mu
    var = _dot_exact_rhs(d * d, rp["hsum"], 2) * inv
    on = d * lax.rsqrt(var + RWKV_GN_EPS) * rp["lnw"] + rp["lnb"]
    bonus = _dot_exact_rhs(r * k2 * rp["rk"], rp["hsum"], 2) * v
    return (on + bonus) * gate


_RWKV_PARAM_NAMES = ("mu", "w0", "wdec", "a0", "wiclr", "wgate", "kk", "ka", "rk", "lnw", "lnb", "hsum")


def _rwkv_prompt_kernel(*refs, chunk, nbs):
    p_refs, shift0_ref, s0_ref, rest = refs[:nbs], refs[nbs], refs[nbs + 1], refs[nbs + 2:]
    n_prm = len(_RWKV_PARAM_NAMES)
    rp = {name: ref[...] for name, ref in zip(_RWKV_PARAM_NAMES, rest[:n_prm])}
    o_ref, nshift_ref, nwkv_ref, prev_sc, s_sc, o_sc = rest[n_prm:]
    n_pairs = s_sc.shape[0] // nbs
    hd = RWKV_HEAD
    assert chunk == hd, "the pair-packed layout below uses chunk == head size"

    @pl.when(pl.program_id(1) == 0)
    def _():
        for i in range(nbs):
            prev_sc[i] = shift0_ref[i]
            for p in range(n_pairs):
                s_sc[i * n_pairs + p] = jnp.concatenate([s0_ref[i, 2 * p], s0_ref[i, 2 * p + 1]], axis=0).T

    ti = lax.broadcasted_iota(jnp.int32, (chunk, chunk), 0)
    si = lax.broadcasted_iota(jnp.int32, (chunk, chunk), 1)
    lower = (ti >= si).astype(BF16)

    def prep(i):
        p = p_refs[i][...]
        row = lax.broadcasted_iota(jnp.int32, p.shape, 0)
        p_prev = jnp.where(row >= 1, pltpu.roll(p, 1, 0), prev_sc[i])
        last = p[chunk - 1:chunk, :]
        prev_sc[i] = last
        nshift_ref[i] = last
        mixed = p + (p_prev - p) * rp["mu"]
        r, k2, v, lw, a, b, gate = _rwkv_rows(mixed, rp)
        cs = _dot_exact_lhs(lower, lw)
        g_in, g_ex, g_inv = jnp.exp(cs), jnp.exp(cs - lw), jnp.exp(-cs)
        bt, kt = b * g_inv, k2 * g_inv
        g_end = g_in[chunk - 1:chunk, :]
        return dict(r=r, k2=k2, v=v, gate=gate, at=a * g_ex, bt=bt, kt=kt, rt=r * g_in, g_end=g_end,
                    bc=bt * g_end, kc=kt * g_end)

    preps = [prep(i) for i in range(nbs)]
    n_fac = max(1, (chunk - 1).bit_length())
    pw_ = 2 * hd
    lane1 = lax.broadcasted_iota(jnp.int32, (1, pw_), 1)
    lo1 = lane1 < hd
    lo2 = jnp.concatenate([lo1, lo1], axis=1)
    ti2 = lax.broadcasted_iota(jnp.int32, (chunk, pw_), 0)
    si2 = lax.broadcasted_iota(jnp.int32, (chunk, pw_), 1) % hd
    strict2, incl2 = ti2 > si2, ti2 >= si2

    def bdiag(x, lo):
        return jnp.concatenate([jnp.where(lo, x, 0), jnp.where(lo, 0, x)], axis=0)

    def bdiag2(x2, lo):
        return bdiag(x2[0], lo), bdiag(x2[1], lo)

    units = [(i, q) for i in range(nbs) for q in range(n_pairs)]
    pairs = range(len(units))
    pls = [slice(q * pw_, (q + 1) * pw_) for _, q in units]
    a_p, r_p, v_p, bt_p, kt_p, bc_p, kc_p, ge_p = (
        [preps[i][name][:, pls[u]] for u, (i, _) in enumerate(units)]
        for name in ("at", "rt", "v", "bt", "kt", "bc", "kc", "g_end"))
    mas, mrs, v2s = [], [], []
    for p in pairs:
        b2, k2_ = _split2(bt_p[p]), _split2(kt_p[p])
        rhs_rows = tuple(jnp.concatenate([bdiag(b2[i], lo1), bdiag(k2_[i], lo1)], axis=0) for i in range(2))
        mas.append(_dot3(_split2(a_p[p]), rhs_rows, _NT))
        mrs.append(_dg(r_p[p].astype(BF16), rhs_rows[0], _NT))
        v2s.append(bdiag2(_split2(v_p[p]), lo1))
    labs = [jnp.where(strict2, m[:, :pw_], 0.0) for m in mas]
    lrbs = [jnp.where(incl2, m[:, :pw_], 0.0).astype(BF16) for m in mrs]
    lakvs = [_dot3(_split2(jnp.where(strict2, mas[p][:, pw_:], 0.0)), v2s[p]) for p in pairs]
    lrkvs = [_dg(jnp.where(incl2, mrs[p][:, pw_:], 0.0).astype(BF16), v2s[p][0], _NN) for p in pairs]
    ys = [jnp.concatenate([a_p[p], lakvs[p]], axis=1) for p in pairs]
    pws = [_split2(lab) for lab in labs]
    for f in range(n_fac):
        ys = [ys[p] + _dot3(pws[p], bdiag2(_split2(ys[p]), lo2)) for p in pairs]
        if f + 1 < n_fac:
            pws = [_split2(_dot3(pw, bdiag2(pw, lo1))) for pw in pws]
    y2s = [_split2(y) for y in ys]
    qos = [jnp.concatenate([r_p[p], lrkvs[p]], axis=1) + _dg(lrbs[p], bdiag(y2s[p][0], lo2), _NN)
           for p in pairs]
    xs_ = [_dot3(_split2(bc_p[p]), y2s[p], _TN) for p in pairs]
    kvs = [_dot3(_split2(kc_p[p]), _split2(v_p[p]), _TN) for p in pairs]
    mts = [_split2(jnp.where(lo1, xs_[p][:hd, :pw_], xs_[p][hd:, :pw_]) + jnp.where(ti2 == si2, ge_p[p], 0.0))
           for p in pairs]
    n0s = [jnp.where(lo1, xs_[p][:hd, pw_:] + kvs[p][:hd], xs_[p][hd:, pw_:] + kvs[p][hd:]) for p in pairs]
    s0s = [bdiag2(_split2(s_sc[p]), lo1) for p in pairs]
    for p, (i, _) in enumerate(units):
        o_sc[i, :, pls[p]] = _dg(qos[p][:, :pw_].astype(BF16), s0s[p][0], _NN) + qos[p][:, pw_:]
    for p in pairs:
        s_sc[p] = _dot3(mts[p], s0s[p]) + n0s[p]

    @pl.when(pl.program_id(1) == pl.num_programs(1) - 1)
    def _():
        for p, (i, q) in enumerate(units):
            s_pair = s_sc[p].T
            nwkv_ref[i, 2 * q] = s_pair[:hd]
            nwkv_ref[i, 2 * q + 1] = s_pair[hd:]

    for i, pre in enumerate(preps):
        o_ref[i] = _rwkv_post(o_sc[i], pre["r"], pre["k2"], pre["v"], pre["gate"], rp).astype(o_ref.dtype)


def _rwkv_param_list(rp):
    return [rp[name] for name in _RWKV_PARAM_NAMES]


def rwkv_prompt(p_rwkv, n_batch, seq, shift0, wkv0, rp):
    cols = shift0.shape[-1]
    n_heads, hd = wkv0.shape[1], wkv0.shape[2]
    width = n_heads * hd
    chunk = min(RWKV_CHUNK, seq)
    n_chunks = seq // chunk
    prm = _rwkv_param_list(rp)
    prm_specs = [pl.BlockSpec(x.shape, lambda b, c: (0, 0)) for x in prm]
    nbs = max(n for n in (1, 2, 4) if n_batch % n == 0)
    p_specs = [pl.BlockSpec((chunk, cols), functools.partial(
        lambda b, c, i: ((b * nbs + i) * n_chunks + c, 0), i=i)) for i in range(nbs)]
    out, new_shift, new_wkv = pl.pallas_call(
        functools.partial(_rwkv_prompt_kernel, chunk=chunk, nbs=nbs),
        grid=(n_batch // nbs, n_chunks),
        in_specs=p_specs + [pl.BlockSpec((nbs, 1, cols), lambda b, c: (b, 0, 0)),
                            pl.BlockSpec((nbs, n_heads, hd, hd), lambda b, c: (b, 0, 0, 0))] + prm_specs,
        out_specs=[pl.BlockSpec((nbs, chunk, width), lambda b, c: (b, c, 0)),
                   pl.BlockSpec((nbs, 1, cols), lambda b, c: (b, 0, 0)),
                   pl.BlockSpec((nbs, n_heads, hd, hd), lambda b, c: (b, 0, 0, 0))],
        out_shape=[jax.ShapeDtypeStruct((n_batch, seq, width), BF16),
                   jax.ShapeDtypeStruct((n_batch, 1, cols), F32),
                   jax.ShapeDtypeStruct((n_batch, n_heads, hd, hd), F32)],
        scratch_shapes=[pltpu.VMEM((nbs, 1, cols), F32), pltpu.VMEM((nbs * n_heads // 2, hd, 2 * hd), F32),
                        pltpu.VMEM((nbs, chunk, width), F32)],
        compiler_params=_cparams(2),
        name="rwkv_prompt",
    )(*([p_rwkv] * nbs), shift0.reshape(n_batch, 1, cols), wkv0, *prm)
    return out.reshape(n_batch * seq, width), new_shift, new_wkv


def _rwkv_step_pre_kernel(p_ref, prev_ref, *rest):
    n_prm = len(_RWKV_PARAM_NAMES)
    rp = {name: ref[...] for name, ref in zip(_RWKV_PARAM_NAMES, rest[:n_prm])}
    r_ref, k_ref, v_ref, w_ref, a_ref, b_ref, gate_ref = rest[n_prm:]
    p = p_ref[...]
    mixed = p + (prev_ref[...] - p) * rp["mu"]
    r, k2, v, lw, a, b, gate = _rwkv_rows(mixed, rp)
    r_ref[...] = r
    k_ref[...] = k2
    v_ref[...] = v
    w_ref[...] = jnp.exp(lw)
    a_ref[...] = a
    b_ref[...] = b
    gate_ref[...] = gate


def _rwkv_step_kernel(s_ref, w_ref, a_ref, b_ref, k_ref, r_ref, vt_ref, ns_ref, ot_ref):
    bt, n_heads = s_ref.shape[0], s_ref.shape[1]
    lane = lax.broadcasted_iota(jnp.int32, ot_ref.shape[1:], 1)
    for i in range(bt):
        vt = vt_ref[i]
        ss = [s_ref[i, h] for h in range(n_heads)]
        sas = [jnp.sum(ss[h] * a_ref[i, h], axis=-1, keepdims=True) for h in range(n_heads)]
        s_news = [ss[h] * w_ref[i, h] + sas[h] * b_ref[i, h] + vt[:, h:h + 1] * k_ref[i, h]
                  for h in range(n_heads)]
        for h in range(n_heads):
            ns_ref[i, h] = s_news[h]
        os_ = [jnp.sum(s_news[h] * r_ref[i, h], axis=-1, keepdims=True) for h in range(n_heads)]
        ot = jnp.zeros(ot_ref.shape[1:], F32)
        for h in range(n_heads):
            ot = jnp.where(lane == h, os_[h], ot)
        ot_ref[i] = ot


def _rwkv_step_post_kernel(o_ref, r_ref, k_ref, v_ref, gate_ref, *rest):
    n_prm = len(_RWKV_PARAM_NAMES)
    rp = {name: ref[...] for name, ref in zip(_RWKV_PARAM_NAMES, rest[:n_prm])}
    out_ref = rest[n_prm]
    out_ref[...] = _rwkv_post(o_ref[...], r_ref[...], k_ref[...], v_ref[...], gate_ref[...],
                              rp).astype(out_ref.dtype)


def rwkv_step(p_rwkv, shift_prev, wkv0, rp):
    n = p_rwkv.shape[0]
    n_heads, hd = wkv0.shape[1], wkv0.shape[2]
    width = n_heads * hd
    prm = _rwkv_param_list(rp)
    row = jax.ShapeDtypeStruct((n, width), F32)
    r, k2, v, w, a, b, gate = _rows_call(_rwkv_step_pre_kernel, [p_rwkv, shift_prev, *prm], [row] * 7,
                                         "rwkv_step_pre")
    hrow = lambda z: z.reshape(n, n_heads, 1, hd)
    vt = jnp.transpose(v.reshape(n, n_heads, hd), (0, 2, 1))
    bt = SUBLANES
    vec_spec = pl.BlockSpec((bt, n_heads, 1, hd), lambda i: (i, 0, 0, 0))
    st_spec = pl.BlockSpec((bt, n_heads, hd, hd), lambda i: (i, 0, 0, 0))
    t_spec = pl.BlockSpec((bt, hd, n_heads), lambda i: (i, 0, 0))
    new_wkv, ot = pl.pallas_call(
        _rwkv_step_kernel,
        grid=(n // bt,),
        in_specs=[st_spec] + [vec_spec] * 5 + [t_spec],
        out_specs=[st_spec, t_spec],
        out_shape=[jax.ShapeDtypeStruct(wkv0.shape, F32), jax.ShapeDtypeStruct((n, hd, n_heads), F32)],
        compiler_params=_cparams(1),
        name="rwkv_step",
    )(wkv0, hrow(w), hrow(a), hrow(b), hrow(k2), hrow(r), vt)
    o = jnp.transpose(ot, (0, 2, 1)).reshape(n, width)
    (out_b,) = _rows_call(_rwkv_step_post_kernel, [o, r, k2, v, gate, *prm],
                          [jax.ShapeDtypeStruct((n, width), BF16)], "rwkv_step_post")
    return out_b, new_wkv


def _mamba_post(y, xs, z, dexp, ng):
    y = (y + dexp * xs) * _silu(z)
    gw = y.shape[1] // SSM_GROUPS
    parts = []
    for g in range(SSM_GROUPS):
        yg = y[:, g * gw:(g + 1) * gw]
        parts.append(yg * lax.rsqrt(jnp.mean(yg * yg, axis=-1, keepdims=True) + SSM_NORM_EPS))
    return jnp.concatenate(parts, axis=1) * ng


def _ssd_prompt_kernel(z_ref, xlo_ref, xhi_ref, dt_ref, conv0_ref, s0_ref, cw_ref, cb_ref, dtb_ref,
                       alog_ref, dexp_ref, ng_ref, hexp_ref, qexp_ref, y_ref, nconv_ref, nssm_ref,
                       ext_sc, s_sc, y_sc, yo_sc, *, q):
    inner = xlo_ref.shape[1]
    n_heads = s_sc.shape[0]
    hpg = n_heads // SSM_GROUPS

    @pl.when(pl.program_id(1) == 0)
    def _():
        ext_sc[0:SUBLANES, :] = jnp.zeros((SUBLANES, ext_sc.shape[1]), F32)
        ext_sc[SUBLANES - 3:SUBLANES, :] = conv0_ref[0]
        s_sc[...] = s0_ref[0]

    ext_sc[SUBLANES:SUBLANES + q, 0:inner] = xlo_ref[...].astype(F32)
    ext_sc[SUBLANES:SUBLANES + q, inner:] = xhi_ref[...].astype(F32)
    xbc = _silu(_conv4(ext_sc[SUBLANES:SUBLANES + q, :], ext_sc[SUBLANES - 1:SUBLANES - 1 + q, :],
                       ext_sc[SUBLANES - 2:SUBLANES - 2 + q, :], ext_sc[SUBLANES - 3:SUBLANES - 3 + q, :],
                       cw_ref[...], cb_ref[...]))
    tail = ext_sc[q + SUBLANES - 3:q + SUBLANES, :]
    ext_sc[SUBLANES - 3:SUBLANES, :] = tail
    nconv_ref[0] = tail

    xs = xbc[:, 0:inner]
    gn = SSM_GROUPS * SSM_STATE
    bm = xbc[:, inner:inner + gn].astype(BF16)
    cm = xbc[:, inner + gn:].astype(BF16)
    dt = _softplus(dt_ref[...] + dtb_ref[...])
    dta = dt * (-jnp.exp(alog_ref[...]))
    ti = lax.broadcasted_iota(jnp.int32, (q, q), 0)
    si = lax.broadcasted_iota(jnp.int32, (q, q), 1)
    causal = ti >= si
    da = _dot_exact_lhs(causal.astype(BF16), dta)
    upper = (ti <= si).astype(BF16)
    da_t = sum(_dg(part, upper, _TN) for part in _split3(dta))
    da_end = da[q - 1:q, :]
    end_decay = jnp.exp(da_end)
    hexp, qexp = hexp_ref[...], qexp_ref[...]
    xdt = xs * _dot_exact_rhs(dt, hexp, 2)
    xdt_b = xdt.astype(BF16)
    xdt_end = (xdt * _dot_exact_rhs(jnp.exp(da_end - da), hexp, 2)).astype(BF16)
    from_start = _dot_exact_rhs(jnp.exp(da), hexp, 2)
    da_col = _dot_exact_rhs(da, qexp)

    heads = range(n_heads)
    bgs = [bm[:, g * SSM_STATE:(g + 1) * SSM_STATE] for g in range(SSM_GROUPS)]
    cgs = [cm[:, g * SSM_STATE:(g + 1) * SSM_STATE] for g in range(SSM_GROUPS)]
    scores = [_dot_nt(cgs[g], bgs[g]) for g in range(SSM_GROUPS)]
    hss = [slice(h * SSM_HEAD, (h + 1) * SSM_HEAD) for h in heads]
    s_hs = [s_sc[h] for h in heads]
    for h in heads:
        yo_sc[:, hss[h]] = _dot_nt(cgs[h // hpg], s_hs[h].astype(BF16))
    wts = [(scores[h // hpg]
            * jnp.exp(jnp.where(causal, da_col[:, h * q:(h + 1) * q] - da_t[h:h + 1, :], -jnp.inf))
            ).astype(BF16) for h in heads]
    for h in heads:
        y_sc[:, hss[h]] = _dot(wts[h], xdt_b[:, hss[h]])
    upd = [_dot_tn(xdt_end[:, hss[h]], bgs[h // hpg]) for h in heads]
    for h in heads:
        s_sc[h] = s_hs[h] * end_decay[:, h:h + 1] + upd[h]

    @pl.when(pl.program_id(1) == pl.num_programs(1) - 1)
    def _():
        nssm_ref[0] = s_sc[...]

    y = y_sc[...] + yo_sc[...] * from_start
    y_ref[...] = _mamba_post(y, xs, z_ref[...].astype(F32), dexp_ref[...], ng_ref[...]).astype(y_ref.dtype)


def ssd_prompt(proj1, dt_raw, n_batch, seq, conv0, ssm0, mp):
    n_heads, hd, n_state = ssm0.shape[1:]
    inner = n_heads * hd
    conv_ch = conv0.shape[-1]
    q = SSM_CHUNK if seq % SSM_CHUNK == 0 else seq
    n_chunks = seq // q
    full = lambda shape: pl.BlockSpec(shape, lambda b, c: (0,) * len(shape))
    return pl.pallas_call(
        functools.partial(_ssd_prompt_kernel, q=q),
        grid=(n_batch, n_chunks),
        in_specs=[pl.BlockSpec((q, inner), lambda b, c: (b * n_chunks + c, 0)),
                  pl.BlockSpec((q, inner), lambda b, c: (b * n_chunks + c, 1)),
                  pl.BlockSpec((q, inner), lambda b, c: (b * n_chunks + c, 2)),
                  pl.BlockSpec((q, n_heads), lambda b, c: (b * n_chunks + c, 0)),
                  pl.BlockSpec((1, 3, conv_ch), lambda b, c: (b, 0, 0)),
                  pl.BlockSpec((1, n_heads, hd, n_state), lambda b, c: (b, 0, 0, 0)),
                  full((CONV_W, conv_ch)), full((1, conv_ch)), full((1, n_heads)), full((1, n_heads)),
                  full((1, inner)), full((1, inner)), full((n_heads, inner)), full((n_heads, n_heads * q))],
        out_specs=[pl.BlockSpec((q, inner), lambda b, c: (b * n_chunks + c, 0)),
                   pl.BlockSpec((1, 3, conv_ch), lambda b, c: (b, 0, 0)),
                   pl.BlockSpec((1, n_heads, hd, n_state), lambda b, c: (b, 0, 0, 0))],
        out_shape=[jax.ShapeDtypeStruct((n_batch * seq, inner), BF16),
                   jax.ShapeDtypeStruct((n_batch, 3, conv_ch), F32),
                   jax.ShapeDtypeStruct(ssm0.shape, F32)],
        scratch_shapes=[pltpu.VMEM((q + SUBLANES, conv_ch), F32), pltpu.VMEM((n_heads, hd, n_state), F32),
                        pltpu.VMEM((q, inner), F32), pltpu.VMEM((q, inner), F32)],
        compiler_params=_cparams(2),
        name="ssd_prompt",
    )(proj1, proj1, proj1, dt_raw, conv0, ssm0, mp["cw"], mp["cb"], mp["dtb"], mp["alog"], mp["dexp"],
      mp["ng"], jnp.repeat(jnp.eye(n_heads, dtype=BF16), hd, axis=1),
      jnp.repeat(jnp.eye(n_heads, dtype=BF16), q, axis=1))


def _ssd_step_pre_kernel(x_ref, dt_ref, buf_ref, cw_ref, cb_ref, dtb_ref, alog_ref,
                         xs_ref, b_ref, c_ref, dt_out_ref, dec_ref, nbuf_ref):
    ch = x_ref.shape[1]
    inner = xs_ref.shape[1]
    gn = b_ref.shape[1]
    u = x_ref[...].astype(F32)
    b0, b1, b2 = (buf_ref[:, k * ch:(k + 1) * ch] for k in range(3))
    xbc = _silu(_conv4(u, b2, b1, b0, cw_ref[...], cb_ref[...]))
    xs_ref[...] = xbc[:, 0:inner]
    b_ref[...] = xbc[:, inner:inner + gn]
    c_ref[...] = xbc[:, inner + gn:]
    dt = _softplus(dt_ref[...] + dtb_ref[...])
    dt_out_ref[...] = dt
    dec_ref[...] = jnp.exp(dt * (-jnp.exp(alog_ref[...])))
    nbuf_ref[:, 0:ch] = b1
    nbuf_ref[:, ch:2 * ch] = b2
    nbuf_ref[:, 2 * ch:3 * ch] = u


def _ssd_step_kernel(s_ref, xt_ref, b_ref, c_ref, dt_ref, dec_ref, ns_ref, yt_ref):
    bt, n_heads = s_ref.shape[0], s_ref.shape[1]
    hpg = n_heads // SSM_GROUPS
    lane = lax.broadcasted_iota(jnp.int32, yt_ref.shape[1:], 1)
    for i in range(bt):
        xt = xt_ref[i]
        dt = dt_ref[i]
        dec = dec_ref[i]
        xdt = xt * dt
        s_news = [s_ref[i, h] * dec[:, h:h + 1] + xdt[:, h:h + 1] * b_ref[i, h // hpg]
                  for h in range(n_heads)]
        for h in range(n_heads):
            ns_ref[i, h] = s_news[h]
        ys = [jnp.sum(s_news[h] * c_ref[i, h // hpg], axis=-1, keepdims=True) for h in range(n_heads)]
        yt = jnp.zeros(yt_ref.shape[1:], F32)
        for h in range(n_heads):
            yt = jnp.where(lane == h, ys[h], yt)
        yt_ref[i] = yt


def _ssd_step_post_kernel(y_ref, xs_ref, z_ref, dexp_ref, ng_ref, o_ref):
    o_ref[...] = _mamba_post(y_ref[...], xs_ref[...], z_ref[...].astype(F32), dexp_ref[...],
                             ng_ref[...]).astype(o_ref.dtype)


def ssd_step(z, xbc_raw, dt_raw, conv_buf, ssm0, mp):
    n, conv_ch = xbc_raw.shape
    n_heads, hd, n_state = ssm0.shape[1:]
    inner = n_heads * hd
    gn = SSM_GROUPS * n_state
    xs, bm, cm, dt, dec, nbuf = _rows_call(
        _ssd_step_pre_kernel,
        [xbc_raw, dt_raw, conv_buf.reshape(n, 3 * conv_ch), mp["cw"], mp["cb"], mp["dtb"], mp["alog"]],
        [jax.ShapeDtypeStruct((n, inner), F32), jax.ShapeDtypeStruct((n, gn), F32),
         jax.ShapeDtypeStruct((n, gn), F32), jax.ShapeDtypeStruct((n, n_heads), F32),
         jax.ShapeDtypeStruct((n, n_heads), F32), jax.ShapeDtypeStruct((n, 3 * conv_ch), F32)],
        "ssd_step_pre")
    xt = jnp.transpose(xs.reshape(n, n_heads, hd), (0, 2, 1))
    bt = 4
    st_spec = pl.BlockSpec((bt, n_heads, hd, n_state), lambda i: (i, 0, 0, 0))
    t_spec = pl.BlockSpec((bt, hd, n_heads), lambda i: (i, 0, 0))
    g_spec = pl.BlockSpec((bt, SSM_GROUPS, 1, n_state), lambda i: (i, 0, 0, 0))
    h_spec = pl.BlockSpec((bt, 1, n_heads), lambda i: (i, 0, 0))
    new_ssm, yt = pl.pallas_call(
        _ssd_step_kernel,
        grid=(n // bt,),
        in_specs=[st_spec, t_spec, g_spec, g_spec, h_spec, h_spec],
        out_specs=[st_spec, t_spec],
        out_shape=[jax.ShapeDtypeStruct(ssm0.shape, F32), jax.ShapeDtypeStruct((n, hd, n_heads), F32)],
        compiler_params=_cparams(1),
        name="ssd_step",
    )(ssm0, xt, bm.reshape(n, SSM_GROUPS, 1, n_state), cm.reshape(n, SSM_GROUPS, 1, n_state),
      dt.reshape(n, 1, n_heads), dec.reshape(n, 1, n_heads))
    y = jnp.transpose(yt, (0, 2, 1)).reshape(n, inner)
    (y,) = _rows_call(_ssd_step_post_kernel, [y, xs, z, mp["dexp"], mp["ng"]],
                      [jax.ShapeDtypeStruct((n, inner), BF16)], "ssd_step_post")
    return y, nbuf, new_ssm


def _router_kernel(x_ref, cnt0_ref, g_ref, rwt_ref, rb_ref, xn_ref, idx_ref, gate_ref, rank_ref, cnt_ref,
                   cnt_sc):
    tm = x_ref.shape[0]

    @pl.when(pl.program_id(0) == 0)
    def _():
        cnt_sc[...] = cnt0_ref[...].astype(F32)

    xn = _rms(x_ref[...], g_ref[...], NORM_EPS)
    xn_ref[...] = xn
    logits = _dot_nt(rwt_ref[...], xn, HIGHEST) + rb_ref[...]
    e_iota = lax.broadcasted_iota(jnp.int32, logits.shape, 0)
    m1 = jnp.max(logits, axis=0, keepdims=True)
    i1 = jnp.min(jnp.where(logits == m1, e_iota, N_EXPERTS), axis=0, keepdims=True)
    rest = jnp.where(e_iota == i1, -jnp.inf, logits)
    m2 = jnp.max(rest, axis=0, keepdims=True)
    i2 = jnp.min(jnp.where(rest == m2, e_iota, N_EXPERTS), axis=0, keepdims=True)
    e2 = jnp.exp(m2 - m1)
    denom = 1.0 + e2
    idx_ref[...] = jnp.concatenate([i1, i2], axis=0)
    gate_ref[...] = jnp.concatenate([1.0 / denom, e2 / denom], axis=0)
    oh1 = (e_iota == i1).astype(F32)
    oh2 = (e_iota == i2).astype(F32)
    oh = oh1 + oh2
    ti = lax.broadcasted_iota(jnp.int32, (tm, tm), 0)
    si = lax.broadcasted_iota(jnp.int32, (tm, tm), 1)
    before = _dot(oh.astype(BF16), (ti < si).astype(BF16)) + cnt_sc[:, 0:1]
    rank_ref[...] = jnp.concatenate(
        [jnp.sum(oh1 * before, axis=0, keepdims=True), jnp.sum(oh2 * before, axis=0, keepdims=True)],
        axis=0).astype(jnp.int32)
    cnt_sc[...] = cnt_sc[...] + jnp.sum(oh, axis=1, keepdims=True)
    cnt_ref[...] = cnt_sc[...].astype(jnp.int32)


def moe_router(x, counts0, g, router_w, router_b, tm):
    t_rows, d = x.shape
    return pl.pallas_call(
        _router_kernel,
        grid=(t_rows // tm,),
        in_specs=[pl.BlockSpec((tm, d), lambda i: (i, 0)),
                  pl.BlockSpec((N_EXPERTS, LANES), lambda i: (0, 0)),
                  pl.BlockSpec((1, d), lambda i: (0, 0)),
                  pl.BlockSpec((N_EXPERTS, d), lambda i: (0, 0)),
                  pl.BlockSpec((N_EXPERTS, 1), lambda i: (0, 0))],
        out_specs=[pl.BlockSpec((tm, d), lambda i: (i, 0)),
                   pl.BlockSpec((2, tm), lambda i: (0, i)),
                   pl.BlockSpec((2, tm), lambda i: (0, i)),
                   pl.BlockSpec((2, tm), lambda i: (0, i)),
                   pl.BlockSpec((N_EXPERTS, LANES), lambda i: (0, 0))],
        out_shape=[jax.ShapeDtypeStruct((t_rows, d), F32),
                   jax.ShapeDtypeStruct((2, t_rows), jnp.int32),
                   jax.ShapeDtypeStruct((2, t_rows), F32),
                   jax.ShapeDtypeStruct((2, t_rows), jnp.int32),
                   jax.ShapeDtypeStruct((N_EXPERTS, LANES), jnp.int32)],
        scratch_shapes=[pltpu.VMEM((N_EXPERTS, LANES), F32)],
        compiler_params=_cparams(1),
        name="moe_router",
    )(x, counts0, g.reshape(1, d), router_w.T, router_b.reshape(N_EXPERTS, 1))


def _row_copy(src_ref, src_row, dst_ref, dst_row, sem):
    return pltpu.make_async_copy(src_ref.at[pl.ds(src_row, 1), :], dst_ref.at[pl.ds(dst_row, 1), :], sem)


def _dispatch_kernel(dest_ref, x_ref, slots_in_ref, slots_ref, sem, *, t_rows, row0):
    del slots_in_ref
    tm = x_ref.shape[0]
    base = row0 + pl.program_id(0) * tm

    def copies(r):
        return [_row_copy(x_ref, r, slots_ref, dest_ref[k * t_rows + base + r], sem) for k in range(2)]

    def start(r, carry):
        for k, cp in enumerate(copies(r)):
            cp.start(priority=k)
        return carry

    def wait(r, carry):
        for cp in copies(r):
            cp.wait()
        return carry

    lax.fori_loop(0, tm, start, 0, unroll=DMA_UNROLL)
    lax.fori_loop(0, tm, wait, 0, unroll=DMA_UNROLL)


def moe_dispatch(xn, dest_flat, slots, row0, tm):
    n_rows, d = xn.shape
    return pl.pallas_call(
        functools.partial(_dispatch_kernel, t_rows=dest_flat.shape[0] // 2, row0=row0),
        grid_spec=pltpu.PrefetchScalarGridSpec(
            num_scalar_prefetch=1,
            grid=(n_rows // tm,),
            in_specs=[pl.BlockSpec((tm, d), lambda i, dest: (i, 0)),
                      pl.BlockSpec(memory_space=pl.ANY)],
            out_specs=pl.BlockSpec(memory_space=pl.ANY),
            scratch_shapes=[pltpu.SemaphoreType.DMA(())]),
        out_shape=jax.ShapeDtypeStruct(slots.shape, slots.dtype),
        input_output_aliases={2: 0},
        compiler_params=_cparams(1),
        name="moe_dispatch",
    )(dest_flat, xn, slots)


def _moe_kernel(te_ref, tv_ref, x_ref, wg_ref, wu_ref, wd_ref, o_ref, xb_sc, acc_sc):
    i, j = pl.program_id(0), pl.program_id(1)
    tm = x_ref.shape[0]

    @pl.when(j == 0)
    def _():
        xb_sc[...] = x_ref[...].astype(BF16)
        acc_sc[...] = jnp.zeros_like(acc_sc)

    def swiglu_rows(rows):
        xb = xb_sc[0:rows, :]
        hg = _dot(xb, wg_ref[0].astype(BF16))
        hu = _dot(xb, wu_ref[0].astype(BF16))
        h = (_silu(hg) * hu).astype(BF16)
        acc_sc[0:rows, :] += _dot(h, wd_ref[0].astype(BF16))

    pl.when(tv_ref[i] == 2)(functools.partial(swiglu_rows, tm))
    pl.when(tv_ref[i] == 1)(functools.partial(swiglu_rows, tm // 2))

    @pl.when(j == pl.num_programs(1) - 1)
    def _():
        o_ref[...] = acc_sc[...]


def moe_experts(slots, tile_expert, tile_valid, wg, wu, wd, tm, tf):
    n_slots, d = slots.shape
    d_ff = wg.shape[2]
    n_f = d_ff // tf

    def f_idx(i, j, te, tv):
        return jnp.where(tv[i] > 0, j, n_f - 1)

    return pl.pallas_call(
        _moe_kernel,
        grid_spec=pltpu.PrefetchScalarGridSpec(
            num_scalar_prefetch=2,
            grid=(n_slots // tm, n_f),
            in_specs=[pl.BlockSpec((tm, d), lambda i, j, te, tv: (i, 0)),
                      pl.BlockSpec((1, d, tf), lambda i, j, te, tv: (te[i], 0, f_idx(i, j, te, tv))),
                      pl.BlockSpec((1, d, tf), lambda i, j, te, tv: (te[i], 0, f_idx(i, j, te, tv))),
                      pl.BlockSpec((1, tf, d), lambda i, j, te, tv: (te[i], f_idx(i, j, te, tv), 0))],
            out_specs=pl.BlockSpec((tm, d), lambda i, j, te, tv: (i, 0)),
            scratch_shapes=[pltpu.VMEM((tm, d), BF16), pltpu.VMEM((tm, d), F32)]),
        out_shape=jax.ShapeDtypeStruct((n_slots, d), F32),
        compiler_params=_cparams(2),
        name="moe_experts",
    )(tile_expert, tile_valid, slots, wg, wu, wd)


def _combine_kernel(dest_ref, x_ref, gates_ref, g_ref, y_hbm_ref, o_ref, buf_sc, sem, *, t_rows, row0):
    tm = x_ref.shape[0]
    base = row0 + pl.program_id(0) * tm

    def copies(r):
        return [_row_copy(y_hbm_ref, dest_ref[k * t_rows + base + r], buf_sc.at[k], r, sem)
                for k in range(2)]

    def start(r, carry):
        for k, cp in enumerate(copies(r)):
            cp.start(priority=k)
        return carry

    def wait(r, carry):
        for cp in copies(r):
            cp.wait()
        return carry

    lax.fori_loop(0, tm, start, 0, unroll=DMA_UNROLL)
    lax.fori_loop(0, tm, wait, 0, unroll=DMA_UNROLL)
    gates = gates_ref[...]
    out = x_ref[...] + (gates[:, 0:1] * buf_sc[0] + gates[:, 1:2] * buf_sc[1])
    o_ref[...] = _rms(out, g_ref[...], NORM_EPS)


def moe_combine(x, gates_col, dest_flat, y_slots, g_final, tm, row0):
    n_rows, d = x.shape
    t_rows = gates_col.shape[0]
    assert row0 % tm == 0 and n_rows % tm == 0
    blk0 = row0 // tm
    return pl.pallas_call(
        functools.partial(_combine_kernel, t_rows=t_rows, row0=row0),
        grid_spec=pltpu.PrefetchScalarGridSpec(
            num_scalar_prefetch=1,
            grid=(n_rows // tm,),
            in_specs=[pl.BlockSpec((tm, d), lambda i, dest: (i, 0)),
                      pl.BlockSpec((tm, 2), lambda i, dest: (i + blk0, 0)),
                      pl.BlockSpec((1, d), lambda i, dest: (0, 0)),
                      pl.BlockSpec(memory_space=pl.ANY)],
            out_specs=pl.BlockSpec((tm, d), lambda i, dest: (i, 0)),
            scratch_shapes=[pltpu.VMEM((2, tm, d), F32), pltpu.SemaphoreType.DMA(())]),
        out_shape=jax.ShapeDtypeStruct((n_rows, d), F32),
        compiler_params=_cparams(1),
        name="moe_combine",
    )(dest_flat, x, gates_col, g_final.reshape(1, d), y_slots)


def moe_final(x_prompt, x_sample, g_ffn, g_final, router_w, router_b, wg, wu, wd):
    n_prompt, n_sample = x_prompt.shape[0], x_sample.shape[0]
    t_rows = n_prompt + n_sample
    route = lambda x, counts0: moe_router(x, counts0, g_ffn, router_w, router_b,
                                          _row_tile(x.shape[0], ROUTER_TILE, LANES))
    xn_p, idx_p, gates_p, rank_p, counts_p = route(x_prompt, jnp.zeros((N_EXPERTS, LANES), jnp.int32))
    xn_s, idx_s, gates_s, rank_s, counts = route(x_sample, counts_p)
    idx, gates, rank = (jnp.concatenate(pair, axis=1) for pair in
                        ((idx_p, idx_s), (gates_p, gates_s), (rank_p, rank_s)))
    counts = counts[:, 0]
    tm = MOE_TILE
    n_tiles = -(-2 * t_rows // tm) + N_EXPERTS
    padded = (counts + tm - 1) // tm * tm
    pend = jnp.cumsum(padded)
    pstart = pend - padded
    dest = jnp.sum(jnp.where(idx[:, :, None] == jnp.arange(N_EXPERTS)[None, None, :], pstart[None, None, :], 0),
                   axis=-1) + rank
    dest_flat = dest.reshape(-1).astype(jnp.int32)
    tile_start = jnp.arange(n_tiles, dtype=jnp.int32) * tm
    last_start = jnp.maximum(pend[-1] - tm, 0)
    probe = jnp.minimum(tile_start, last_start)
    tile_expert = jnp.minimum(jnp.sum((pend[None, :] <= probe[:, None]).astype(jnp.int32), axis=1),
                              N_EXPERTS - 1)
    onehot_e = tile_expert[:, None] == jnp.arange(N_EXPERTS)[None, :]
    group_end = jnp.sum(jnp.where(onehot_e, (pstart + counts)[None, :], 0), axis=1)
    live_rows = jnp.clip(group_end - tile_start, 0, tm)
    half = tm // 2
    tile_valid = jnp.where(tile_start < pend[-1], (live_rows + half - 1) // half, 0).astype(jnp.int32)
    slots = jnp.zeros((n_tiles * tm, x_prompt.shape[1]), F32)
    slots = moe_dispatch(xn_p, dest_flat, slots, 0, _row_tile(n_prompt, GATHER_TILE))
    slots = moe_dispatch(xn_s, dest_flat, slots, n_prompt, _row_tile(n_sample, GATHER_TILE))
    y_slots = moe_experts(slots, tile_expert, tile_valid, wg, wu, wd, tm, MOE_FF_TILE)
    gates_col = gates.T
    y_prompt = moe_combine(x_prompt, gates_col, dest_flat, y_slots, g_final,
                           _row_tile(n_prompt, COMBINE_TILE), 0)
    y_sample = moe_combine(x_sample, gates_col, dest_flat, y_slots, g_final, n_sample, n_prompt)
    return y_prompt, y_sample


def _block_diag(w):
    h, i, j = w.shape
    eye = jnp.eye(h, dtype=w.dtype)
    return jnp.einsum("hij,hg->higj", w, eye).reshape(h * i, h * j)


def kernel(x_prompt, x_sample, state_lru_conv, state_lru_h, state_rwkv_shift, state_rwkv_wkv, state_ssm_conv, state_ssm, norm_mix, norm_ffn, norm_final, w_in0, lru_conv_w, lru_conv_b, lru_wa, lru_ba, lru_wx, lru_bx, lru_lambda, rwkv_mu, rwkv_w0, rwkv_w_decay_up, rwkv_a0, rwkv_w_iclr_up, rwkv_w_gate_up, rwkv_k_k, rwkv_k_a, rwkv_r_k, rwkv_ln_w, rwkv_ln_b, w_out0, ffn_wg, ffn_wu, ffn_wd, w_in1, ssm_conv_w, ssm_conv_b, ssm_dt_bias, ssm_a_log, ssm_d, ssm_norm_g, w_out1, router_w, router_b, moe_wg, moe_wu, moe_wd):
    nb, seq, d = x_prompt.shape
    ns = x_sample.shape[0]
    tp = nb * seq
    lru_w = lru_conv_w.shape[-1]
    rw_w = rwkv_w0.shape[-1]
    shift_cols = rwkv_mu.shape[-1]
    n_rheads = rw_w // RWKV_HEAD
    inner = ssm_norm_g.shape[-1]
    n_sheads = ssm_a_log.shape[-1]
    conv_ch = ssm_conv_w.shape[-1]

    xp, xs = x_prompt.reshape(tp, d), x_sample.reshape(ns, d)
    tmp, tms = (_row_tile(n, TOKEN_TILE, 2 * SUBLANES) for n in (tp, ns))

    row = lambda v: v.reshape(1, -1)
    lp = dict(cw=lru_conv_w[0], cb=row(lru_conv_b[0]), wa=_block_diag(lru_wa[0]).astype(BF16),
              ba=row(lru_ba[0]), wx=_block_diag(lru_wx[0]).astype(BF16), bx=row(lru_bx[0]),
              lam=row(lru_lambda[0]))
    hsum = _block_diag(jnp.ones((n_rheads, RWKV_HEAD, RWKV_HEAD), BF16))
    rp = dict(mu=row(rwkv_mu[0]), w0=row(rwkv_w0[0]), wdec=rwkv_w_decay_up[0], a0=row(rwkv_a0[0]),
              wiclr=rwkv_w_iclr_up[0], wgate=rwkv_w_gate_up[0], kk=row(rwkv_k_k[0]), ka=row(rwkv_k_a[0]),
              rk=row(rwkv_r_k[0]), lnw=row(rwkv_ln_w[0]), lnb=row(rwkv_ln_b[0]), hsum=hsum)
    mp = dict(cw=ssm_conv_w[0], cb=row(ssm_conv_b[0]), dtb=row(ssm_dt_bias[0]), alog=row(ssm_a_log[0]),
              dexp=row(jnp.repeat(ssm_d[0], SSM_HEAD)), ng=row(ssm_norm_g[0]))

    w_in0_rwkv = w_in0[0][:, 2 * lru_w:]

    def in0(x, tm):
        return (norm_matmul(x, norm_mix[0], w_in0[0], 0, 1, 2 * lru_w, tm, "in0_lru"),
                norm_matmul(x, norm_mix[0], w_in0_rwkv, 0, 1, shift_cols, tm, "in0_rwkv"))

    proj_lru_p, proj_rwkv_p = in0(xp, tmp)
    proj_lru_s, s_p_rwkv = in0(xs, tms)
    zeros = lambda *shape: jnp.zeros(shape, F32)
    out_a_p, p_lru_conv, p_lru_h = lru_prompt(proj_lru_p, nb, seq, zeros(nb, 3, lru_w), zeros(nb, lru_w), lp)
    out_b_p, p_shift, p_wkv = rwkv_prompt(proj_rwkv_p, nb, seq, zeros(nb, shift_cols),
                                          zeros(nb, n_rheads, RWKV_HEAD, RWKV_HEAD), rp)
    out_a_s, s_lru_conv, s_lru_h = lru_step(proj_lru_s[:, :lru_w], proj_lru_s[:, lru_w:], state_lru_conv[0],
                                            state_lru_h[0], lp)
    out_b_s, s_wkv = rwkv_step(s_p_rwkv, state_rwkv_shift[0], state_rwkv_wkv[0], rp)

    def mix0_ffn(out_a, out_b, x, tm):
        x = matmul_residual([out_a, out_b], w_out0[0], x, tm, d, "out0")
        return ffn_residual(x, norm_ffn[0], ffn_wg[0], ffn_wu[0], ffn_wd[0], tm, FFN_FF_TILE)

    xp = mix0_ffn(out_a_p, out_b_p, xp, tmp)
    xs = mix0_ffn(out_a_s, out_b_s, xs, tms)

    tn1 = 1536
    w_in1_dt = w_in1[0][:, inner + conv_ch:]

    def in1(x, tm):
        return (norm_matmul(x, norm_mix[1], w_in1[0], 0, (inner + conv_ch) // tn1, tn1, tm, "in1_main", BF16),
                norm_matmul(x, norm_mix[1], w_in1_dt, 0, 1, n_sheads, tm, "in1_dt"))

    proj1_p, dt_p = in1(xp, _row_tile(tp, 2 * TOKEN_TILE, 2 * SUBLANES))
    proj1_s, dt_s = in1(xs, tms)
    y_mix_p, p_ssm_conv, p_ssm = ssd_prompt(proj1_p, dt_p, nb, seq, zeros(nb, 3, conv_ch),
                                            zeros(nb, n_sheads, SSM_HEAD, SSM_STATE), mp)
    y_mix_s, s_ssm_conv, s_ssm = ssd_step(proj1_s[:, :inner], proj1_s[:, inner:], dt_s,
                                          state_ssm_conv[0], state_ssm[0], mp)
    xp = matmul_residual([y_mix_p], w_out1[0], xp, tmp, d, "out1")
    xs = matmul_residual([y_mix_s], w_out1[0], xs, tms, d, "out1")
    y_p, y_s = moe_final(xp, xs, norm_ffn[1], norm_final, router_w[0], router_b[0], moe_wg[0], moe_wu[0],
                         moe_wd[0])

    return (y_p.reshape(nb, seq, d), y_s.reshape(ns, 1, d),
            p_lru_conv[None], p_lru_h.reshape(1, nb, lru_w), p_shift.reshape(1, nb, shift_cols), p_wkv[None],
            p_ssm_conv[None], p_ssm[None],
            s_lru_conv.reshape(1, ns, 3, lru_w), s_lru_h[None], s_p_rwkv[None], s_wkv[None],
            s_ssm_conv.reshape(1, ns, 3, conv_ch), s_ssm[None])
```

```python
import functools

import jax
import jax.numpy as jnp
from jax import lax
from jax.experimental import pallas as pl
from jax.experimental.pallas import tpu as pltpu

F32 = jnp.float32
BF16 = jnp.bfloat16
HIGHEST = lax.Precision.HIGHEST

NORM_EPS = 1e-6
CONV_W = 4
LRU_HEADS = 8
LRU_C = 8.0
RWKV_HEAD = 64
DECAY_RANK = 64
ICLR_RANK = 64
GATE_RANK = 128
RWKV_GN_EPS = 64e-5
SSM_HEAD = 64
SSM_GROUPS = 8
SSM_STATE = 128
SSM_CHUNK = 128
SSM_NORM_EPS = 1e-5
N_EXPERTS = 8

V7X_VMEM_BYTES = 64 * 1024 * 1024
VMEM_LIMIT = V7X_VMEM_BYTES - 8 * 1024 * 1024
SUBLANES = 8
LANES = 128

RWKV_CHUNK = 64
LRU_CHUNK = 256
MOE_TILE = 1024
MOE_FF_TILE = 512
FFN_FF_TILE = 512
TOKEN_TILE = 1024
ROUTER_TILE = 512
GATHER_TILE = 512
COMBINE_TILE = 512
DMA_UNROLL = 8


def _cparams(n_axes):
    return pltpu.CompilerParams(dimension_semantics=("arbitrary",) * n_axes,
                                vmem_limit_bytes=VMEM_LIMIT)


def _row_tile(n_rows, cap, mult=SUBLANES):
    best = None
    for t in range(mult, min(cap, n_rows) + 1, mult):
        if n_rows % t == 0:
            best = t
    assert best is not None, (n_rows, cap)
    return best


def _dot(a, b, precision=None):
    return jnp.dot(a, b, preferred_element_type=F32, precision=precision)


def _dot_nt(a, b, precision=None):
    return lax.dot_general(a, b, (((1,), (1,)), ((), ())), preferred_element_type=F32,
                           precision=precision)


def _dot_tn(a, b, precision=None):
    return lax.dot_general(a, b, (((0,), (0,)), ((), ())), preferred_element_type=F32,
                           precision=precision)


_NN = (((1,), (0,)), ((), ()))
_NT = (((1,), (1,)), ((), ()))
_TN = (((0,), (0,)), ((), ()))


def _split2(x):
    hi = x.astype(BF16)
    lo = (x - hi.astype(F32)).astype(BF16)
    return hi, lo


def _split3(x):
    hi = x.astype(BF16)
    r1 = x - hi.astype(F32)
    mid = r1.astype(BF16)
    lo = (r1 - mid.astype(F32)).astype(BF16)
    return hi, mid, lo


def _dg(a, b, dims):
    return lax.dot_general(a, b, dims, preferred_element_type=F32)


def _dot3(a2, b2, dims=_NN):
    (ah, al), (bh, bl) = a2, b2
    return _dg(ah, bh, dims) + _dg(al, bh, dims) + _dg(ah, bl, dims)


def _dot_exact_rhs(x, m_bf16, terms=3):
    return sum(_dg(part, m_bf16, _NN) for part in (_split3(x) if terms == 3 else _split2(x)))


def _dot_exact_lhs(m_bf16, x):
    return sum(_dg(m_bf16, part, _NN) for part in _split3(x))


def _softplus(x):
    return jnp.maximum(x, 0.0) + jnp.log1p(jnp.exp(-jnp.abs(x)))


def _silu(x):
    return x * jax.nn.sigmoid(x)


def _gelu_tanh(x):
    return 0.5 * x * (1.0 + jnp.tanh(0.7978845608028654 * (x + 0.044715 * (x * x * x))))


def _rms(x, g, eps):
    return x * lax.rsqrt(jnp.mean(x * x, axis=-1, keepdims=True) + eps) * g


def _norm_mm_kernel(x_ref, g_ref, w_ref, o_ref, xn_sc):
    @pl.when(pl.program_id(1) == 0)
    def _():
        xn_sc[...] = _rms(x_ref[...], g_ref[...], NORM_EPS).astype(BF16)

    o_ref[...] = _dot(xn_sc[...], w_ref[...].astype(BF16)).astype(o_ref.dtype)


def norm_matmul(x, g, w, col0_blk, n_blk, tn, tm, name, out_dtype=F32):
    t_rows, d = x.shape
    return pl.pallas_call(
        _norm_mm_kernel,
        grid=(t_rows // tm, n_blk),
        in_specs=[pl.BlockSpec((tm, d), lambda i, j: (i, 0)),
                  pl.BlockSpec((1, d), lambda i, j: (0, 0)),
                  pl.BlockSpec((d, tn), lambda i, j: (0, j + col0_blk))],
        out_specs=pl.BlockSpec((tm, tn), lambda i, j: (i, j)),
        out_shape=jax.ShapeDtypeStruct((t_rows, n_blk * tn), out_dtype),
        scratch_shapes=[pltpu.VMEM((tm, d), BF16)],
        compiler_params=_cparams(2),
        name=name,
    )(x, g.reshape(1, d), w)


def _mm_res_kernel(*refs, n_in):
    x_refs, w_refs = refs[:n_in], refs[n_in:2 * n_in]
    res_ref, o_ref = refs[2 * n_in], refs[2 * n_in + 1]
    acc = res_ref[...]
    for x_ref, w_ref in zip(x_refs, w_refs):
        acc = acc + _dot(x_ref[...].astype(BF16), w_ref[...].astype(BF16))
    o_ref[...] = acc


def matmul_residual(xs, w, res, tm, tn, name):
    n_in = len(xs)
    t_rows, kp = xs[0].shape
    n_cols = w.shape[1]
    in_specs = [pl.BlockSpec((tm, kp), lambda i, j: (i, 0)) for _ in xs]
    in_specs += [pl.BlockSpec((kp, tn), functools.partial(lambda i, j, p: (p, j), p=p))
                 for p in range(n_in)]
    in_specs += [pl.BlockSpec((tm, tn), lambda i, j: (i, j))]
    return pl.pallas_call(
        functools.partial(_mm_res_kernel, n_in=n_in),
        grid=(t_rows // tm, n_cols // tn),
        in_specs=in_specs,
        out_specs=pl.BlockSpec((tm, tn), lambda i, j: (i, j)),
        out_shape=jax.ShapeDtypeStruct((t_rows, n_cols), F32),
        compiler_params=_cparams(2),
        name=name,
    )(*xs, *([w] * n_in), res)


def _ffn_kernel(x_ref, g_ref, wg_ref, wu_ref, wd_ref, o_ref, xn_sc, acc_sc):
    j = pl.program_id(1)

    @pl.when(j == 0)
    def _():
        xn_sc[...] = _rms(x_ref[...], g_ref[...], NORM_EPS).astype(BF16)
        acc_sc[...] = jnp.zeros_like(acc_sc)

    xn = xn_sc[...]
    hg = _dot(xn, wg_ref[...].astype(BF16))
    hu = _dot(xn, wu_ref[...].astype(BF16))
    h = (_silu(hg) * hu).astype(BF16)
    acc_sc[...] += _dot(h, wd_ref[...].astype(BF16))

    @pl.when(j == pl.num_programs(1) - 1)
    def _():
        o_ref[...] = x_ref[...] + acc_sc[...]


def ffn_residual(x, g, wg, wu, wd, tm, tf):
    t_rows, d = x.shape
    d_ff = wg.shape[1]
    return pl.pallas_call(
        _ffn_kernel,
        grid=(t_rows // tm, d_ff // tf),
        in_specs=[pl.BlockSpec((tm, d), lambda i, j: (i, 0)),
                  pl.BlockSpec((1, d), lambda i, j: (0, 0)),
                  pl.BlockSpec((d, tf), lambda i, j: (0, j)),
                  pl.BlockSpec((d, tf), lambda i, j: (0, j)),
                  pl.BlockSpec((tf, d), lambda i, j: (j, 0))],
        out_specs=pl.BlockSpec((tm, d), lambda i, j: (i, 0)),
        out_shape=jax.ShapeDtypeStruct((t_rows, d), F32),
        scratch_shapes=[pltpu.VMEM((tm, d), BF16), pltpu.VMEM((tm, d), F32)],
        compiler_params=_cparams(2),
        name="ffn_swiglu",
    )(x, g.reshape(1, d), wg, wu, wd)


def _conv4(u, u1, u2, u3, cw, cb):
    return cb + cw[3:4] * u + cw[2:3] * u1 + cw[1:2] * u2 + cw[0:1] * u3


def _lru_gates(xc, wa, ba, wx, bx, lam):
    xb = xc.astype(BF16)
    r = jax.nn.sigmoid(_dot(xb, wa) + ba)
    i = jax.nn.sigmoid(_dot(xb, wx) + bx)
    log_a = -LRU_C * r * _softplus(-lam)
    a = jnp.exp(log_a)
    u = jnp.sqrt(1.0 - jnp.exp(2.0 * log_a)) * (i * xc)
    return a, u


def _lru_prompt_kernel(x_ref, g_ref, conv0_ref, h0_ref, cw_ref, cb_ref, wa_ref, ba_ref, wx_ref,
                       bx_ref, lam_ref, o_ref, nconv_ref, nh_ref, ext_sc, h_sc, *, lc):
    width = x_ref.shape[1]

    @pl.when(pl.program_id(1) == 0)
    def _():
        ext_sc[0:SUBLANES, :] = jnp.zeros((SUBLANES, width), F32)
        ext_sc[SUBLANES - 3:SUBLANES, :] = conv0_ref[0]
        h_sc[...] = h0_ref[0]

    u = x_ref[...]
    ext_sc[SUBLANES:SUBLANES + lc, :] = u
    xc = _conv4(u, ext_sc[SUBLANES - 1:SUBLANES - 1 + lc, :], ext_sc[SUBLANES - 2:SUBLANES - 2 + lc, :],
                ext_sc[SUBLANES - 3:SUBLANES - 3 + lc, :], cw_ref[...], cb_ref[...])
    tail = ext_sc[lc + SUBLANES - 3:lc + SUBLANES, :]
    ext_sc[SUBLANES - 3:SUBLANES, :] = tail
    nconv_ref[0] = tail

    a, h = _lru_gates(xc, wa_ref[...], ba_ref[...], wx_ref[...], bx_ref[...], lam_ref[...])
    row = lax.broadcasted_iota(jnp.int32, (lc, width), 0)
    s = 1
    while s < lc:
        keep = row >= s
        a_sh = jnp.where(keep, pltpu.roll(a, s, 0), 1.0)
        h_sh = jnp.where(keep, pltpu.roll(h, s, 0), 0.0)
        h = a * h_sh + h
        a = a * a_sh
        s *= 2
    hs = h + a * h_sc[...]
    h_last = hs[lc - 1:lc, :]
    h_sc[...] = h_last
    nh_ref[0] = h_last
    o_ref[...] = (hs * _gelu_tanh(g_ref[...])).astype(o_ref.dtype)


def lru_prompt(proj_lru, n_batch, seq, conv0, h0, lp):
    width = conv0.shape[-1]
    lc = min(LRU_CHUNK, seq)
    n_chunks = seq // lc
    full = lambda shape: pl.BlockSpec(shape, lambda b, c: (0,) * len(shape))
    return pl.pallas_call(
        functools.partial(_lru_prompt_kernel, lc=lc),
        grid=(n_batch, n_chunks),
        in_specs=[pl.BlockSpec((lc, width), lambda b, c: (b * n_chunks + c, 0)),
                  pl.BlockSpec((lc, width), lambda b, c: (b * n_chunks + c, 1)),
                  pl.BlockSpec((1, 3, width), lambda b, c: (b, 0, 0)),
                  pl.BlockSpec((1, 1, width), lambda b, c: (b, 0, 0)),
                  full((CONV_W, width)), full((1, width)), full((width, width)), full((1, width)),
                  full((width, width)), full((1, width)), full((1, width))],
        out_specs=[pl.BlockSpec((lc, width), lambda b, c: (b * n_chunks + c, 0)),
                   pl.BlockSpec((1, 3, width), lambda b, c: (b, 0, 0)),
                   pl.BlockSpec((1, 1, width), lambda b, c: (b, 0, 0))],
        out_shape=[jax.ShapeDtypeStruct((n_batch * seq, width), BF16),
                   jax.ShapeDtypeStruct((n_batch, 3, width), F32),
                   jax.ShapeDtypeStruct((n_batch, 1, width), F32)],
        scratch_shapes=[pltpu.VMEM((lc + SUBLANES, width), F32), pltpu.VMEM((1, width), F32)],
        compiler_params=_cparams(2),
        name="lru_prompt",
    )(proj_lru, proj_lru, conv0, h0.reshape(n_batch, 1, width), lp["cw"], lp["cb"], lp["wa"], lp["ba"],
      lp["wx"], lp["bx"], lp["lam"])


def _rows_call(body, inputs, out_shapes, name):
    return pl.pallas_call(body, out_shape=out_shapes, name=name,
                          compiler_params=pltpu.CompilerParams(vmem_limit_bytes=VMEM_LIMIT))(*inputs)


def _lru_step_kernel(x_ref, g_ref, buf_ref, h0_ref, cw_ref, cb_ref, wa_ref, ba_ref, wx_ref, bx_ref,
                     lam_ref, o_ref, nbuf_ref, nh_ref):
    width = x_ref.shape[1]
    u = x_ref[...]
    b0, b1, b2 = (buf_ref[:, k * width:(k + 1) * width] for k in range(3))
    xc = _conv4(u, b2, b1, b0, cw_ref[...], cb_ref[...])
    a, uu = _lru_gates(xc, wa_ref[...], ba_ref[...], wx_ref[...], bx_ref[...], lam_ref[...])
    h = a * h0_ref[...] + uu
    nh_ref[...] = h
    o_ref[...] = (h * _gelu_tanh(g_ref[...])).astype(o_ref.dtype)
    nbuf_ref[:, 0:width] = b1
    nbuf_ref[:, width:2 * width] = b2
    nbuf_ref[:, 2 * width:3 * width] = u


def lru_step(x_lru, g_lru, conv_buf, h0, lp):
    n, width = x_lru.shape
    return _rows_call(
        _lru_step_kernel,
        [x_lru, g_lru, conv_buf.reshape(n, 3 * width), h0, lp["cw"], lp["cb"], lp["wa"], lp["ba"],
         lp["wx"], lp["bx"], lp["lam"]],
        [jax.ShapeDtypeStruct((n, width), BF16), jax.ShapeDtypeStruct((n, 3 * width), F32),
         jax.ShapeDtypeStruct((n, width), F32)],
        "lru_step")


def _rwkv_rows(mixed, rp):
    w = rp["w0"].shape[1]
    r, k, v = mixed[:, 0:w], mixed[:, w:2 * w], mixed[:, 2 * w:3 * w]
    o = 3 * w
    wd = mixed[:, o:o + DECAY_RANK]
    ad = mixed[:, o + DECAY_RANK:o + DECAY_RANK + ICLR_RANK]
    gd = mixed[:, o + DECAY_RANK + ICLR_RANK:o + DECAY_RANK + ICLR_RANK + GATE_RANK]
    dec_in = rp["w0"] + _dot(jnp.tanh(wd).astype(BF16), rp["wdec"].astype(BF16))
    w_log = -_softplus(-dec_in) - 0.5
    lw = -jnp.exp(w_log)
    iclr = jax.nn.sigmoid(rp["a0"] + _dot(ad.astype(BF16), rp["wiclr"].astype(BF16)))
    gate = _dot(jax.nn.sigmoid(gd).astype(BF16), rp["wgate"].astype(BF16))
    kk = k * rp["kk"]
    ss = _dot_exact_rhs(kk * kk, rp["hsum"])
    kkn = kk / jnp.maximum(jnp.sqrt(ss), 1e-12)
    k2 = k * (1.0 + (iclr - 1.0) * rp["ka"])
    return r, k2, v, lw, -kkn, kkn * iclr, gate


def _rwkv_post(o, r, k2, v, gate, rp):
    inv = 1.0 / RWKV_HEAD
    mu = _dot_exact_rhs(o, rp["hsum"], 2) * inv
    d = o - mu
    var = _dot_exact_rhs(d * d, rp["hsum"], 2) * inv
    on = d * lax.rsqrt(var + RWKV_GN_EPS) * rp["lnw"] + rp["lnb"]
    bonus = _dot_exact_rhs(r * k2 * rp["rk"], rp["hsum"], 2) * v
    return (on + bonus) * gate


_RWKV_PARAM_NAMES = ("mu", "w0", "wdec", "a0", "wiclr", "wgate", "kk", "ka", "rk", "lnw", "lnb", "hsum")


def _rwkv_prompt_kernel(*refs, chunk, nbs):
    p_refs, shift0_ref, s0_ref, rest = refs[:nbs], refs[nbs], refs[nbs + 1], refs[nbs + 2:]
    n_prm = len(_RWKV_PARAM_NAMES)
    rp = {name: ref[...] for name, ref in zip(_RWKV_PARAM_NAMES, rest[:n_prm])}
    o_ref, nshift_ref, nwkv_ref, prev_sc, s_sc, o_sc = rest[n_prm:]
    n_pairs = s_sc.shape[0] // nbs
    hd = RWKV_HEAD
    assert chunk == hd, "the pair-packed layout below uses chunk == head size"

    @pl.when(pl.program_id(1) == 0)
    def _():
        for i in range(nbs):
            prev_sc[i] = shift0_ref[i]
            for p in range(n_pairs):
                s_sc[i * n_pairs + p] = jnp.concatenate([s0_ref[i, 2 * p], s0_ref[i, 2 * p + 1]], axis=0).T

    ti = lax.broadcasted_iota(jnp.int32, (chunk, chunk), 0)
    si = lax.broadcasted_iota(jnp.int32, (chunk, chunk), 1)
    lower = (ti >= si).astype(BF16)

    def prep(i):
        p = p_refs[i][...]
        row = lax.broadcasted_iota(jnp.int32, p.shape, 0)
        p_prev = jnp.where(row >= 1, pltpu.roll(p, 1, 0), prev_sc[i])
        last = p[chunk - 1:chunk, :]
        prev_sc[i] = last
        nshift_ref[i] = last
        mixed = p + (p_prev - p) * rp["mu"]
        r, k2, v, lw, a, b, gate = _rwkv_rows(mixed, rp)
        cs = _dot_exact_lhs(lower, lw)
        g_in, g_ex, g_inv = jnp.exp(cs), jnp.exp(cs - lw), jnp.exp(-cs)
        bt, kt = b * g_inv, k2 * g_inv
        g_end = g_in[chunk - 1:chunk, :]
        return dict(r=r, k2=k2, v=v, gate=gate, at=a * g_ex, bt=bt, kt=kt, rt=r * g_in, g_end=g_end,
                    bc=bt * g_end, kc=kt * g_end)

    preps = [prep(i) for i in range(nbs)]
    n_fac = max(1, (chunk - 1).bit_length())
    pw_ = 2 * hd
    lane1 = lax.broadcasted_iota(jnp.int32, (1, pw_), 1)
    lo1 = lane1 < hd
    lo2 = jnp.concatenate([lo1, lo1], axis=1)
    ti2 = lax.broadcasted_iota(jnp.int32, (chunk, pw_), 0)
    si2 = lax.broadcasted_iota(jnp.int32, (chunk, pw_), 1) % hd
    strict2, incl2 = ti2 > si2, ti2 >= si2

    def bdiag(x, lo):
        return jnp.concatenate([jnp.where(lo, x, 0), jnp.where(lo, 0, x)], axis=0)

    def bdiag2(x2, lo):
        return bdiag(x2[0], lo), bdiag(x2[1], lo)

    units = [(i, q) for i in range(nbs) for q in range(n_pairs)]
    pairs = range(len(units))
    pls = [slice(q * pw_, (q + 1) * pw_) for _, q in units]
    a_p, r_p, v_p, bt_p, kt_p, bc_p, kc_p, ge_p = (
        [preps[i][name][:, pls[u]] for u, (i, _) in enumerate(units)]
        for name in ("at", "rt", "v", "bt", "kt", "bc", "kc", "g_end"))
    mas, mrs, v2s = [], [], []
    for p in pairs:
        b2, k2_ = _split2(bt_p[p]), _split2(kt_p[p])
        rhs_rows = tuple(jnp.concatenate([bdiag(b2[i], lo1), bdiag(k2_[i], lo1)], axis=0) for i in range(2))
        mas.append(_dot3(_split2(a_p[p]), rhs_rows, _NT))
        mrs.append(_dg(r_p[p].astype(BF16), rhs_rows[0], _NT))
        v2s.append(bdiag2(_split2(v_p[p]), lo1))
    labs = [jnp.where(strict2, m[:, :pw_], 0.0) for m in mas]
    lrbs = [jnp.where(incl2, m[:, :pw_], 0.0).astype(BF16) for m in mrs]
    lakvs = [_dot3(_split2(jnp.where(strict2, mas[p][:, pw_:], 0.0)), v2s[p]) for p in pairs]
    lrkvs = [_dg(jnp.where(incl2, mrs[p][:, pw_:], 0.0).astype(BF16), v2s[p][0], _NN) for p in pairs]
    ys = [jnp.concatenate([a_p[p], lakvs[p]], axis=1) for p in pairs]
    pws = [_split2(lab) for lab in labs]
    for f in range(n_fac):
        ys = [ys[p] + _dot3(pws[p], bdiag2(_split2(ys[p]), lo2)) for p in pairs]
        if f + 1 < n_fac:
            pws = [_split2(_dot3(pw, bdiag2(pw, lo1))) for pw in pws]
    y2s = [_split2(y) for y in ys]
    qos = [jnp.concatenate([r_p[p], lrkvs[p]], axis=1) + _dg(lrbs[p], bdiag(y2s[p][0], lo2), _NN)
           for p in pairs]
    xs_ = [_dot3(_split2(bc_p[p]), y2s[p], _TN) for p in pairs]
    kvs = [_dot3(_split2(kc_p[p]), _split2(v_p[p]), _TN) for p in pairs]
    mts = [_split2(jnp.where(lo1, xs_[p][:hd, :pw_], xs_[p][hd:, :pw_]) + jnp.where(ti2 == si2, ge_p[p], 0.0))
           for p in pairs]
    n0s = [jnp.where(lo1, xs_[p][:hd, pw_:] + kvs[p][:hd], xs_[p][hd:, pw_:] + kvs[p][hd:]) for p in pairs]
    s0s = [bdiag2(_split2(s_sc[p]), lo1) for p in pairs]
    for p, (i, _) in enumerate(units):
        o_sc[i, :, pls[p]] = _dg(qos[p][:, :pw_].astype(BF16), s0s[p][0], _NN) + qos[p][:, pw_:]
    for p in pairs:
        s_sc[p] = _dot3(mts[p], s0s[p]) + n0s[p]

    @pl.when(pl.program_id(1) == pl.num_programs(1) - 1)
    def _():
        for p, (i, q) in enumerate(units):
            s_pair = s_sc[p].T
            nwkv_ref[i, 2 * q] = s_pair[:hd]
            nwkv_ref[i, 2 * q + 1] = s_pair[hd:]

    for i, pre in enumerate(preps):
        o_ref[i] = _rwkv_post(o_sc[i], pre["r"], pre["k2"], pre["v"], pre["gate"], rp).astype(o_ref.dtype)


def _rwkv_param_list(rp):
    return [rp[name] for name in _RWKV_PARAM_NAMES]


def rwkv_prompt(p_rwkv, n_batch, seq, shift0, wkv0, rp):
    cols = shift0.shape[-1]
    n_heads, hd = wkv0.shape[1], wkv0.shape[2]
    width = n_heads * hd
    chunk = min(RWKV_CHUNK, seq)
    n_chunks = seq // chunk
    prm = _rwkv_param_list(rp)
    prm_specs = [pl.BlockSpec(x.shape, lambda b, c: (0, 0)) for x in prm]
    nbs = max(n for n in (1, 2, 4) if n_batch % n == 0)
    p_specs = [pl.BlockSpec((chunk, cols), functools.partial(
        lambda b, c, i: ((b * nbs + i) * n_chunks + c, 0), i=i)) for i in range(nbs)]
    out, new_shift, new_wkv = pl.pallas_call(
        functools.partial(_rwkv_prompt_kernel, chunk=chunk, nbs=nbs),
        grid=(n_batch // nbs, n_chunks),
        in_specs=p_specs + [pl.BlockSpec((nbs, 1, cols), lambda b, c: (b, 0, 0)),
                            pl.BlockSpec((nbs, n_heads, hd, hd), lambda b, c: (b, 0, 0, 0))] + prm_specs,
        out_specs=[pl.BlockSpec((nbs, chunk, width), lambda b, c: (b, c, 0)),
                   pl.BlockSpec((nbs, 1, cols), lambda b, c: (b, 0, 0)),
                   pl.BlockSpec((nbs, n_heads, hd, hd), lambda b, c: (b, 0, 0, 0))],
        out_shape=[jax.ShapeDtypeStruct((n_batch, seq, width), BF16),
                   jax.ShapeDtypeStruct((n_batch, 1, cols), F32),
                   jax.ShapeDtypeStruct((n_batch, n_heads, hd, hd), F32)],
        scratch_shapes=[pltpu.VMEM((nbs, 1, cols), F32), pltpu.VMEM((nbs * n_heads // 2, hd, 2 * hd), F32),
                        pltpu.VMEM((nbs, chunk, width), F32)],
        compiler_params=_cparams(2),
        name="rwkv_prompt",
    )(*([p_rwkv] * nbs), shift0.reshape(n_batch, 1, cols), wkv0, *prm)
    return out.reshape(n_batch * seq, width), new_shift, new_wkv


def _rwkv_step_pre_kernel(p_ref, prev_ref, *rest):
    n_prm = len(_RWKV_PARAM_NAMES)
    rp = {name: ref[...] for name, ref in zip(_RWKV_PARAM_NAMES, rest[:n_prm])}
    r_ref, k_ref, v_ref, w_ref, a_ref, b_ref, gate_ref = rest[n_prm:]
    p = p_ref[...]
    mixed = p + (prev_ref[...] - p) * rp["mu"]
    r, k2, v, lw, a, b, gate = _rwkv_rows(mixed, rp)
    r_ref[...] = r
    k_ref[...] = k2
    v_ref[...] = v
    w_ref[...] = jnp.exp(lw)
    a_ref[...] = a
    b_ref[...] = b
    gate_ref[...] = gate


def _rwkv_step_kernel(s_ref, w_ref, a_ref, b_ref, k_ref, r_ref, vt_ref, ns_ref, ot_ref):
    bt, n_heads = s_ref.shape[0], s_ref.shape[1]
    lane = lax.broadcasted_iota(jnp.int32, ot_ref.shape[1:], 1)
    for i in range(bt):
        vt = vt_ref[i]
        ss = [s_ref[i, h] for h in range(n_heads)]
        sas = [jnp.sum(ss[h] * a_ref[i, h], axis=-1, keepdims=True) for h in range(n_heads)]
        s_news = [ss[h] * w_ref[i, h] + sas[h] * b_ref[i, h] + vt[:, h:h + 1] * k_ref[i, h]
                  for h in range(n_heads)]
        for h in range(n_heads):
            ns_ref[i, h] = s_news[h]
        os_ = [jnp.sum(s_news[h] * r_ref[i, h], axis=-1, keepdims=True) for h in range(n_heads)]
        ot = jnp.zeros(ot_ref.shape[1:], F32)
        for h in range(n_heads):
            ot = jnp.where(lane == h, os_[h], ot)
        ot_ref[i] = ot


def _rwkv_step_post_kernel(o_ref, r_ref, k_ref, v_ref, gate_ref, *rest):
    n_prm = len(_RWKV_PARAM_NAMES)
    rp = {name: ref[...] for name, ref in zip(_RWKV_PARAM_NAMES, rest[:n_prm])}
    out_ref = rest[n_prm]
    out_ref[...] = _rwkv_post(o_ref[...], r_ref[...], k_ref[...], v_ref[...], gate_ref[...],
                              rp).astype(out_ref.dtype)


def rwkv_step(p_rwkv, shift_prev, wkv0, rp):
    n = p_rwkv.shape[0]
    n_heads, hd = wkv0.shape[1], wkv0.shape[2]
    width = n_heads * hd
    prm = _rwkv_param_list(rp)
    row = jax.ShapeDtypeStruct((n, width), F32)
    r, k2, v, w, a, b, gate = _rows_call(_rwkv_step_pre_kernel, [p_rwkv, shift_prev, *prm], [row] * 7,
                                         "rwkv_step_pre")
    hrow = lambda z: z.reshape(n, n_heads, 1, hd)
    vt = jnp.transpose(v.reshape(n, n_heads, hd), (0, 2, 1))
    bt = SUBLANES
    vec_spec = pl.BlockSpec((bt, n_heads, 1, hd), lambda i: (i, 0, 0, 0))
    st_spec = pl.BlockSpec((bt, n_heads, hd, hd), lambda i: (i, 0, 0, 0))
    t_spec = pl.BlockSpec((bt, hd, n_heads), lambda i: (i, 0, 0))
    new_wkv, ot = pl.pallas_call(
        _rwkv_step_kernel,
        grid=(n // bt,),
        in_specs=[st_spec] + [vec_spec] * 5 + [t_spec],
        out_specs=[st_spec, t_spec],
        out_shape=[jax.ShapeDtypeStruct(wkv0.shape, F32), jax.ShapeDtypeStruct((n, hd, n_heads), F32)],
        compiler_params=_cparams(1),
        name="rwkv_step",
    )(wkv0, hrow(w), hrow(a), hrow(b), hrow(k2), hrow(r), vt)
    o = jnp.transpose(ot, (0, 2, 1)).reshape(n, width)
    (out_b,) = _rows_call(_rwkv_step_post_kernel, [o, r, k2, v, gate, *prm],
                          [jax.ShapeDtypeStruct((n, width), BF16)], "rwkv_step_post")
    return out_b, new_wkv


def _mamba_post(y, xs, z, dexp, ng):
    y = (y + dexp * xs) * _silu(z)
    gw = y.shape[1] // SSM_GROUPS
    parts = []
    for g in range(SSM_GROUPS):
        yg = y[:, g * gw:(g + 1) * gw]
        parts.append(yg * lax.rsqrt(jnp.mean(yg * yg, axis=-1, keepdims=True) + SSM_NORM_EPS))
    return jnp.concatenate(parts, axis=1) * ng


def _ssd_prompt_kernel(z_ref, xlo_ref, xhi_ref, dt_ref, conv0_ref, s0_ref, cw_ref, cb_ref, dtb_ref,
                       alog_ref, dexp_ref, ng_ref, hexp_ref, qexp_ref, y_ref, nconv_ref, nssm_ref,
                       ext_sc, s_sc, y_sc, yo_sc, *, q):
    inner = xlo_ref.shape[1]
    n_heads = s_sc.shape[0]
    hpg = n_heads // SSM_GROUPS

    @pl.when(pl.program_id(1) == 0)
    def _():
        ext_sc[0:SUBLANES, :] = jnp.zeros((SUBLANES, ext_sc.shape[1]), F32)
        ext_sc[SUBLANES - 3:SUBLANES, :] = conv0_ref[0]
        s_sc[...] = s0_ref[0]

    ext_sc[SUBLANES:SUBLANES + q, 0:inner] = xlo_ref[...].astype(F32)
    ext_sc[SUBLANES:SUBLANES + q, inner:] = xhi_ref[...].astype(F32)
    xbc = _silu(_conv4(ext_sc[SUBLANES:SUBLANES + q, :], ext_sc[SUBLANES - 1:SUBLANES - 1 + q, :],
                       ext_sc[SUBLANES - 2:SUBLANES - 2 + q, :], ext_sc[SUBLANES - 3:SUBLANES - 3 + q, :],
                       cw_ref[...], cb_ref[...]))
    tail = ext_sc[q + SUBLANES - 3:q + SUBLANES, :]
    ext_sc[SUBLANES - 3:SUBLANES, :] = tail
    nconv_ref[0] = tail

    xs = xbc[:, 0:inner]
    gn = SSM_GROUPS * SSM_STATE
    bm = xbc[:, inner:inner + gn].astype(BF16)
    cm = xbc[:, inner + gn:].astype(BF16)
    dt = _softplus(dt_ref[...] + dtb_ref[...])
    dta = dt * (-jnp.exp(alog_ref[...]))
    ti = lax.broadcasted_iota(jnp.int32, (q, q), 0)
    si = lax.broadcasted_iota(jnp.int32, (q, q), 1)
    causal = ti >= si
    da = _dot_exact_lhs(causal.astype(BF16), dta)
    upper = (ti <= si).astype(BF16)
    da_t = sum(_dg(part, upper, _TN) for part in _split3(dta))
    da_end = da[q - 1:q, :]
    end_decay = jnp.exp(da_end)
    hexp, qexp = hexp_ref[...], qexp_ref[...]
    xdt = xs * _dot_exact_rhs(dt, hexp, 2)
    xdt_b = xdt.astype(BF16)
    xdt_end = (xdt * _dot_exact_rhs(jnp.exp(da_end - da), hexp, 2)).astype(BF16)
    from_start = _dot_exact_rhs(jnp.exp(da), hexp, 2)
    da_col = _dot_exact_rhs(da, qexp)

    heads = range(n_heads)
    bgs = [bm[:, g * SSM_STATE:(g + 1) * SSM_STATE] for g in range(SSM_GROUPS)]
    cgs = [cm[:, g * SSM_STATE:(g + 1) * SSM_STATE] for g in range(SSM_GROUPS)]
    scores = [_dot_nt(cgs[g], bgs[g]) for g in range(SSM_GROUPS)]
    hss = [slice(h * SSM_HEAD, (h + 1) * SSM_HEAD) for h in heads]
    s_hs = [s_sc[h] for h in heads]
    for h in heads:
        yo_sc[:, hss[h]] = _dot_nt(cgs[h // hpg], s_hs[h].astype(BF16))
    wts = [(scores[h // hpg]
            * jnp.exp(jnp.where(causal, da_col[:, h * q:(h + 1) * q] - da_t[h:h + 1, :], -jnp.inf))
            ).astype(BF16) for h in heads]
    for h in heads:
        y_sc[:, hss[h]] = _dot(wts[h], xdt_b[:, hss[h]])
    upd = [_dot_tn(xdt_end[:, hss[h]], bgs[h // hpg]) for h in heads]
    for h in heads:
        s_sc[h] = s_hs[h] * end_decay[:, h:h + 1] + upd[h]

    @pl.when(pl.program_id(1) == pl.num_programs(1) - 1)
    def _():
        nssm_ref[0] = s_sc[...]

    y = y_sc[...] + yo_sc[...] * from_start
    y_ref[...] = _mamba_post(y, xs, z_ref[...].astype(F32), dexp_ref[...], ng_ref[...]).astype(y_ref.dtype)


def ssd_prompt(proj1, dt_raw, n_batch, seq, conv0, ssm0, mp):
    n_heads, hd, n_state = ssm0.shape[1:]
    inner = n_heads * hd
    conv_ch = conv0.shape[-1]
    q = SSM_CHUNK if seq % SSM_CHUNK == 0 else seq
    n_chunks = seq // q
    full = lambda shape: pl.BlockSpec(shape, lambda b, c: (0,) * len(shape))
    return pl.pallas_call(
        functools.partial(_ssd_prompt_kernel, q=q),
        grid=(n_batch, n_chunks),
        in_specs=[pl.BlockSpec((q, inner), lambda b, c: (b * n_chunks + c, 0)),
                  pl.BlockSpec((q, inner), lambda b, c: (b * n_chunks + c, 1)),
                  pl.BlockSpec((q, inner), lambda b, c: (b * n_chunks + c, 2)),
                  pl.BlockSpec((q, n_heads), lambda b, c: (b * n_chunks + c, 0)),
                  pl.BlockSpec((1, 3, conv_ch), lambda b, c: (b, 0, 0)),
                  pl.BlockSpec((1, n_heads, hd, n_state), lambda b, c: (b, 0, 0, 0)),
                  full((CONV_W, conv_ch)), full((1, conv_ch)), full((1, n_heads)), full((1, n_heads)),
                  full((1, inner)), full((1, inner)), full((n_heads, inner)), full((n_heads, n_heads * q))],
        out_specs=[pl.BlockSpec((q, inner), lambda b, c: (b * n_chunks + c, 0)),
                   pl.BlockSpec((1, 3, conv_ch), lambda b, c: (b, 0, 0)),
                   pl.BlockSpec((1, n_heads, hd, n_state), lambda b, c: (b, 0, 0, 0))],
        out_shape=[jax.ShapeDtypeStruct((n_batch * seq, inner), BF16),
                   jax.ShapeDtypeStruct((n_batch, 3, conv_ch), F32),
                   jax.ShapeDtypeStruct(ssm0.shape, F32)],
        scratch_shapes=[pltpu.VMEM((q + SUBLANES, conv_ch), F32), pltpu.VMEM((n_heads, hd, n_state), F32),
                        pltpu.VMEM((q, inner), F32), pltpu.VMEM((q, inner), F32)],
        compiler_params=_cparams(2),
        name="ssd_prompt",
    )(proj1, proj1, proj1, dt_raw, conv0, ssm0, mp["cw"], mp["cb"], mp["dtb"], mp["alog"], mp["dexp"],
      mp["ng"], jnp.repeat(jnp.eye(n_heads, dtype=BF16), hd, axis=1),
      jnp.repeat(jnp.eye(n_heads, dtype=BF16), q, axis=1))


def _ssd_step_pre_kernel(x_ref, dt_ref, buf_ref, cw_ref, cb_ref, dtb_ref, alog_ref,
                         xs_ref, b_ref, c_ref, dt_out_ref, dec_ref, nbuf_ref):
    ch = x_ref.shape[1]
    inner = xs_ref.shape[1]
    gn = b_ref.shape[1]
    u = x_ref[...].astype(F32)
    b0, b1, b2 = (buf_ref[:, k * ch:(k + 1) * ch] for k in range(3))
    xbc = _silu(_conv4(u, b2, b1, b0, cw_ref[...], cb_ref[...]))
    xs_ref[...] = xbc[:, 0:inner]
    b_ref[...] = xbc[:, inner:inner + gn]
    c_ref[...] = xbc[:, inner + gn:]
    dt = _softplus(dt_ref[...] + dtb_ref[...])
    dt_out_ref[...] = dt
    dec_ref[...] = jnp.exp(dt * (-jnp.exp(alog_ref[...])))
    nbuf_ref[:, 0:ch] = b1
    nbuf_ref[:, ch:2 * ch] = b2
    nbuf_ref[:, 2 * ch:3 * ch] = u


def _ssd_step_kernel(s_ref, xt_ref, b_ref, c_ref, dt_ref, dec_ref, ns_ref, yt_ref):
    bt, n_heads = s_ref.shape[0], s_ref.shape[1]
    hpg = n_heads // SSM_GROUPS
    lane = lax.broadcasted_iota(jnp.int32, yt_ref.shape[1:], 1)
    for i in range(bt):
        xt = xt_ref[i]
        dt = dt_ref[i]
        dec = dec_ref[i]
        xdt = xt * dt
        s_news = [s_ref[i, h] * dec[:, h:h + 1] + xdt[:, h:h + 1] * b_ref[i, h // hpg]
                  for h in range(n_heads)]
        for h in range(n_heads):
            ns_ref[i, h] = s_news[h]
        ys = [jnp.sum(s_news[h] * c_ref[i, h // hpg], axis=-1, keepdims=True) for h in range(n_heads)]
        yt = jnp.zeros(yt_ref.shape[1:], F32)
        for h in range(n_heads):
            yt = jnp.where(lane == h, ys[h], yt)
        yt_ref[i] = yt


def _ssd_step_post_kernel(y_ref, xs_ref, z_ref, dexp_ref, ng_ref, o_ref):
    o_ref[...] = _mamba_post(y_ref[...], xs_ref[...], z_ref[...].astype(F32), dexp_ref[...],
                             ng_ref[...]).astype(o_ref.dtype)


def ssd_step(z, xbc_raw, dt_raw, conv_buf, ssm0, mp):
    n, conv_ch = xbc_raw.shape
    n_heads, hd, n_state = ssm0.shape[1:]
    inner = n_heads * hd
    gn = SSM_GROUPS * n_state
    xs, bm, cm, dt, dec, nbuf = _rows_call(
        _ssd_step_pre_kernel,
        [xbc_raw, dt_raw, conv_buf.reshape(n, 3 * conv_ch), mp["cw"], mp["cb"], mp["dtb"], mp["alog"]],
        [jax.ShapeDtypeStruct((n, inner), F32), jax.ShapeDtypeStruct((n, gn), F32),
         jax.ShapeDtypeStruct((n, gn), F32), jax.ShapeDtypeStruct((n, n_heads), F32),
         jax.ShapeDtypeStruct((n, n_heads), F32), jax.ShapeDtypeStruct((n, 3 * conv_ch), F32)],
        "ssd_step_pre")
    xt = jnp.transpose(xs.reshape(n, n_heads, hd), (0, 2, 1))
    bt = 4
    st_spec = pl.BlockSpec((bt, n_heads, hd, n_state), lambda i: (i, 0, 0, 0))
    t_spec = pl.BlockSpec((bt, hd, n_heads), lambda i: (i, 0, 0))
    g_spec = pl.BlockSpec((bt, SSM_GROUPS, 1, n_state), lambda i: (i, 0, 0, 0))
    h_spec = pl.BlockSpec((bt, 1, n_heads), lambda i: (i, 0, 0))
    new_ssm, yt = pl.pallas_call(
        _ssd_step_kernel,
        grid=(n // bt,),
        in_specs=[st_spec, t_spec, g_spec, g_spec, h_spec, h_spec],
        out_specs=[st_spec, t_spec],
        out_shape=[jax.ShapeDtypeStruct(ssm0.shape, F32), jax.ShapeDtypeStruct((n, hd, n_heads), F32)],
        compiler_params=_cparams(1),
        name="ssd_step",
    )(ssm0, xt, bm.reshape(n, SSM_GROUPS, 1, n_state), cm.reshape(n, SSM_GROUPS, 1, n_state),
      dt.reshape(n, 1, n_heads), dec.reshape(n, 1, n_heads))
    y = jnp.transpose(yt, (0, 2, 1)).reshape(n, inner)
    (y,) = _rows_call(_ssd_step_post_kernel, [y, xs, z, mp["dexp"], mp["ng"]],
                      [jax.ShapeDtypeStruct((n, inner), BF16)], "ssd_step_post")
    return y, nbuf, new_ssm


def _router_kernel(x_ref, cnt0_ref, g_ref, rwt_ref, rb_ref, xn_ref, idx_ref, gate_ref, rank_ref, cnt_ref,
                   cnt_sc):
    tm = x_ref.shape[0]

    @pl.when(pl.program_id(0) == 0)
    def _():
        cnt_sc[...] = cnt0_ref[...].astype(F32)

    xn = _rms(x_ref[...], g_ref[...], NORM_EPS)
    xn_ref[...] = xn
    logits = _dot_nt(rwt_ref[...], xn, HIGHEST) + rb_ref[...]
    e_iota = lax.broadcasted_iota(jnp.int32, logits.shape, 0)
    m1 = jnp.max(logits, axis=0, keepdims=True)
    i1 = jnp.min(jnp.where(logits == m1, e_iota, N_EXPERTS), axis=0, keepdims=True)
    rest = jnp.where(e_iota == i1, -jnp.inf, logits)
    m2 = jnp.max(rest, axis=0, keepdims=True)
    i2 = jnp.min(jnp.where(rest == m2, e_iota, N_EXPERTS), axis=0, keepdims=True)
    e2 = jnp.exp(m2 - m1)
    denom = 1.0 + e2
    idx_ref[...] = jnp.concatenate([i1, i2], axis=0)
    gate_ref[...] = jnp.concatenate([1.0 / denom, e2 / denom], axis=0)
    oh1 = (e_iota == i1).astype(F32)
    oh2 = (e_iota == i2).astype(F32)
    oh = oh1 + oh2
    ti = lax.broadcasted_iota(jnp.int32, (tm, tm), 0)
    si = lax.broadcasted_iota(jnp.int32, (tm, tm), 1)
    before = _dot(oh.astype(BF16), (ti < si).astype(BF16)) + cnt_sc[:, 0:1]
    rank_ref[...] = jnp.concatenate(
        [jnp.sum(oh1 * before, axis=0, keepdims=True), jnp.sum(oh2 * before, axis=0, keepdims=True)],
        axis=0).astype(jnp.int32)
    cnt_sc[...] = cnt_sc[...] + jnp.sum(oh, axis=1, keepdims=True)
    cnt_ref[...] = cnt_sc[...].astype(jnp.int32)


def moe_router(x, counts0, g, router_w, router_b, tm):
    t_rows, d = x.shape
    return pl.pallas_call(
        _router_kernel,
        grid=(t_rows // tm,),
        in_specs=[pl.BlockSpec((tm, d), lambda i: (i, 0)),
                  pl.BlockSpec((N_EXPERTS, LANES), lambda i: (0, 0)),
                  pl.BlockSpec((1, d), lambda i: (0, 0)),
                  pl.BlockSpec((N_EXPERTS, d), lambda i: (0, 0)),
                  pl.BlockSpec((N_EXPERTS, 1), lambda i: (0, 0))],
        out_specs=[pl.BlockSpec((tm, d), lambda i: (i, 0)),
                   pl.BlockSpec((2, tm), lambda i: (0, i)),
                   pl.BlockSpec((2, tm), lambda i: (0, i)),
                   pl.BlockSpec((2, tm), lambda i: (0, i)),
                   pl.BlockSpec((N_EXPERTS, LANES), lambda i: (0, 0))],
        out_shape=[jax.ShapeDtypeStruct((t_rows, d), F32),
                   jax.ShapeDtypeStruct((2, t_rows), jnp.int32),
                   jax.ShapeDtypeStruct((2, t_rows), F32),
                   jax.ShapeDtypeStruct((2, t_rows), jnp.int32),
                   jax.ShapeDtypeStruct((N_EXPERTS, LANES), jnp.int32)],
        scratch_shapes=[pltpu.VMEM((N_EXPERTS, LANES), F32)],
        compiler_params=_cparams(1),
        name="moe_router",
    )(x, counts0, g.reshape(1, d), router_w.T, router_b.reshape(N_EXPERTS, 1))


def _row_copy(src_ref, src_row, dst_ref, dst_row, sem):
    return pltpu.make_async_copy(src_ref.at[pl.ds(src_row, 1), :], dst_ref.at[pl.ds(dst_row, 1), :], sem)


def _dispatch_kernel(dest_ref, x_ref, slots_in_ref, slots_ref, sem, *, t_rows, row0):
    del slots_in_ref
    tm = x_ref.shape[0]
    base = row0 + pl.program_id(0) * tm

    def copies(r):
        return [_row_copy(x_ref, r, slots_ref, dest_ref[k * t_rows + base + r], sem) for k in range(2)]

    def start(r, carry):
        for k, cp in enumerate(copies(r)):
            cp.start(priority=k)
        return carry

    lax.fori_loop(0, tm, start, 0, unroll=DMA_UNROLL)
    for _ in range(2):
        pltpu.make_async_copy(x_ref, slots_ref.at[pl.ds(0, tm), :], sem).wait()


def moe_dispatch(xn, dest_flat, slots, row0, tm):
    n_rows, d = xn.shape
    return pl.pallas_call(
        functools.partial(_dispatch_kernel, t_rows=dest_flat.shape[0] // 2, row0=row0),
        grid_spec=pltpu.PrefetchScalarGridSpec(
            num_scalar_prefetch=1,
            grid=(n_rows // tm,),
            in_specs=[pl.BlockSpec((tm, d), lambda i, dest: (i, 0)),
                      pl.BlockSpec(memory_space=pl.ANY)],
            out_specs=pl.BlockSpec(memory_space=pl.ANY),
            scratch_shapes=[pltpu.SemaphoreType.DMA(())]),
        out_shape=jax.ShapeDtypeStruct(slots.shape, slots.dtype),
        input_output_aliases={2: 0},
        compiler_params=_cparams(1),
        name="moe_dispatch",
    )(dest_flat, xn, slots)


def _moe_kernel(te_ref, tv_ref, x_ref, wg_ref, wu_ref, wd_ref, o_ref, xb_sc, acc_sc):
    i, j = pl.program_id(0), pl.program_id(1)
    tm = x_ref.shape[0]

    @pl.when(j == 0)
    def _():
        xb_sc[...] = x_ref[...].astype(BF16)
        acc_sc[...] = jnp.zeros_like(acc_sc)

    def swiglu_rows(rows):
        xb = xb_sc[0:rows, :]
        hg = _dot(xb, wg_ref[0].astype(BF16))
        hu = _dot(xb, wu_ref[0].astype(BF16))
        h = (_silu(hg) * hu).astype(BF16)
        acc_sc[0:rows, :] += _dot(h, wd_ref[0].astype(BF16))

    pl.when(tv_ref[i] == 2)(functools.partial(swiglu_rows, tm))
    pl.when(tv_ref[i] == 1)(functools.partial(swiglu_rows, tm // 2))

    @pl.when(j == pl.num_programs(1) - 1)
    def _():
        o_ref[...] = acc_sc[...]


def moe_experts(slots, tile_expert, tile_valid, wg, wu, wd, tm, tf):
    n_slots, d = slots.shape
    d_ff = wg.shape[2]
    n_f = d_ff // tf

    def f_idx(i, j, te, tv):
        return jnp.where(tv[i] > 0, j, n_f - 1)

    return pl.pallas_call(
        _moe_kernel,
        grid_spec=pltpu.PrefetchScalarGridSpec(
            num_scalar_prefetch=2,
            grid=(n_slots // tm, n_f),
            in_specs=[pl.BlockSpec((tm, d), lambda i, j, te, tv: (i, 0)),
                      pl.BlockSpec((1, d, tf), lambda i, j, te, tv: (te[i], 0, f_idx(i, j, te, tv))),
                      pl.BlockSpec((1, d, tf), lambda i, j, te, tv: (te[i], 0, f_idx(i, j, te, tv))),
                      pl.BlockSpec((1, tf, d), lambda i, j, te, tv: (te[i], f_idx(i, j, te, tv), 0))],
            out_specs=pl.BlockSpec((tm, d), lambda i, j, te, tv: (i, 0)),
            scratch_shapes=[pltpu.VMEM((tm, d), BF16), pltpu.VMEM((tm, d), F32)]),
        out_shape=jax.ShapeDtypeStruct((n_slots, d), F32),
        compiler_params=_cparams(2),
        name="moe_experts",
    )(tile_expert, tile_valid, slots, wg, wu, wd)


def _combine_kernel(dest_ref, x_ref, gates_ref, g_ref, y_hbm_ref, o_ref, buf_sc, sem, *, t_rows, row0):
    tm = x_ref.shape[0]
    base = row0 + pl.program_id(0) * tm

    def copies(r):
        return [_row_copy(y_hbm_ref, dest_ref[k * t_rows + base + r], buf_sc.at[k], r, sem)
                for k in range(2)]

    def start(r, carry):
        for k, cp in enumerate(copies(r)):
            cp.start(priority=k)
        return carry

    lax.fori_loop(0, tm, start, 0, unroll=DMA_UNROLL)
    for k in range(2):
        pltpu.make_async_copy(y_hbm_ref.at[pl.ds(0, tm), :], buf_sc.at[k], sem).wait()
    gates = gates_ref[...]
    out = x_ref[...] + (gates[:, 0:1] * buf_sc[0] + gates[:, 1:2] * buf_sc[1])
    o_ref[...] = _rms(out, g_ref[...], NORM_EPS)


def moe_combine(x, gates_col, dest_flat, y_slots, g_final, tm, row0):
    n_rows, d = x.shape
    t_rows = gates_col.shape[0]
    assert row0 % tm == 0 and n_rows % tm == 0
    blk0 = row0 // tm
    return pl.pallas_call(
        functools.partial(_combine_kernel, t_rows=t_rows, row0=row0),
        grid_spec=pltpu.PrefetchScalarGridSpec(
            num_scalar_prefetch=1,
            grid=(n_rows // tm,),
            in_specs=[pl.BlockSpec((tm, d), lambda i, dest: (i, 0)),
                      pl.BlockSpec((tm, 2), lambda i, dest: (i + blk0, 0)),
                      pl.BlockSpec((1, d), lambda i, dest: (0, 0)),
                      pl.BlockSpec(memory_space=pl.ANY)],
            out_specs=pl.BlockSpec((tm, d), lambda i, dest: (i, 0)),
            scratch_shapes=[pltpu.VMEM((2, tm, d), F32), pltpu.SemaphoreType.DMA(())]),
        out_shape=jax.ShapeDtypeStruct((n_rows, d), F32),
        compiler_params=_cparams(1),
        name="moe_combine",
    )(dest_flat, x, gates_col, g_final.reshape(1, d), y_slots)


def moe_final(x_prompt, x_sample, g_ffn, g_final, router_w, router_b, wg, wu, wd):
    n_prompt, n_sample = x_prompt.shape[0], x_sample.shape[0]
    t_rows = n_prompt + n_sample
    route = lambda x, counts0: moe_router(x, counts0, g_ffn, router_w, router_b,
                                          _row_tile(x.shape[0], ROUTER_TILE, LANES))
    xn_p, idx_p, gates_p, rank_p, counts_p = route(x_prompt, jnp.zeros((N_EXPERTS, LANES), jnp.int32))
    xn_s, idx_s, gates_s, rank_s, counts = route(x_sample, counts_p)
    idx, gates, rank = (jnp.concatenate(pair, axis=1) for pair in
                        ((idx_p, idx_s), (gates_p, gates_s), (rank_p, rank_s)))
    counts = counts[:, 0]
    tm = MOE_TILE
    n_tiles = -(-2 * t_rows // tm) + N_EXPERTS
    padded = (counts + tm - 1) // tm * tm
    pend = jnp.cumsum(padded)
    pstart = pend - padded
    dest = jnp.sum(jnp.where(idx[:, :, None] == jnp.arange(N_EXPERTS)[None, None, :], pstart[None, None, :], 0),
                   axis=-1) + rank
    dest_flat = dest.reshape(-1).astype(jnp.int32)
    tile_start = jnp.arange(n_tiles, dtype=jnp.int32) * tm
    last_start = jnp.maximum(pend[-1] - tm, 0)
    probe = jnp.minimum(tile_start, last_start)
    tile_expert = jnp.minimum(jnp.sum((pend[None, :] <= probe[:, None]).astype(jnp.int32), axis=1),
                              N_EXPERTS - 1)
    onehot_e = tile_expert[:, None] == jnp.arange(N_EXPERTS)[None, :]
    group_end = jnp.sum(jnp.where(onehot_e, (pstart + counts)[None, :], 0), axis=1)
    live_rows = jnp.clip(group_end - tile_start, 0, tm)
    half = tm // 2
    tile_valid = jnp.where(tile_start < pend[-1], (live_rows + half - 1) // half, 0).astype(jnp.int32)
    slots = jnp.zeros((n_tiles * tm, x_prompt.shape[1]), F32)
    slots = moe_dispatch(xn_p, dest_flat, slots, 0, _row_tile(n_prompt, GATHER_TILE))
    slots = moe_dispatch(xn_s, dest_flat, slots, n_prompt, _row_tile(n_sample, GATHER_TILE))
    y_slots = moe_experts(slots, tile_expert, tile_valid, wg, wu, wd, tm, MOE_FF_TILE)
    gates_col = gates.T
    y_prompt = moe_combine(x_prompt, gates_col, dest_flat, y_slots, g_final,
                           _row_tile(n_prompt, COMBINE_TILE), 0)
    y_sample = moe_combine(x_sample, gates_col, dest_flat, y_slots, g_final, n_sample, n_prompt)
    return y_prompt, y_sample


def _block_diag(w):
    h, i, j = w.shape
    eye = jnp.eye(h, dtype=w.dtype)
    return jnp.einsum("hij,hg->higj", w, eye).reshape(h * i, h * j)


def kernel(x_prompt, x_sample, state_lru_conv, state_lru_h, state_rwkv_shift, state_rwkv_wkv, state_ssm_conv, state_ssm, norm_mix, norm_ffn, norm_final, w_in0, lru_conv_w, lru_conv_b, lru_wa, lru_ba, lru_wx, lru_bx, lru_lambda, rwkv_mu, rwkv_w0, rwkv_w_decay_up, rwkv_a0, rwkv_w_iclr_up, rwkv_w_gate_up, rwkv_k_k, rwkv_k_a, rwkv_r_k, rwkv_ln_w, rwkv_ln_b, w_out0, ffn_wg, ffn_wu, ffn_wd, w_in1, ssm_conv_w, ssm_conv_b, ssm_dt_bias, ssm_a_log, ssm_d, ssm_norm_g, w_out1, router_w, router_b, moe_wg, moe_wu, moe_wd):
    nb, seq, d = x_prompt.shape
    ns = x_sample.shape[0]
    tp = nb * seq
    lru_w = lru_conv_w.shape[-1]
    rw_w = rwkv_w0.shape[-1]
    shift_cols = rwkv_mu.shape[-1]
    n_rheads = rw_w // RWKV_HEAD
    inner = ssm_norm_g.shape[-1]
    n_sheads = ssm_a_log.shape[-1]
    conv_ch = ssm_conv_w.shape[-1]

    xp, xs = x_prompt.reshape(tp, d), x_sample.reshape(ns, d)
    tmp, tms = (_row_tile(n, TOKEN_TILE, 2 * SUBLANES) for n in (tp, ns))

    row = lambda v: v.reshape(1, -1)
    lp = dict(cw=lru_conv_w[0], cb=row(lru_conv_b[0]), wa=_block_diag(lru_wa[0]).astype(BF16),
              ba=row(lru_ba[0]), wx=_block_diag(lru_wx[0]).astype(BF16), bx=row(lru_bx[0]),
              lam=row(lru_lambda[0]))
    hsum = _block_diag(jnp.ones((n_rheads, RWKV_HEAD, RWKV_HEAD), BF16))
    rp = dict(mu=row(rwkv_mu[0]), w0=row(rwkv_w0[0]), wdec=rwkv_w_decay_up[0], a0=row(rwkv_a0[0]),
              wiclr=rwkv_w_iclr_up[0], wgate=rwkv_w_gate_up[0], kk=row(rwkv_k_k[0]), ka=row(rwkv_k_a[0]),
              rk=row(rwkv_r_k[0]), lnw=row(rwkv_ln_w[0]), lnb=row(rwkv_ln_b[0]), hsum=hsum)
    mp = dict(cw=ssm_conv_w[0], cb=row(ssm_conv_b[0]), dtb=row(ssm_dt_bias[0]), alog=row(ssm_a_log[0]),
              dexp=row(jnp.repeat(ssm_d[0], SSM_HEAD)), ng=row(ssm_norm_g[0]))

    w_in0_rwkv = w_in0[0][:, 2 * lru_w:]

    def in0(x, tm):
        return (norm_matmul(x, norm_mix[0], w_in0[0], 0, 1, 2 * lru_w, tm, "in0_lru"),
                norm_matmul(x, norm_mix[0], w_in0_rwkv, 0, 1, shift_cols, tm, "in0_rwkv"))

    proj_lru_p, proj_rwkv_p = in0(xp, tmp)
    proj_lru_s, s_p_rwkv = in0(xs, tms)
    zeros = lambda *shape: jnp.zeros(shape, F32)
    out_a_p, p_lru_conv, p_lru_h = lru_prompt(proj_lru_p, nb, seq, zeros(nb, 3, lru_w), zeros(nb, lru_w), lp)
    out_b_p, p_shift, p_wkv = rwkv_prompt(proj_rwkv_p, nb, seq, zeros(nb, shift_cols),
                                          zeros(nb, n_rheads, RWKV_HEAD, RWKV_HEAD), rp)
    out_a_s, s_lru_conv, s_lru_h = lru_step(proj_lru_s[:, :lru_w], proj_lru_s[:, lru_w:], state_lru_conv[0],
                                            state_lru_h[0], lp)
    out_b_s, s_wkv = rwkv_step(s_p_rwkv, state_rwkv_shift[0], state_rwkv_wkv[0], rp)

    def mix0_ffn(out_a, out_b, x, tm):
        x = matmul_residual([out_a, out_b], w_out0[0], x, tm, d, "out0")
        return ffn_residual(x, norm_ffn[0], ffn_wg[0], ffn_wu[0], ffn_wd[0], tm, FFN_FF_TILE)

    xp = mix0_ffn(out_a_p, out_b_p, xp, tmp)
    xs = mix0_ffn(out_a_s, out_b_s, xs, tms)

    tn1 = 1536
    w_in1_dt = w_in1[0][:, inner + conv_ch:]

    def in1(x, tm):
        return (norm_matmul(x, norm_mix[1], w_in1[0], 0, (inner + conv_ch) // tn1, tn1, tm, "in1_main", BF16),
                norm_matmul(x, norm_mix[1], w_in1_dt, 0, 1, n_sheads, tm, "in1_dt"))

    proj1_p, dt_p = in1(xp, _row_tile(tp, 2 * TOKEN_TILE, 2 * SUBLANES))
    proj1_s, dt_s = in1(xs, tms)
    y_mix_p, p_ssm_conv, p_ssm = ssd_prompt(proj1_p, dt_p, nb, seq, zeros(nb, 3, conv_ch),
                                            zeros(nb, n_sheads, SSM_HEAD, SSM_STATE), mp)
    y_mix_s, s_ssm_conv, s_ssm = ssd_step(proj1_s[:, :inner], proj1_s[:, inner:], dt_s,
                                          state_ssm_conv[0], state_ssm[0], mp)
    xp = matmul_residual([y_mix_p], w_out1[0], xp, tmp, d, "out1")
    xs = matmul_residual([y_mix_s], w_out1[0], xs, tms, d, "out1")
    y_p, y_s = moe_final(xp, xs, norm_ffn[1], norm_final, router_w[0], router_b[0], moe_wg[0], moe_wu[0],
                         moe_wd[0])

    return (y_p.reshape(nb, seq, d), y_s.reshape(ns, 1, d),
            p_lru_conv[None], p_lru_h.reshape(1, nb, lru_w), p_shift.reshape(1, nb, shift_cols), p_wkv[None],
            p_ssm_conv[None], p_ssm[None],
            s_lru_conv.reshape(1, ns, 3, lru_w), s_lru_h[None], s_p_rwkv[None], s_wkv[None],
            s_ssm_conv.reshape(1, ns, 3, conv_ch), s_ssm[None])
```
